```python
import math
import jax, jax.numpy as jnp
from jax import lax
import numpy as np

D_MODEL = 1024
BATCH = 4
SEQ = 4096
DEPTH = 1
DEC_BATCH = 128
DEC_SEQ = 4
PAST_LEN = 8192
PAGE_SIZE = 128

D_MIX = D_MODEL
D_ATT = D_MIX // 2
D_RWKV = D_MIX - D_ATT
DK_A = 64
H_A = D_ATT // (2 * DK_A)
DV_A = 2 * DK_A
HS_R = 64
H_R = D_RWKV // HS_R
R_DECAY = 64
R_ICL = 64
SHIFT_W = 3 * D_RWKV + R_DECAY + R_ICL
D_IN = 4 * D_ATT + SHIFT_W + D_RWKV
SPLITS = (D_ATT, 2 * D_ATT, 3 * D_ATT, 4 * D_ATT, 4 * D_ATT + SHIFT_W)
Q_BLOCK = 128
ROPE_THETA = 10000.0
RMS_EPS = 1e-6
GN_EPS = 64e-5
L2_EPS = 1e-12

kernel_name = "hymba_diffattn_rwkv7_step"

F32 = jnp.float32


def _lambda_init(layer):
    return 0.8 - 0.6 * math.exp(-0.3 * layer)


def _rms(x, g):
    xf = x.astype(F32)
    y = xf * lax.rsqrt(jnp.mean(xf * xf, -1, keepdims=True) + RMS_EPS)
    return (y * g.astype(F32)).astype(x.dtype)


def _rope(x, pos):
    half = DK_A // 2
    inv = 1.0 / (ROPE_THETA ** (jnp.arange(half, dtype=F32) / half))
    ang = pos.astype(F32)[:, None] * inv[None, :]
    cos = jnp.cos(ang)[None, :, None, None, :]
    sin = jnp.sin(ang)[None, :, None, None, :]
    xf = x.astype(F32)
    x1, x2 = xf[..., :half], xf[..., half:]
    return jnp.concatenate([x1 * cos - x2 * sin, x2 * cos + x1 * sin], -1).astype(x.dtype)


def _diff_qkv(q, k, v, pos, q_g, k_g):
    B, T = q.shape[:2]
    q = _rope(_rms(q.reshape(B, T, H_A, 2, DK_A), q_g), pos) * (DK_A ** -0.5)
    k = _rope(_rms(k.reshape(B, T, H_A, 2, DK_A), k_g), pos)
    v = v.reshape(B, T, H_A, DV_A)
    return q, k, v


def _attend_prompt(q, k, v):
    B, S = q.shape[:2]
    nb = S // Q_BLOCK
    qb = q.reshape(B, nb, Q_BLOCK, H_A, 2, DK_A).transpose(1, 0, 2, 3, 4, 5)
    kf = k.astype(F32)
    vf = v.astype(F32)
    k_pos = jnp.arange(S)

    def block(args):
        qi, i = args
        s = jnp.einsum('bthcd,bshcd->bhcts', qi.astype(F32), kf)
        q_pos = i * Q_BLOCK + jnp.arange(Q_BLOCK)
        s = jnp.where(k_pos[None, :] <= q_pos[:, None], s, -jnp.inf)
        p = jax.nn.softmax(s, -1)
        return jnp.einsum('bhcts,bshe->bthce', p, vf)

    o = lax.map(block, (qb, jnp.arange(nb)))
    return o.transpose(1, 0, 2, 3, 4, 5).reshape(B, S, H_A, 2, DV_A)


def _softmax_update(carry, s, vv):
    m, l, acc = carry
    m_new = jnp.maximum(m, s.max(-1))
    corr = jnp.exp(m - m_new)
    p = jnp.exp(s - m_new[..., None])
    l = l * corr + p.sum(-1)
    acc = acc * corr[..., None] + jnp.einsum('bhcts,bshe->bhcte', p, vv.astype(F32))
    return (m_new, l, acc)


def _attend_sample(q, k_new, v_new, cache_k, cache_v, page_table, layer):
    DB, T = q.shape[:2]
    qf = q.astype(F32)
    m0 = jnp.full((DB, H_A, 2, T), -jnp.inf, F32)
    l0 = jnp.zeros((DB, H_A, 2, T), F32)
    a0 = jnp.zeros((DB, H_A, 2, T, DV_A), F32)

    def page_step(carry, phys):
        kp = cache_k[layer, phys]
        vp = cache_v[layer, phys]
        s = jnp.einsum('bthcd,bshcd->bhcts', qf, kp.astype(F32))
        return _softmax_update(carry, s, vp), None

    carry, _ = lax.scan(page_step, (m0, l0, a0), page_table.T)
    s_new = jnp.einsum('bthcd,bshcd->bhcts', qf, k_new.astype(F32))
    causal = jnp.tril(jnp.ones((T, T), bool))
    s_new = jnp.where(causal, s_new, -jnp.inf)
    m, l, acc = _softmax_update(carry, s_new, v_new)
    o = acc / l[..., None]
    return o.transpose(0, 3, 1, 2, 4)


def _diff_lambda(lq1, lk1, lq2, lk2, lam_init):
    return (jnp.exp(jnp.sum(lq1.astype(F32) * lk1.astype(F32)))
            - jnp.exp(jnp.sum(lq2.astype(F32) * lk2.astype(F32))) + lam_init)


def _diff_combine(o, lam, lam_init, subln_g, z_a):
    y = o[:, :, :, 0] - lam * o[:, :, :, 1]
    y = _rms(y, subln_g) * (1.0 - lam_init)
    B, T = y.shape[:2]
    return y.reshape(B, T, D_ATT).astype(z_a.dtype) * jax.nn.silu(z_a)


def _rwkv(rw, z_r, shift0, S0, mu, w0, w_up, a0, a_up, k_k, k_a, r_k, gn_g, gn_b):
    B, T = rw.shape[:2]
    prev = jnp.concatenate([shift0[:, None].astype(rw.dtype), rw[:, :-1]], 1)
    u = rw + (prev - rw) * mu
    r, k, v, wd, ad = jnp.split(u, [D_RWKV, 2 * D_RWKV, 3 * D_RWKV, 3 * D_RWKV + R_DECAY], -1)
    w_log = -jax.nn.softplus(-(w0 + jnp.tanh(wd) @ w_up).astype(F32)) - 0.5
    decay = jnp.exp(-jnp.exp(w_log))
    a = jax.nn.sigmoid((a0 + ad @ a_up).astype(F32))

    def heads(t):
        return t.astype(F32).reshape(B, T, H_R, HS_R)

    r, k, v, decay, a = heads(r), heads(k), heads(v), heads(decay), heads(a)
    kk = k * k_k.astype(F32).reshape(H_R, HS_R)
    kk = kk / jnp.maximum(jnp.sqrt(jnp.sum(kk * kk, -1, keepdims=True)), L2_EPS)
    k = k * (1.0 + (a - 1.0) * k_a.astype(F32).reshape(H_R, HS_R))

    def step(S, inp):
        r_t, w_t, k_t, v_t, kk_t, a_t = inp
        sa = jnp.einsum('bhvk,bhk->bhv', S, -kk_t)
        S = (S * w_t[:, :, None, :] + sa[..., None] * (kk_t * a_t)[:, :, None, :]
             + v_t[..., None] * k_t[:, :, None, :])
        return S, jnp.einsum('bhvk,bhk->bhv', S, r_t)

    def tm(t):
        return jnp.swapaxes(t, 0, 1)

    S_fin, y = lax.scan(step, S0.astype(F32), (tm(r), tm(decay), tm(k), tm(v), tm(kk), tm(a)))
    y = tm(y)
    mean = jnp.mean(y, -1, keepdims=True)
    var = jnp.mean(jnp.square(y - mean), -1, keepdims=True)
    y = ((y - mean) * lax.rsqrt(var + GN_EPS) * gn_g.astype(F32).reshape(H_R, HS_R)
         + gn_b.astype(F32).reshape(H_R, HS_R))
    y = y + jnp.sum(r * k * r_k.astype(F32).reshape(H_R, HS_R), -1, keepdims=True) * v
    out = y.reshape(B, T, D_RWKV).astype(z_r.dtype) * jax.nn.silu(z_r)
    return out, S_fin, rw[:, -1]


def _layer(x, pos, shift0, S0, attend, lp, lam_init):
    h = _rms(x, lp['ln_g'])
    p = h @ lp['w_in']
    q, k, v, z_a, rw, z_r = jnp.split(p, SPLITS, -1)
    q, k, v = _diff_qkv(q, k, v, pos, lp['q_g'], lp['k_g'])
    o = attend(q, k, v)
    lam = _diff_lambda(lp['lq1'], lp['lk1'], lp['lq2'], lp['lk2'], lam_init)
    out_a = _diff_combine(o, lam, lam_init, lp['subln_g'], z_a)
    out_r, S_fin, shift_last = _rwkv(rw, z_r, shift0, S0, lp['mu'], lp['w0'], lp['w_up'],
                                     lp['a0'], lp['a_up'], lp['k_k'], lp['k_a'], lp['r_k'],
                                     lp['gn_g'], lp['gn_b'])
    y = x + jnp.concatenate([out_a, out_r], -1) @ lp['w_out']
    return y, k, v, S_fin, shift_last


def setup_inputs(seed: int = 0) -> dict:
    key = jax.random.key(seed)
    ks = jax.random.split(key, 32)
    n_pages = PAST_LEN // PAGE_SIZE
    n_used = DEC_BATCH * n_pages
    n_phys = n_used + max(1, n_used // 4)

    def nrm(k, shape, scale):
        return jax.random.normal(k, shape, F32) * scale

    page_table = jax.random.permutation(ks[0], n_phys)[:n_used].reshape(DEC_BATCH, n_pages).astype(jnp.int32)
    return {
        "x_prompt": nrm(ks[1], (BATCH, SEQ, D_MODEL), 1.0),
        "x_sample": nrm(ks[2], (DEC_BATCH, DEC_SEQ, D_MODEL), 1.0),
        "cache_k": nrm(ks[3], (DEPTH, n_phys, PAGE_SIZE, H_A, 2, DK_A), 1.0),
        "cache_v": nrm(ks[4], (DEPTH, n_phys, PAGE_SIZE, H_A, DV_A), 1.0),
        "state_wkv": nrm(ks[5], (DEPTH, DEC_BATCH, H_R, HS_R, HS_R), 0.3),
        "state_shift": nrm(ks[6], (DEPTH, DEC_BATCH, SHIFT_W), 1.0),
        "page_table": page_table,
        "ln_g": 1.0 + nrm(ks[7], (DEPTH, D_MODEL), 0.05),
        "w_in": nrm(ks[8], (DEPTH, D_MODEL, D_IN), D_MODEL ** -0.5),
        "q_norm_g": 1.0 + nrm(ks[9], (DEPTH, DK_A), 0.05),
        "k_norm_g": 1.0 + nrm(ks[10], (DEPTH, DK_A), 0.05),
        "lambda_q1": nrm(ks[11], (DEPTH, DK_A), 0.1),
        "lambda_k1": nrm(ks[12], (DEPTH, DK_A), 0.1),
        "lambda_q2": nrm(ks[13], (DEPTH, DK_A), 0.1),
        "lambda_k2": nrm(ks[14], (DEPTH, DK_A), 0.1),
        "subln_g": 1.0 + nrm(ks[15], (DEPTH, DV_A), 0.05),
        "shift_mu": jax.random.uniform(ks[16], (DEPTH, SHIFT_W), F32),
        "w0": jax.random.uniform(ks[17], (DEPTH, D_RWKV), F32, -6.0, 1.0),
        "w_lora_up": nrm(ks[18], (DEPTH, R_DECAY, D_RWKV), 0.5 * R_DECAY ** -0.5),
        "a0": nrm(ks[19], (DEPTH, D_RWKV), 0.1),
        "a_lora_up": nrm(ks[20], (DEPTH, R_ICL, D_RWKV), 0.5 * R_ICL ** -0.5),
        "k_k": 0.85 + nrm(ks[21], (DEPTH, D_RWKV), 0.05),
        "k_a": 1.0 + nrm(ks[22], (DEPTH, D_RWKV), 0.05),
        "r_k": nrm(ks[23], (DEPTH, D_RWKV), 0.1),
        "gn_g": 1.0 + nrm(ks[24], (DEPTH, D_RWKV), 0.05),
        "gn_b": nrm(ks[25], (DEPTH, D_RWKV), 0.02),
        "w_out": nrm(ks[26], (DEPTH, D_MIX, D_MODEL), D_MIX ** -0.5),
    }


def reference(x_prompt, x_sample, cache_k, cache_v, state_wkv, state_shift, page_table,
              ln_g, w_in, q_norm_g, k_norm_g, lambda_q1, lambda_k1, lambda_q2, lambda_k2,
              subln_g, shift_mu, w0, w_lora_up, a0, a_lora_up, k_k, k_a, r_k, gn_g, gn_b, w_out):
    B, S = x_prompt.shape[:2]
    DB, T = x_sample.shape[:2]
    past = page_table.shape[1] * cache_k.shape[2]
    pos_p = jnp.arange(S)
    pos_s = past + jnp.arange(T)
    yp, ys = x_prompt, x_sample
    kp_l, vp_l, sp_l, hp_l, ks_l, vs_l, ss_l, hs_l = [], [], [], [], [], [], [], []
    for layer in range(DEPTH):
        lp = {'ln_g': ln_g[layer], 'w_in': w_in[layer], 'q_g': q_norm_g[layer], 'k_g': k_norm_g[layer],
              'lq1': lambda_q1[layer], 'lk1': lambda_k1[layer], 'lq2': lambda_q2[layer],
              'lk2': lambda_k2[layer], 'subln_g': subln_g[layer], 'mu': shift_mu[layer],
              'w0': w0[layer], 'w_up': w_lora_up[layer], 'a0': a0[layer], 'a_up': a_lora_up[layer],
              'k_k': k_k[layer], 'k_a': k_a[layer], 'r_k': r_k[layer], 'gn_g': gn_g[layer],
              'gn_b': gn_b[layer], 'w_out': w_out[layer]}
        lam_init = _lambda_init(layer)
        shift0 = jnp.zeros((B, SHIFT_W), x_prompt.dtype)
        S0 = jnp.zeros((B, H_R, HS_R, HS_R), F32)
        yp, kp, vp, sp, hp = _layer(yp, pos_p, shift0, S0, _attend_prompt, lp, lam_init)
        attend_s = (lambda q, k, v, _l=layer:
                    _attend_sample(q, k, v, cache_k, cache_v, page_table, _l))
        ys, ksm, vsm, ssm, hsm = _layer(ys, pos_s, state_shift[layer], state_wkv[layer],
                                        attend_s, lp, lam_init)
        kp_l.append(kp); vp_l.append(vp); sp_l.append(sp); hp_l.append(hp)
        ks_l.append(ksm); vs_l.append(vsm); ss_l.append(ssm); hs_l.append(hsm)
    return (yp, ys,
            jnp.stack(kp_l), jnp.stack(vp_l), jnp.stack(sp_l), jnp.stack(hp_l),
            jnp.stack(ks_l), jnp.stack(vs_l), jnp.stack(ss_l), jnp.stack(hs_l))
```

```python
import functools
import math

import jax
import jax.numpy as jnp
from jax import lax
from jax.experimental import pallas as pl
from jax.experimental.pallas import tpu as pltpu

F32 = jnp.float32
BF16 = jnp.bfloat16

D_MODEL = 1024
D_ATT = 512
D_RWKV = 512
DK_A = 64
H_A = 4
DV_A = 128
HS_R = 64
H_R = 8
R_LORA = 64
SHIFT_W = 3 * D_RWKV + 2 * R_LORA
ROPE_THETA = 10000.0
RMS_EPS = 1e-6
GN_EPS = 64e-5
L2_EPS = 1e-12
LAM_INIT = 0.8 - 0.6 * math.exp(-0.3 * 0)

ROW_TILE = 256
SCAN_SEQS = 4
SCAN_CHUNK = 128
PAGES_PER_STEP = 8
VMEM_LIMIT = 48 * 1024 * 1024

_NT = (((1,), (1,)), ((), ()))


def _dot(a, b):
    return jnp.dot(a, b, preferred_element_type=F32)


def _segsum(x, bd):
    hi = x.astype(BF16)
    lo = (x - hi.astype(F32)).astype(BF16)
    return _dot(hi, bd) + _dot(lo, bd)


def _diff_lambda(lq1, lk1, lq2, lk2):
    return (jnp.exp(jnp.sum(lq1 * lk1, axis=1, keepdims=True))
            - jnp.exp(jnp.sum(lq2 * lk2, axis=1, keepdims=True)) + LAM_INIT)


def _subln_gate(o0, o1, lam, subg, gate):
    y = o0 - lam * o1
    y = y * lax.rsqrt(jnp.mean(y * y, axis=-1, keepdims=True) + RMS_EPS) * subg
    return (y * (1.0 - LAM_INIT)) * gate


def _proj_kernel(x_ref, lng_ref, wq_ref, wk_ref, wr_ref, cos_ref, sin_ref, qg_ref, kg_ref,
                 q_ref, kt_ref, ktb_ref, v_ref, vb_ref, ga_ref, rw_ref, gr_ref):
    tm = x_ref.shape[0]
    x = x_ref[...]
    h = x * lax.rsqrt(jnp.mean(x * x, axis=-1, keepdims=True) + RMS_EPS) * lng_ref[...]
    h = h.astype(BF16)
    cos = cos_ref[...][None]
    sin = sin_ref[...][None]

    def norm_rope(t, g):
        t3 = t.reshape(2 * H_A, DK_A, tm)
        t3 = t3 * lax.rsqrt(jnp.mean(t3 * t3, axis=1, keepdims=True) + RMS_EPS) * g[None]
        x1 = t3[:, :DK_A // 2]
        x2 = t3[:, DK_A // 2:]
        o = jnp.concatenate([x1 * cos - x2 * sin, x2 * cos + x1 * sin], axis=1)
        return o.reshape(D_ATT, tm)

    qt = lax.dot_general(wq_ref[...], h, _NT, preferred_element_type=F32)
    qt = norm_rope(qt, qg_ref[...]) * (DK_A ** -0.5)
    q_ref[...] = qt.T.astype(BF16)
    kt = lax.dot_general(wk_ref[...], h, _NT, preferred_element_type=F32)
    kt = norm_rope(kt, kg_ref[...])
    kt_ref[...] = kt
    ktb_ref[...] = kt.astype(BF16)

    v = _dot(h, wr_ref[:, 0:D_ATT])
    v_ref[...] = v
    vb_ref[...] = v.astype(BF16)
    za = _dot(h, wr_ref[:, D_ATT:2 * D_ATT])
    ga_ref[...] = za * jax.nn.sigmoid(za)
    rw_ref[...] = _dot(h, wr_ref[:, 2 * D_ATT:2 * D_ATT + SHIFT_W])
    zr = _dot(h, wr_ref[:, 2 * D_ATT + SHIFT_W:])
    gr_ref[...] = zr * jax.nn.sigmoid(zr)


def _proj(x, ln_g, wq_t, wk_t, w_rest, cos_t, sin_t, qg, kg, n_seq, seq_len):
    rows = x.shape[0]
    tm = ROW_TILE
    nsi = seq_len // tm
    const = lambda i: (0, 0)
    row_blk = lambda w: pl.BlockSpec((tm, w), lambda i: (i, 0))
    out_shape = (
        jax.ShapeDtypeStruct((rows, D_ATT), BF16),
        jax.ShapeDtypeStruct((n_seq, D_ATT, seq_len), F32),
        jax.ShapeDtypeStruct((n_seq, nsi, D_ATT, tm), BF16),
        jax.ShapeDtypeStruct((rows, D_ATT), F32),
        jax.ShapeDtypeStruct((rows, D_ATT), BF16),
        jax.ShapeDtypeStruct((rows, D_ATT), F32),
        jax.ShapeDtypeStruct((rows, SHIFT_W), F32),
        jax.ShapeDtypeStruct((rows, D_RWKV), F32),
    )
    out_specs = (
        row_blk(D_ATT),
        pl.BlockSpec((None, D_ATT, tm), lambda i: (i // nsi, 0, i % nsi)),
        pl.BlockSpec((None, None, D_ATT, tm), lambda i: (i // nsi, i % nsi, 0, 0)),
        row_blk(D_ATT), row_blk(D_ATT), row_blk(D_ATT), row_blk(SHIFT_W), row_blk(D_RWKV),
    )
    in_specs = [
        row_blk(D_MODEL),
        pl.BlockSpec((1, D_MODEL), const),
        pl.BlockSpec(wq_t.shape, const),
        pl.BlockSpec(wk_t.shape, const),
        pl.BlockSpec(w_rest.shape, const),
        pl.BlockSpec((DK_A // 2, tm), lambda i: (0, i)),
        pl.BlockSpec((DK_A // 2, tm), lambda i: (0, i)),
        pl.BlockSpec((DK_A, 1), const),
        pl.BlockSpec((DK_A, 1), const),
    ]
    return pl.pallas_call(
        _proj_kernel, out_shape=out_shape, grid=(rows // tm,), in_specs=in_specs, out_specs=out_specs,
        compiler_params=pltpu.CompilerParams(dimension_semantics=("parallel",), vmem_limit_bytes=VMEM_LIMIT),
        name="proj",
    )(x, ln_g, wq_t, wk_t, w_rest, cos_t, sin_t, qg, kg)


def _attn_prompt_kernel(q_ref, kt_ref, v_ref, ga_ref, lq1_ref, lk1_ref, lq2_ref, lk2_ref, subg_ref,
                        o_ref, m_sc, l_sc, acc_sc):
    tq = q_ref.shape[0]
    tk = kt_ref.shape[2]
    i = pl.program_id(2)
    q = q_ref[...]
    lane = lax.broadcasted_iota(jnp.int32, q.shape, 1)
    zero = jnp.zeros_like(q)
    qs = jnp.concatenate([jnp.where(lane < DK_A, q, zero), jnp.where(lane >= DK_A, q, zero)], axis=0)
    m_sc[...] = jnp.full(m_sc.shape, -jnp.inf, F32)
    l_sc[...] = jnp.zeros(l_sc.shape, F32)
    acc_sc[...] = jnp.zeros(acc_sc.shape, F32)

    def step(j, masked):
        kj = kt_ref[j]
        vj = v_ref[pl.ds(pl.multiple_of(j * tk, tk), tk), :]
        s = _dot(qs, kj)
        if masked:
            row = lax.broadcasted_iota(jnp.int32, s.shape, 0)
            col = lax.broadcasted_iota(jnp.int32, s.shape, 1)
            qpos = jnp.where(row >= tq, row - tq, row)
            s = jnp.where(col <= qpos, s, -jnp.inf)
        m_prev = m_sc[...]
        m_new = jnp.maximum(m_prev, jnp.max(s, axis=1, keepdims=True))
        alpha = jnp.exp(m_prev - m_new)
        p = jnp.exp(s - m_new)
        l_sc[...] = alpha * l_sc[...] + jnp.sum(p, axis=1, keepdims=True)
        acc_sc[...] = alpha * acc_sc[...] + _dot(p.astype(BF16), vj)
        m_sc[...] = m_new

    def body(j, c):
        step(j, False)
        return c

    lax.fori_loop(0, i, body, 0)
    step(i, True)

    o = acc_sc[...] / l_sc[...]
    lam = _diff_lambda(lq1_ref[...], lk1_ref[...], lq2_ref[...], lk2_ref[...])
    o_ref[...] = _subln_gate(o[:tq], o[tq:], lam, subg_ref[...], ga_ref[...]).astype(BF16)


def _attn_prompt(q_b, ktb, v_b, ga, lq1, lk1, lq2, lk2, subg, n_seq, seq_len):
    tq = ROW_TILE
    nq = seq_len // tq
    rows = n_seq * seq_len
    vec = lambda w: pl.BlockSpec((1, w), lambda b, h, i: (0, 0))
    in_specs = [
        pl.BlockSpec((tq, DV_A), lambda b, h, i: (b * nq + i, h)),
        pl.BlockSpec((None, nq, DV_A, tq), lambda b, h, i: (b, 0, h, 0)),
        pl.BlockSpec((seq_len, DV_A), lambda b, h, i: (b, h)),
        pl.BlockSpec((tq, DV_A), lambda b, h, i: (b * nq + i, h)),
        vec(DK_A), vec(DK_A), vec(DK_A), vec(DK_A), vec(DV_A),
    ]
    return pl.pallas_call(
        _attn_prompt_kernel,
        out_shape=jax.ShapeDtypeStruct((rows, D_ATT), BF16),
        grid=(n_seq, H_A, nq), in_specs=in_specs,
        out_specs=pl.BlockSpec((tq, DV_A), lambda b, h, i: (b * nq + i, h)),
        scratch_shapes=[pltpu.VMEM((2 * tq, 1), F32), pltpu.VMEM((2 * tq, 1), F32),
                        pltpu.VMEM((2 * tq, DV_A), F32)],
        compiler_params=pltpu.CompilerParams(dimension_semantics=("parallel", "parallel", "arbitrary"),
                                             vmem_limit_bytes=VMEM_LIMIT),
        name="attn_prompt",
    )(q_b, ktb, v_b, ga, lq1, lk1, lq2, lk2, subg)


def _attn_sample_kernel(pt_ref, q8_ref, kn_ref, vn_ref, ga_ref, lq1_ref, lk1_ref, lq2_ref, lk2_ref,
                        subg_ref, *refs):
    npg = PAGES_PER_STEP
    k_refs = refs[:npg]
    v_refs = refs[npg:2 * npg]
    o_ref, m_sc, l_sc, acc_sc = refs[2 * npg:]
    g = pl.program_id(1)
    n_dec = q8_ref.shape[0] // 2

    @pl.when(g == 0)
    def _():
        m_sc[...] = jnp.full(m_sc.shape, -jnp.inf, F32)
        l_sc[...] = jnp.zeros(l_sc.shape, F32)
        acc_sc[...] = jnp.zeros(acc_sc.shape, F32)

    rowi = lax.broadcasted_iota(jnp.int32, (2 * n_dec, DV_A), 0)
    lane = lax.broadcasted_iota(jnp.int32, (2 * n_dec, DV_A), 1)
    sel = ((rowi < n_dec) & (lane < DK_A)) | ((rowi >= n_dec) & (lane >= DK_A))
    q8 = q8_ref[...]

    def head_q(h):
        qh = q8[:, h * DV_A:(h + 1) * DV_A]
        return jnp.where(sel, qh, jnp.zeros_like(qh))

    for h in range(H_A):
        qh = head_q(h)
        s = jnp.concatenate(
            [_dot(qh, k_refs[i][h].reshape(2 * DK_A, -1).astype(BF16)) for i in range(npg)], axis=1)
        m_prev = m_sc[h]
        m_new = jnp.maximum(m_prev, jnp.max(s, axis=1, keepdims=True))
        alpha = jnp.exp(m_prev - m_new)
        p = jnp.exp(s - m_new)
        l_sc[h] = alpha * l_sc[h] + jnp.sum(p, axis=1, keepdims=True)
        pb = p.astype(BF16)
        acc = alpha * acc_sc[h]
        page = k_refs[0].shape[-1]
        for i in range(npg):
            acc = acc + _dot(pb[:, i * page:(i + 1) * page], v_refs[i][:, h, :].astype(BF16))
        acc_sc[h] = acc
        m_sc[h] = m_new

    @pl.when(g == pl.num_programs(1) - 1)
    def _():
        lam = _diff_lambda(lq1_ref[...], lk1_ref[...], lq2_ref[...], lk2_ref[...])
        t_idx = jnp.where(rowi[:, :1] >= n_dec, rowi[:, :1] - n_dec, rowi[:, :1])
        for h in range(H_A):
            hs = slice(h * DV_A, (h + 1) * DV_A)
            qf = head_q(h).astype(F32)
            s_new = []
            for j in range(n_dec):
                sj = jnp.sum(qf * kn_ref[j:j + 1, hs], axis=1, keepdims=True)
                s_new.append(jnp.where(j <= t_idx, sj, -jnp.inf))
            m_prev = m_sc[h]
            m_new = m_prev
            for sj in s_new:
                m_new = jnp.maximum(m_new, sj)
            alpha = jnp.exp(m_prev - m_new)
            l = alpha * l_sc[h]
            acc = alpha * acc_sc[h]
            for j, sj in enumerate(s_new):
                pj = jnp.exp(sj - m_new)
                l = l + pj
                acc = acc + pj * vn_ref[j:j + 1, hs]
            o = acc / l
            o_ref[:, hs] = _subln_gate(o[:n_dec], o[n_dec:], lam, subg_ref[...], ga_ref[:, hs])


def _attn_sample(page_table, q8, k_new, v_new, ga, lq1, lk1, lq2, lk2, subg, kt_pages, v_pages):
    n_b, n_pages = page_table.shape
    n_dec = k_new.shape[1]
    npg = PAGES_PER_STEP
    page = kt_pages.shape[-1]
    vec = lambda w: pl.BlockSpec((1, w), lambda b, g, pt: (0, 0))
    per_b = lambda r: pl.BlockSpec((None, r, D_ATT), lambda b, g, pt: (b, 0, 0))

    def k_spec(i):
        return pl.BlockSpec((None, H_A, 2, DK_A, page), lambda b, g, pt: (pt[b, g * npg + i], 0, 0, 0, 0))

    def v_spec(i):
        return pl.BlockSpec((None, page, H_A, DV_A), lambda b, g, pt: (pt[b, g * npg + i], 0, 0, 0))

    in_specs = ([per_b(2 * n_dec), per_b(n_dec), per_b(n_dec), per_b(n_dec),
                 vec(DK_A), vec(DK_A), vec(DK_A), vec(DK_A), vec(DV_A)]
                + [k_spec(i) for i in range(npg)] + [v_spec(i) for i in range(npg)])
    grid_spec = pltpu.PrefetchScalarGridSpec(
        num_scalar_prefetch=1, grid=(n_b, n_pages // npg), in_specs=in_specs,
        out_specs=per_b(n_dec),
        scratch_shapes=[pltpu.VMEM((H_A, 2 * n_dec, 1), F32), pltpu.VMEM((H_A, 2 * n_dec, 1), F32),
                        pltpu.VMEM((H_A, 2 * n_dec, DV_A), F32)])
    return pl.pallas_call(
        _attn_sample_kernel, out_shape=jax.ShapeDtypeStruct((n_b, n_dec, D_ATT), F32), grid_spec=grid_spec,
        compiler_params=pltpu.CompilerParams(dimension_semantics=("parallel", "arbitrary"),
                                             vmem_limit_bytes=VMEM_LIMIT),
        name="attn_sample",
    )(page_table, q8, k_new, v_new, ga, lq1, lk1, lq2, lk2, subg,
      *([kt_pages] * npg), *([v_pages] * npg))


def _rwkv_pre_kernel(seq_len, rw_ref, prev_ref, first_ref, mu_ref, w0_ref, wup_ref, a0_ref, aup_ref,
                     kk_ref, ka_ref, rk_ref, bd_ref,
                     r_out, w_out, k_out, v_out, nkk_out, b_out, bonus_out):
    tm = rw_ref.shape[0]
    rw = rw_ref[...]
    rolled = pltpu.roll(rw, 1, axis=0)
    rowi = lax.broadcasted_iota(jnp.int32, (tm, 1), 0)
    if seq_len >= tm:
        at_start = (pl.program_id(0) % (seq_len // tm)) == 0
        row0 = jnp.where(at_start, first_ref[...], prev_ref[7:8, :])
        prev = jnp.where(rowi == 0, row0, rolled)
    else:
        prev = jnp.where(rowi % seq_len == 0, first_ref[...], rolled)
    u = rw + (prev - rw) * mu_ref[...]
    r = u[:, 0:D_RWKV]
    k = u[:, D_RWKV:2 * D_RWKV]
    v = u[:, 2 * D_RWKV:3 * D_RWKV]
    wd = u[:, 3 * D_RWKV:3 * D_RWKV + R_LORA]
    ad = u[:, 3 * D_RWKV + R_LORA:]
    w_log = -jax.nn.softplus(-(w0_ref[...] + _dot(jnp.tanh(wd).astype(BF16), wup_ref[...]))) - 0.5
    decay = jnp.exp(-jnp.exp(w_log))
    a = jax.nn.sigmoid(a0_ref[...] + _dot(ad.astype(BF16), aup_ref[...]))
    bd = bd_ref[...]
    kk = k * kk_ref[...]
    kk = kk / jnp.maximum(jnp.sqrt(_segsum(kk * kk, bd)), L2_EPS)
    k2 = k * (1.0 + (a - 1.0) * ka_ref[...])
    r_out[...] = r
    w_out[...] = decay
    k_out[...] = k2
    v_out[...] = v
    nkk_out[...] = -kk
    b_out[...] = kk * a
    bonus_out[...] = _segsum(r * k2 * rk_ref[...], bd) * v


def _rwkv_pre(rw, first, seq_len, mu, w0, wup, a0, aup, k_k, k_a, r_k, bd):
    rows = rw.shape[0]
    tm = ROW_TILE
    const = lambda i: (0, 0)
    row_blk = lambda w: pl.BlockSpec((tm, w), lambda i: (i, 0))
    if seq_len >= tm:
        first_spec = pl.BlockSpec((None, 1, SHIFT_W), lambda i: (i * tm // seq_len, 0, 0))
    else:
        first_spec = row_blk(SHIFT_W)
    in_specs = [
        row_blk(SHIFT_W),
        pl.BlockSpec((8, SHIFT_W), lambda i: (jnp.maximum(i * (tm // 8) - 1, 0), 0)),
        first_spec,
        pl.BlockSpec((1, SHIFT_W), const),
        pl.BlockSpec((1, D_RWKV), const), pl.BlockSpec((R_LORA, D_RWKV), const),
        pl.BlockSpec((1, D_RWKV), const), pl.BlockSpec((R_LORA, D_RWKV), const),
        pl.BlockSpec((1, D_RWKV), const), pl.BlockSpec((1, D_RWKV), const), pl.BlockSpec((1, D_RWKV), const),
        pl.BlockSpec((D_RWKV, D_RWKV), const),
    ]
    out = jax.ShapeDtypeStruct((rows, D_RWKV), F32)
    return pl.pallas_call(
        functools.partial(_rwkv_pre_kernel, seq_len),
        out_shape=(out,) * 7, grid=(rows // tm,), in_specs=in_specs, out_specs=(row_blk(D_RWKV),) * 7,
        compiler_params=pltpu.CompilerParams(dimension_semantics=("parallel",), vmem_limit_bytes=VMEM_LIMIT),
        name="rwkv_pre",
    )(rw, rw, first, mu, w0, wup, a0, aup, k_k, k_a, r_k, bd)


def _rwkv_scan_kernel(r_ref, w_ref, k_ref, v_ref, nkk_ref, b_ref, s0_ref, gng_ref, gnb_ref,
                      y_ref, s_ref):
    nb, tc = r_ref.shape[0], r_ref.shape[1]
    n_pair = H_R // 2
    c = pl.program_id(1)

    @pl.when(c == 0)
    def _():
        s_ref[...] = s0_ref[...]

    lane = lax.broadcasted_iota(jnp.int32, (HS_R, 2 * HS_R), 1)
    row = lax.broadcasted_iota(jnp.int32, (HS_R, 2 * HS_R), 0)
    even = lane < HS_R
    eye_e = jnp.where(lane == row, 1.0, 0.0).astype(F32)
    eye_o = jnp.where(lane == row + HS_R, 1.0, 0.0).astype(F32)
    lane1 = lax.broadcasted_iota(jnp.int32, (1, 2 * HS_R), 1)
    even1 = lane1 < HS_R
    row8 = lax.broadcasted_iota(jnp.int32, (8, 2 * HS_R), 0)
    lane8 = lax.broadcasted_iota(jnp.int32, (8, 2 * HS_R), 1)

    unroll = min(8, tc)

    def steps(g, carry):
        t0 = pl.multiple_of(g * unroll, unroll)
        for b in range(nb):
            for p in range(n_pair):
                ls = slice(p * 2 * HS_R, (p + 1) * 2 * HS_R)
                rows = pl.ds(t0, unroll)
                r8, w8, k8, v8, nkk8, b8 = (ref[b, rows, ls] for ref in
                                            (r_ref, w_ref, k_ref, v_ref, nkk_ref, b_ref))
                sel = ((row8 == 2 * p) & (lane8 < HS_R)) | ((row8 == 2 * p + 1) & (lane8 >= HS_R))
                s = s_ref[b, p]
                for j in range(unroll):
                    nkk = nkk8[j:j + 1]
                    sa_e = jnp.sum(s * jnp.where(even1, nkk, 0.0), axis=1, keepdims=True)
                    sa_o = jnp.sum(s * jnp.where(even1, 0.0, nkk), axis=1, keepdims=True)
                    sa = jnp.where(even, sa_e, sa_o)
                    vr = v8[j:j + 1]
                    vc_e = jnp.sum(eye_e * vr, axis=1, keepdims=True)
                    vc_o = jnp.sum(eye_o * vr, axis=1, keepdims=True)
                    vc = jnp.where(even, vc_e, vc_o)
                    s = s * w8[j:j + 1] + sa * b8[j:j + 1] + vc * k8[j:j + 1]
                    r2 = jnp.where(sel, jnp.broadcast_to(r8[j:j + 1], (8, 2 * HS_R)), 0.0).astype(BF16)
                    yp = lax.dot_general(r2, s.astype(BF16), _NT, preferred_element_type=F32)
                    if p == 0:
                        y_ref[b, t0 + j] = yp
                    else:
                        y_ref[b, t0 + j] += yp
                s_ref[b, p] = s
        return carry

    lax.fori_loop(0, tc // unroll, steps, 0)

    y = y_ref[...]
    mean = jnp.mean(y, axis=-1, keepdims=True)
    var = jnp.mean(jnp.square(y - mean), axis=-1, keepdims=True)
    y_ref[...] = (y - mean) * lax.rsqrt(var + GN_EPS) * gng_ref[...] + gnb_ref[...]


def _rwkv_scan(r, w, k, v, nkk, b, s0, gn_g, gn_b, n_seq, seq_len):
    nb = SCAN_SEQS
    tc = min(SCAN_CHUNK, seq_len)
    n_pair = H_R // 2
    seq = lambda a: a.reshape(n_seq, seq_len, D_RWKV)
    in_blk = pl.BlockSpec((nb, tc, D_RWKV), lambda g, c: (g, c, 0))
    st_blk = pl.BlockSpec((nb, n_pair, HS_R, 2 * HS_R), lambda g, c: (g, 0, 0, 0))
    gn_blk = pl.BlockSpec((1, 1, H_R, HS_R), lambda g, c: (0, 0, 0, 0))
    return pl.pallas_call(
        _rwkv_scan_kernel,
        out_shape=(jax.ShapeDtypeStruct((n_seq, seq_len, H_R, HS_R), F32),
                   jax.ShapeDtypeStruct((n_seq, n_pair, HS_R, 2 * HS_R), F32)),
        grid=(n_seq // nb, seq_len // tc),
        in_specs=[in_blk] * 6 + [st_blk, gn_blk, gn_blk],
        out_specs=(pl.BlockSpec((nb, tc, H_R, HS_R), lambda g, c: (g, c, 0, 0)), st_blk),
        compiler_params=pltpu.CompilerParams(dimension_semantics=("parallel", "arbitrary"),
                                             vmem_limit_bytes=VMEM_LIMIT),
        name="rwkv_scan",
    )(seq(r), seq(w), seq(k), seq(v), seq(nkk), seq(b), s0,
      gn_g.reshape(1, 1, H_R, HS_R), gn_b.reshape(1, 1, H_R, HS_R))


def _out_proj_kernel(oa_ref, yg_ref, bonus_ref, gr_ref, x_ref, wo_ref, y_ref):
    mix_r = ((yg_ref[...] + bonus_ref[...]) * gr_ref[...]).astype(BF16)
    y_ref[...] = (x_ref[...] + _dot(oa_ref[...].astype(BF16), wo_ref[0:D_ATT, :])
                  + _dot(mix_r, wo_ref[D_ATT:, :]))


def _out_proj(oa, yg, bonus, gr, x, wo):
    rows = x.shape[0]
    tm = ROW_TILE
    row_blk = lambda w: pl.BlockSpec((tm, w), lambda i: (i, 0))
    return pl.pallas_call(
        _out_proj_kernel, out_shape=jax.ShapeDtypeStruct((rows, D_MODEL), F32), grid=(rows // tm,),
        in_specs=[row_blk(D_ATT), row_blk(D_RWKV), row_blk(D_RWKV), row_blk(D_RWKV), row_blk(D_MODEL),
                  pl.BlockSpec(wo.shape, lambda i: (0, 0))],
        out_specs=row_blk(D_MODEL),
        compiler_params=pltpu.CompilerParams(dimension_semantics=("parallel",), vmem_limit_bytes=VMEM_LIMIT),
        name="out_proj",
    )(oa, yg, bonus, gr, x, wo)


def _rope_tables(pos):
    half = DK_A // 2
    inv = 1.0 / (ROPE_THETA ** (jnp.arange(half, dtype=F32) / half))
    ang = inv[:, None] * pos.astype(F32)[None, :]
    return jnp.cos(ang), jnp.sin(ang)


def _pair_state(s):
    n = s.shape[0]
    return s.reshape(n, H_R // 2, 2, HS_R, HS_R).transpose(0, 1, 3, 2, 4).reshape(n, H_R // 2, HS_R, 2 * HS_R)


def _unpair_state(s):
    n = s.shape[0]
    return s.reshape(n, H_R // 2, HS_R, 2, HS_R).transpose(0, 1, 3, 2, 4).reshape(n, H_R, HS_R, HS_R)


def kernel(x_prompt, x_sample, cache_k, cache_v, state_wkv, state_shift, page_table, ln_g, w_in, q_norm_g,
           k_norm_g, lambda_q1, lambda_k1, lambda_q2, lambda_k2, subln_g, shift_mu, w0, w_lora_up, a0,
           a_lora_up, k_k, k_a, r_k, gn_g, gn_b, w_out):
    n_b, seq = x_prompt.shape[:2]
    n_d, n_dec = x_sample.shape[:2]
    page = cache_k.shape[2]
    past = page_table.shape[1] * page
    layer = 0

    w_in_b = w_in[layer].astype(BF16)
    wq_t = w_in_b[:, 0:D_ATT].T
    wk_t = w_in_b[:, D_ATT:2 * D_ATT].T
    w_rest = w_in_b[:, 2 * D_ATT:]
    wo = w_out[layer].astype(BF16)
    wup = w_lora_up[layer].astype(BF16)
    aup = a_lora_up[layer].astype(BF16)
    qg = q_norm_g[layer].reshape(DK_A, 1)
    kg = k_norm_g[layer].reshape(DK_A, 1)
    seg = jnp.arange(D_RWKV) // HS_R
    bd = (seg[:, None] == seg[None, :]).astype(BF16)
    lam_vecs = (lambda_q1[layer][None], lambda_k1[layer][None], lambda_q2[layer][None], lambda_k2[layer][None])
    subg = subln_g[layer][None]
    row = lambda a: a[layer][None]

    def branch(x, pos, n_seq, seq_len, proj_seqs, first, s0, attend):
        rows = n_seq * seq_len
        cos_t, sin_t = _rope_tables(pos)
        q_b, kt, ktb, v, v_b, ga, rw, gr = _proj(x.reshape(rows, D_MODEL), ln_g[layer][None], wq_t, wk_t,
                                                 w_rest, cos_t, sin_t, qg, kg, proj_seqs, rows // proj_seqs)
        out_a = attend(q_b, kt, ktb, v, v_b, ga)
        r, w, k2, vv, nkk, bb, bonus = _rwkv_pre(rw, first, seq_len, row(shift_mu), row(w0), wup, row(a0),
                                                 aup, row(k_k), row(k_a), row(r_k), bd)
        yg, s_fin = _rwkv_scan(r, w, k2, vv, nkk, bb, _pair_state(s0), gn_g[layer], gn_b[layer],
                               n_seq, seq_len)
        y = _out_proj(out_a, yg.reshape(rows, D_RWKV), bonus, gr, x.reshape(rows, D_MODEL), wo)
        return y, kt, v, _unpair_state(s_fin), rw

    def attend_prompt(q_b, kt, ktb, v, v_b, ga):
        return _attn_prompt(q_b, ktb, v_b, ga, *lam_vecs, subg, n_b, seq)

    pos_p = jnp.tile(jnp.arange(seq), n_b)
    yp, kt_p, v_p, s_p, rw_p = branch(x_prompt, pos_p, n_b, seq, n_b,
                                      jnp.zeros((n_b, 1, SHIFT_W), F32),
                                      jnp.zeros((n_b, H_R, HS_R, HS_R), F32), attend_prompt)

    rows_s = n_d * n_dec
    kt_pages = cache_k[layer].transpose(0, 2, 3, 4, 1)
    v_pages = cache_v[layer]

    def attend_sample(q_b, kt, ktb, v, v_b, ga):
        q3 = q_b.reshape(n_d, n_dec, D_ATT)
        q8 = jnp.concatenate([q3, q3], axis=1)
        k_new = kt.reshape(D_ATT, rows_s).T.reshape(n_d, n_dec, D_ATT)
        out = _attn_sample(page_table, q8, k_new, v.reshape(n_d, n_dec, D_ATT),
                           ga.reshape(n_d, n_dec, D_ATT), *lam_vecs, subg, kt_pages, v_pages)
        return out.reshape(rows_s, D_ATT)

    pos_s = jnp.tile(past + jnp.arange(n_dec), n_d)
    first_s = jnp.repeat(state_shift[layer], n_dec, axis=0)

    ys, kt_s, v_s, s_s, rw_s = branch(x_sample, pos_s, n_d, n_dec, 1, first_s, state_wkv[layer],
                                      attend_sample)

    k_prompt = kt_p.reshape(n_b, H_A, 2, DK_A, seq).transpose(0, 4, 1, 2, 3)
    k_sample = kt_s.reshape(H_A, 2, DK_A, n_d, n_dec).transpose(3, 4, 0, 1, 2)
    return (yp.reshape(n_b, seq, D_MODEL), ys.reshape(n_d, n_dec, D_MODEL),
            k_prompt[None], v_p.reshape(n_b, seq, H_A, DV_A)[None], s_p[None],
            rw_p.reshape(n_b, seq, SHIFT_W)[:, -1][None],
            k_sample[None], v_s.reshape(n_d, n_dec, H_A, DV_A)[None], s_s[None],
            rw_s.reshape(n_d, n_dec, SHIFT_W)[:, -1][None])
```

```python
import functools
import math

import jax
import jax.numpy as jnp
from jax import lax
from jax.experimental import pallas as pl
from jax.experimental.pallas import tpu as pltpu

F32 = jnp.float32
BF16 = jnp.bfloat16

D_MODEL = 1024
D_ATT = 512
D_RWKV = 512
DK_A = 64
H_A = 4
DV_A = 128
HS_R = 64
H_R = 8
R_LORA = 64
SHIFT_W = 3 * D_RWKV + 2 * R_LORA
ROPE_THETA = 10000.0
RMS_EPS = 1e-6
GN_EPS = 64e-5
L2_EPS = 1e-12
LAM_INIT = 0.8 - 0.6 * math.exp(-0.3 * 0)

ROW_TILE = 256
ATT_TILE = 512
LOG2E = math.log2(math.e)
SCAN_SEQS = 4
SCAN_HEADS = 2
SCAN_CHUNK = 128
PAGES_PER_STEP = 16
VMEM_LIMIT = 48 * 1024 * 1024

_NT = (((1,), (1,)), ((), ()))


def _dot(a, b):
    return jnp.dot(a, b, preferred_element_type=F32)


def _segsum(x, bd):
    hi = x.astype(BF16)
    lo = (x - hi.astype(F32)).astype(BF16)
    return _dot(hi, bd) + _dot(lo, bd)


def _diff_lambda(lq1, lk1, lq2, lk2):
    return (jnp.exp(jnp.sum(lq1 * lk1, axis=1, keepdims=True))
            - jnp.exp(jnp.sum(lq2 * lk2, axis=1, keepdims=True)) + LAM_INIT)


def _subln_gate(o0, o1, lam, subg, gate):
    y = o0 - lam * o1
    y = y * lax.rsqrt(jnp.mean(y * y, axis=-1, keepdims=True) + RMS_EPS) * subg
    return (y * (1.0 - LAM_INIT)) * gate


def _proj_kernel(x_ref, lng_ref, wq_ref, wk_ref, wr_ref, cos_ref, sin_ref, qg_ref, kg_ref,
                 q_ref, kt_ref, ktb_ref, v_ref, vb_ref, ga_ref, rw_ref, gr_ref):
    tm = x_ref.shape[0]
    x = x_ref[...]
    h = x * lax.rsqrt(jnp.mean(x * x, axis=-1, keepdims=True) + RMS_EPS) * lng_ref[...]
    h = h.astype(BF16)
    cos = cos_ref[...][None]
    sin = sin_ref[...][None]

    def norm_rope(t, g):
        t3 = t.reshape(2 * H_A, DK_A, tm)
        t3 = t3 * lax.rsqrt(jnp.mean(t3 * t3, axis=1, keepdims=True) + RMS_EPS) * g[None]
        x1 = t3[:, :DK_A // 2]
        x2 = t3[:, DK_A // 2:]
        o = jnp.concatenate([x1 * cos - x2 * sin, x2 * cos + x1 * sin], axis=1)
        return o.reshape(D_ATT, tm)

    qt = lax.dot_general(wq_ref[...], h, _NT, preferred_element_type=F32)
    qt = norm_rope(qt, qg_ref[...]) * (DK_A ** -0.5 * LOG2E)
    q_ref[...] = qt.T.astype(BF16)
    kt = lax.dot_general(wk_ref[...], h, _NT, preferred_element_type=F32)
    kt = norm_rope(kt, kg_ref[...])
    kt_ref[...] = kt
    ktb_ref[...] = kt.astype(BF16)

    v = _dot(h, wr_ref[:, 0:D_ATT])
    v_ref[...] = v
    vb_ref[...] = v.astype(BF16)
    za = _dot(h, wr_ref[:, D_ATT:2 * D_ATT])
    ga_ref[...] = za * jax.nn.sigmoid(za)
    rw_ref[...] = _dot(h, wr_ref[:, 2 * D_ATT:2 * D_ATT + SHIFT_W])
    zr = _dot(h, wr_ref[:, 2 * D_ATT + SHIFT_W:])
    gr_ref[...] = zr * jax.nn.sigmoid(zr)


def _proj(x, ln_g, wq_t, wk_t, w_rest, cos_t, sin_t, qg, kg, n_seq, seq_len):
    rows = x.shape[0]
    tm = ROW_TILE
    nsi = seq_len // tm
    tk = min(ATT_TILE, seq_len)
    per_tk = tk // tm
    const = lambda i: (0, 0)
    row_blk = lambda w: pl.BlockSpec((tm, w), lambda i: (i, 0))
    out_shape = (
        jax.ShapeDtypeStruct((rows, D_ATT), BF16),
        jax.ShapeDtypeStruct((n_seq, D_ATT, seq_len), F32),
        jax.ShapeDtypeStruct((n_seq, seq_len // tk, D_ATT, tk), BF16),
        jax.ShapeDtypeStruct((rows, D_ATT), F32),
        jax.ShapeDtypeStruct((rows, D_ATT), BF16),
        jax.ShapeDtypeStruct((rows, D_ATT), F32),
        jax.ShapeDtypeStruct((rows, SHIFT_W), F32),
        jax.ShapeDtypeStruct((rows, D_RWKV), F32),
    )
    out_specs = (
        row_blk(D_ATT),
        pl.BlockSpec((None, D_ATT, tm), lambda i: (i // nsi, 0, i % nsi)),
        pl.BlockSpec((None, None, D_ATT, tm), lambda i: (i // nsi, (i % nsi) // per_tk, 0, i % per_tk)),
        row_blk(D_ATT), row_blk(D_ATT), row_blk(D_ATT), row_blk(SHIFT_W), row_blk(D_RWKV),
    )
    in_specs = [
        row_blk(D_MODEL),
        pl.BlockSpec((1, D_MODEL), const),
        pl.BlockSpec(wq_t.shape, const),
        pl.BlockSpec(wk_t.shape, const),
        pl.BlockSpec(w_rest.shape, const),
        pl.BlockSpec((DK_A // 2, tm), lambda i: (0, i)),
        pl.BlockSpec((DK_A // 2, tm), lambda i: (0, i)),
        pl.BlockSpec((DK_A, 1), const),
        pl.BlockSpec((DK_A, 1), const),
    ]
    return pl.pallas_call(
        _proj_kernel, out_shape=out_shape, grid=(rows // tm,), in_specs=in_specs, out_specs=out_specs,
        compiler_params=pltpu.CompilerParams(dimension_semantics=("parallel",), vmem_limit_bytes=VMEM_LIMIT),
        name="proj",
    )(x, ln_g, wq_t, wk_t, w_rest, cos_t, sin_t, qg, kg)


def _attn_prompt_kernel(q_ref, kt_ref, v_ref, ga_ref, lq1_ref, lk1_ref, lq2_ref, lk2_ref, subg_ref,
                        o_ref, m_sc, acc_sc):
    tq = q_ref.shape[0]
    tk = kt_ref.shape[2]
    i = pl.program_id(2)
    q = q_ref[...]
    lane = lax.broadcasted_iota(jnp.int32, q.shape, 1)
    zero = jnp.zeros_like(q)
    qs = jnp.concatenate([jnp.where(lane < DK_A, q, zero), jnp.where(lane >= DK_A, q, zero)], axis=0)
    m_sc[...] = jnp.full(m_sc.shape, -jnp.inf, F32)
    acc_sc[...] = jnp.zeros(acc_sc.shape, F32)
    ones = jnp.ones((tk, DV_A), BF16)

    def step(j, masked):
        kj = kt_ref[j]
        vj = v_ref[pl.ds(pl.multiple_of(j * tk, tk), tk), :]
        s = _dot(qs, kj)
        if masked:
            row = lax.broadcasted_iota(jnp.int32, s.shape, 0)
            col = lax.broadcasted_iota(jnp.int32, s.shape, 1)
            qpos = jnp.where(row >= tq, row - tq, row)
            s = jnp.where(col <= qpos, s, -jnp.inf)
        m_prev = m_sc[...]
        m_new = jnp.maximum(m_prev, jnp.max(s, axis=1, keepdims=True))
        alpha = jnp.exp2(m_prev - m_new)
        p = jnp.exp2(s - jnp.tile(m_new, (1, tk // DV_A)))
        acc_sc[...] = (jnp.tile(alpha, (1, 2)) * acc_sc[...]
                       + _dot(p.astype(BF16), jnp.concatenate([vj, ones], axis=1)))
        m_sc[...] = m_new

    def body(j, c):
        step(j, False)
        return c

    lax.fori_loop(0, i, body, 0)
    step(i, True)

    acc = acc_sc[...]
    o = acc[:, :DV_A] / acc[:, DV_A:]
    lam = _diff_lambda(lq1_ref[...], lk1_ref[...], lq2_ref[...], lk2_ref[...])
    o_ref[...] = _subln_gate(o[:tq], o[tq:], lam, subg_ref[...], ga_ref[...]).astype(BF16)


def _attn_prompt(q_b, ktb, v_b, ga, lq1, lk1, lq2, lk2, subg, n_seq, seq_len):
    tq = ATT_TILE
    nq = seq_len // tq
    rows = n_seq * seq_len
    vec = lambda w: pl.BlockSpec((1, w), lambda b, h, i: (0, 0))
    in_specs = [
        pl.BlockSpec((tq, DV_A), lambda b, h, i: (b * nq + i, h)),
        pl.BlockSpec((None, nq, DV_A, tq), lambda b, h, i: (b, 0, h, 0)),
        pl.BlockSpec((seq_len, DV_A), lambda b, h, i: (b, h)),
        pl.BlockSpec((tq, DV_A), lambda b, h, i: (b * nq + i, h)),
        vec(DK_A), vec(DK_A), vec(DK_A), vec(DK_A), vec(DV_A),
    ]
    return pl.pallas_call(
        _attn_prompt_kernel,
        out_shape=jax.ShapeDtypeStruct((rows, D_ATT), BF16),
        grid=(n_seq, H_A, nq), in_specs=in_specs,
        out_specs=pl.BlockSpec((tq, DV_A), lambda b, h, i: (b * nq + i, h)),
        scratch_shapes=[pltpu.VMEM((2 * tq, DV_A), F32), pltpu.VMEM((2 * tq, 2 * DV_A), F32)],
        compiler_params=pltpu.CompilerParams(dimension_semantics=("parallel", "parallel", "arbitrary"),
                                             vmem_limit_bytes=VMEM_LIMIT),
        name="attn_prompt",
    )(q_b, ktb, v_b, ga, lq1, lk1, lq2, lk2, subg)


def _attn_sample_kernel(pt_ref, q8_ref, kn_ref, vn_ref, ga_ref, lq1_ref, lk1_ref, lq2_ref, lk2_ref,
                        subg_ref, *refs):
    npg = PAGES_PER_STEP
    k_refs = refs[:npg]
    v_refs = refs[npg:2 * npg]
    o_ref, m_sc, l_sc, acc_sc = refs[2 * npg:]
    g = pl.program_id(1)
    n_dec = q8_ref.shape[0] // 2

    @pl.when(g == 0)
    def _():
        m_sc[...] = jnp.full(m_sc.shape, -jnp.inf, F32)
        l_sc[...] = jnp.zeros(l_sc.shape, F32)
        acc_sc[...] = jnp.zeros(acc_sc.shape, F32)

    rowi = lax.broadcasted_iota(jnp.int32, (2 * n_dec, DV_A), 0)
    lane = lax.broadcasted_iota(jnp.int32, (2 * n_dec, DV_A), 1)
    sel = ((rowi < n_dec) & (lane < DK_A)) | ((rowi >= n_dec) & (lane >= DK_A))
    q8 = q8_ref[...]

    def head_q(h):
        qh = q8[:, h * DV_A:(h + 1) * DV_A]
        return jnp.where(sel, qh, jnp.zeros_like(qh))

    page = k_refs[0].shape[-1]
    s = jnp.concatenate(
        [jnp.concatenate([_dot(head_q(h), k_refs[i][h].astype(BF16).reshape(2 * DK_A, -1))
                          for i in range(npg)], axis=1) for h in range(H_A)], axis=0)
    m_prev = m_sc[...]
    m_new = jnp.maximum(m_prev, jnp.max(s, axis=1, keepdims=True))
    alpha = jnp.exp2(m_prev - m_new)
    p = jnp.exp2(s - m_new)
    l_sc[...] = alpha * l_sc[...] + jnp.sum(p, axis=1, keepdims=True)
    m_sc[...] = m_new
    pb = p.astype(BF16)
    rows_h = 2 * n_dec
    for h in range(H_A):
        acc = alpha[h * rows_h:(h + 1) * rows_h] * acc_sc[h]
        for i in range(npg):
            vh = v_refs[i][pl.ds(h, page, stride=H_A), :]
            acc = acc + _dot(pb[h * rows_h:(h + 1) * rows_h, i * page:(i + 1) * page], vh.astype(BF16))
        acc_sc[h] = acc

    @pl.when(g == pl.num_programs(1) - 1)
    def _():
        lam = _diff_lambda(lq1_ref[...], lk1_ref[...], lq2_ref[...], lk2_ref[...])
        t_idx = jnp.where(rowi[:, :1] >= n_dec, rowi[:, :1] - n_dec, rowi[:, :1])
        for h in range(H_A):
            hs = slice(h * DV_A, (h + 1) * DV_A)
            qf = head_q(h).astype(F32)
            s_new = []
            for j in range(n_dec):
                sj = jnp.sum(qf * kn_ref[j:j + 1, hs], axis=1, keepdims=True)
                s_new.append(jnp.where(j <= t_idx, sj, -jnp.inf))
            m_prev = m_sc[h * rows_h:(h + 1) * rows_h]
            m_new = m_prev
            for sj in s_new:
                m_new = jnp.maximum(m_new, sj)
            alpha = jnp.exp2(m_prev - m_new)
            l = alpha * l_sc[h * rows_h:(h + 1) * rows_h]
            acc = alpha * acc_sc[h]
            for j, sj in enumerate(s_new):
                pj = jnp.exp2(sj - m_new)
                l = l + pj
                acc = acc + pj * vn_ref[j:j + 1, hs]
            o = acc / l
            o_ref[:, hs] = _subln_gate(o[:n_dec], o[n_dec:], lam, subg_ref[...], ga_ref[:, hs])


def _attn_sample(page_table, q8, k_new, v_new, ga, lq1, lk1, lq2, lk2, subg, kt_pages, v_pages):
    n_b, n_pages = page_table.shape
    n_dec = k_new.shape[1]
    npg = PAGES_PER_STEP
    page = kt_pages.shape[-1]
    vec = lambda w: pl.BlockSpec((1, w), lambda b, g, pt: (0, 0))
    per_b = lambda r: pl.BlockSpec((None, r, D_ATT), lambda b, g, pt: (b, 0, 0))

    def k_spec(i):
        return pl.BlockSpec((None, H_A, 2, DK_A, page), lambda b, g, pt: (pt[b, g * npg + i], 0, 0, 0, 0))

    def v_spec(i):
        return pl.BlockSpec((None, page * H_A, DV_A), lambda b, g, pt: (pt[b, g * npg + i], 0, 0))

    in_specs = ([per_b(2 * n_dec), per_b(n_dec), per_b(n_dec), per_b(n_dec),
                 vec(DK_A), vec(DK_A), vec(DK_A), vec(DK_A), vec(DV_A)]
                + [k_spec(i) for i in range(npg)] + [v_spec(i) for i in range(npg)])
    grid_spec = pltpu.PrefetchScalarGridSpec(
        num_scalar_prefetch=1, grid=(n_b, n_pages // npg), in_specs=in_specs,
        out_specs=per_b(n_dec),
        scratch_shapes=[pltpu.VMEM((H_A * 2 * n_dec, 1), F32), pltpu.VMEM((H_A * 2 * n_dec, 1), F32),
                        pltpu.VMEM((H_A, 2 * n_dec, DV_A), F32)])
    return pl.pallas_call(
        _attn_sample_kernel, out_shape=jax.ShapeDtypeStruct((n_b, n_dec, D_ATT), F32), grid_spec=grid_spec,
        compiler_params=pltpu.CompilerParams(dimension_semantics=("parallel", "arbitrary"),
                                             vmem_limit_bytes=VMEM_LIMIT),
        name="attn_sample",
    )(page_table, q8, k_new, v_new, ga, lq1, lk1, lq2, lk2, subg,
      *([kt_pages] * npg), *([v_pages] * npg))


def _rwkv_pre_kernel(seq_len, rw_ref, prev_ref, first_ref, mu_ref, w0_ref, wup_ref, a0_ref, aup_ref,
                     kk_ref, ka_ref, rk_ref, bd_ref,
                     r_out, w_out, k_out, v_out, nkk_out, b_out, bonus_out):
    tm = rw_ref.shape[0]
    rw = rw_ref[...]
    rolled = pltpu.roll(rw, 1, axis=0)
    rowi = lax.broadcasted_iota(jnp.int32, (tm, 1), 0)
    if seq_len >= tm:
        at_start = (pl.program_id(0) % (seq_len // tm)) == 0
        row0 = jnp.where(at_start, first_ref[...], prev_ref[7:8, :])
        prev = jnp.where(rowi == 0, row0, rolled)
    else:
        prev = jnp.where(rowi % seq_len == 0, first_ref[...], rolled)
    u = rw + (prev - rw) * mu_ref[...]
    r = u[:, 0:D_RWKV]
    k = u[:, D_RWKV:2 * D_RWKV]
    v = u[:, 2 * D_RWKV:3 * D_RWKV]
    wd = u[:, 3 * D_RWKV:3 * D_RWKV + R_LORA]
    ad = u[:, 3 * D_RWKV + R_LORA:]
    w_log = -jax.nn.softplus(-(w0_ref[...] + _dot(jnp.tanh(wd).astype(BF16), wup_ref[...]))) - 0.5
    decay = jnp.exp(-jnp.exp(w_log))
    a = jax.nn.sigmoid(a0_ref[...] + _dot(ad.astype(BF16), aup_ref[...]))
    bd = bd_ref[...]
    kk = k * kk_ref[...]
    kk = kk / jnp.maximum(jnp.sqrt(_segsum(kk * kk, bd)), L2_EPS)
    k2 = k * (1.0 + (a - 1.0) * ka_ref[...])
    r_out[...] = r
    w_out[...] = decay
    k_out[...] = k2
    v_out[...] = v
    nkk_out[...] = -kk
    b_out[...] = kk * a
    bonus_out[...] = _segsum(r * k2 * rk_ref[...], bd) * v


def _rwkv_pre(rw, first, seq_len, mu, w0, wup, a0, aup, k_k, k_a, r_k, bd):
    rows = rw.shape[0]
    tm = ROW_TILE
    const = lambda i: (0, 0)
    row_blk = lambda w: pl.BlockSpec((tm, w), lambda i: (i, 0))
    if seq_len >= tm:
        first_spec = pl.BlockSpec((None, 1, SHIFT_W), lambda i: (i * tm // seq_len, 0, 0))
    else:
        first_spec = row_blk(SHIFT_W)
    in_specs = [
        row_blk(SHIFT_W),
        pl.BlockSpec((8, SHIFT_W), lambda i: (jnp.maximum(i * (tm // 8) - 1, 0), 0)),
        first_spec,
        pl.BlockSpec((1, SHIFT_W), const),
        pl.BlockSpec((1, D_RWKV), const), pl.BlockSpec((R_LORA, D_RWKV), const),
        pl.BlockSpec((1, D_RWKV), const), pl.BlockSpec((R_LORA, D_RWKV), const),
        pl.BlockSpec((1, D_RWKV), const), pl.BlockSpec((1, D_RWKV), const), pl.BlockSpec((1, D_RWKV), const),
        pl.BlockSpec((D_RWKV, D_RWKV), const),
    ]
    out = jax.ShapeDtypeStruct((rows, D_RWKV), F32)
    return pl.pallas_call(
        functools.partial(_rwkv_pre_kernel, seq_len),
        out_shape=(out,) * 7, grid=(rows // tm,), in_specs=in_specs, out_specs=(row_blk(D_RWKV),) * 7,
        compiler_params=pltpu.CompilerParams(dimension_semantics=("parallel",), vmem_limit_bytes=VMEM_LIMIT),
        name="rwkv_pre",
    )(rw, rw, first, mu, w0, wup, a0, aup, k_k, k_a, r_k, bd)


def _rwkv_scan_kernel(r_ref, w_ref, k_ref, v_ref, nkk_ref, b_ref, s0_ref, gng_ref, gnb_ref,
                      y_ref, s_ref):
    nb, tc = r_ref.shape[0], r_ref.shape[1]
    n_grp = H_R // SCAN_HEADS
    gw = SCAN_HEADS * HS_R
    c = pl.program_id(1)

    @pl.when(c == 0)
    def _():
        s_ref[...] = s0_ref[...]

    lane = lax.broadcasted_iota(jnp.int32, (HS_R, gw), 1)
    row = lax.broadcasted_iota(jnp.int32, (HS_R, gw), 0)
    head_of_lane = lane // HS_R
    diag_of_head = [lane == row + e * HS_R for e in range(SCAN_HEADS)]
    head_of_lane1 = lax.broadcasted_iota(jnp.int32, (1, gw), 1) // HS_R
    row8 = lax.broadcasted_iota(jnp.int32, (H_R, gw), 0)
    lane8 = lax.broadcasted_iota(jnp.int32, (H_R, gw), 1)

    unroll = min(8, tc)

    def steps(g, carry):
        t0 = pl.multiple_of(g * unroll, unroll)
        rows = pl.ds(t0, unroll)
        for j in range(unroll):
            for b in range(nb):
                y8 = None
                for p in range(n_grp):
                    ls = slice(p * gw, (p + 1) * gw)
                    row = lambda ref: ref[b, rows, ls][j:j + 1]
                    s = s_ref[b, p]
                    nkk = row(nkk_ref)
                    sa = None
                    for e in range(SCAN_HEADS):
                        in_head = head_of_lane1 == e
                        part = jnp.sum(s * jnp.where(in_head, nkk, 0.0), axis=1, keepdims=True)
                        sa = part if sa is None else jnp.where(head_of_lane == e, part, sa)
                    vr = row(v_ref)
                    vc = None
                    for e in range(SCAN_HEADS):
                        part = jnp.sum(jnp.where(diag_of_head[e], vr, 0.0), axis=1, keepdims=True)
                        vc = part if vc is None else jnp.where(head_of_lane == e, part, vc)
                    s = s * row(w_ref) + sa * row(b_ref) + vc * row(k_ref)
                    s_ref[b, p] = s
                    sel = row8 == p * SCAN_HEADS + lane8 // HS_R
                    r2 = jnp.where(sel, jnp.broadcast_to(row(r_ref), (H_R, gw)), 0.0).astype(BF16)
                    yp = lax.dot_general(r2, s.astype(BF16), _NT, preferred_element_type=F32)
                    y8 = yp if y8 is None else y8 + yp
                y_ref[b, t0 + j] = y8
        return carry

    lax.fori_loop(0, tc // unroll, steps, 0)

    y = y_ref[...]
    mean = jnp.mean(y, axis=-1, keepdims=True)
    var = jnp.mean(jnp.square(y - mean), axis=-1, keepdims=True)
    y_ref[...] = (y - mean) * lax.rsqrt(var + GN_EPS) * gng_ref[...] + gnb_ref[...]


def _rwkv_scan(r, w, k, v, nkk, b, s0, gn_g, gn_b, n_seq, seq_len):
    nb = SCAN_SEQS
    tc = min(SCAN_CHUNK, seq_len)
    n_grp = H_R // SCAN_HEADS
    gw = SCAN_HEADS * HS_R
    seq = lambda a: a.reshape(n_seq, seq_len, D_RWKV)
    in_blk = pl.BlockSpec((nb, tc, D_RWKV), lambda g, c: (g, c, 0))
    st_blk = pl.BlockSpec((nb, n_grp, HS_R, gw), lambda g, c: (g, 0, 0, 0))
    gn_blk = pl.BlockSpec((1, 1, H_R, HS_R), lambda g, c: (0, 0, 0, 0))
    return pl.pallas_call(
        _rwkv_scan_kernel,
        out_shape=(jax.ShapeDtypeStruct((n_seq, seq_len, H_R, HS_R), F32),
                   jax.ShapeDtypeStruct((n_seq, n_grp, HS_R, gw), F32)),
        grid=(n_seq // nb, seq_len // tc),
        in_specs=[in_blk] * 6 + [st_blk, gn_blk, gn_blk],
        out_specs=(pl.BlockSpec((nb, tc, H_R, HS_R), lambda g, c: (g, c, 0, 0)), st_blk),
        compiler_params=pltpu.CompilerParams(dimension_semantics=("parallel", "arbitrary"),
                                             vmem_limit_bytes=VMEM_LIMIT),
        name="rwkv_scan",
    )(seq(r), seq(w), seq(k), seq(v), seq(nkk), seq(b), s0,
      gn_g.reshape(1, 1, H_R, HS_R), gn_b.reshape(1, 1, H_R, HS_R))


def _out_proj_kernel(oa_ref, yg_ref, bonus_ref, gr_ref, x_ref, wo_ref, y_ref):
    mix_r = ((yg_ref[...] + bonus_ref[...]) * gr_ref[...]).astype(BF16)
    y_ref[...] = (x_ref[...] + _dot(oa_ref[...].astype(BF16), wo_ref[0:D_ATT, :])
                  + _dot(mix_r, wo_ref[D_ATT:, :]))


def _out_proj(oa, yg, bonus, gr, x, wo):
    rows = x.shape[0]
    tm = ROW_TILE
    row_blk = lambda w: pl.BlockSpec((tm, w), lambda i: (i, 0))
    return pl.pallas_call(
        _out_proj_kernel, out_shape=jax.ShapeDtypeStruct((rows, D_MODEL), F32), grid=(rows // tm,),
        in_specs=[row_blk(D_ATT), row_blk(D_RWKV), row_blk(D_RWKV), row_blk(D_RWKV), row_blk(D_MODEL),
                  pl.BlockSpec(wo.shape, lambda i: (0, 0))],
        out_specs=row_blk(D_MODEL),
        compiler_params=pltpu.CompilerParams(dimension_semantics=("parallel",), vmem_limit_bytes=VMEM_LIMIT),
        name="out_proj",
    )(oa, yg, bonus, gr, x, wo)


def _rope_tables(pos):
    half = DK_A // 2
    inv = 1.0 / (ROPE_THETA ** (jnp.arange(half, dtype=F32) / half))
    ang = inv[:, None] * pos.astype(F32)[None, :]
    return jnp.cos(ang), jnp.sin(ang)


def _pair_state(s):
    n = s.shape[0]
    g = H_R // SCAN_HEADS
    return (s.reshape(n, g, SCAN_HEADS, HS_R, HS_R).transpose(0, 1, 3, 2, 4)
            .reshape(n, g, HS_R, SCAN_HEADS * HS_R))


def _unpair_state(s):
    n = s.shape[0]
    g = H_R // SCAN_HEADS
    return s.reshape(n, g, HS_R, SCAN_HEADS, HS_R).transpose(0, 1, 3, 2, 4).reshape(n, H_R, HS_R, HS_R)


def kernel(x_prompt, x_sample, cache_k, cache_v, state_wkv, state_shift, page_table, ln_g, w_in, q_norm_g,
           k_norm_g, lambda_q1, lambda_k1, lambda_q2, lambda_k2, subln_g, shift_mu, w0, w_lora_up, a0,
           a_lora_up, k_k, k_a, r_k, gn_g, gn_b, w_out):
    n_b, seq = x_prompt.shape[:2]
    n_d, n_dec = x_sample.shape[:2]
    page = cache_k.shape[2]
    past = page_table.shape[1] * page
    layer = 0

    w_in_b = w_in[layer].astype(BF16)
    wq_t = w_in_b[:, 0:D_ATT].T
    wk_t = w_in_b[:, D_ATT:2 * D_ATT].T
    w_rest = w_in_b[:, 2 * D_ATT:]
    wo = w_out[layer].astype(BF16)
    wup = w_lora_up[layer].astype(BF16)
    aup = a_lora_up[layer].astype(BF16)
    qg = q_norm_g[layer].reshape(DK_A, 1)
    kg = k_norm_g[layer].reshape(DK_A, 1)
    seg = jnp.arange(D_RWKV) // HS_R
    bd = (seg[:, None] == seg[None, :]).astype(BF16)
    lam_vecs = (lambda_q1[layer][None], lambda_k1[layer][None], lambda_q2[layer][None], lambda_k2[layer][None])
    subg = subln_g[layer][None]
    row = lambda a: a[layer][None]

    def branch(x, pos, n_seq, seq_len, proj_seqs, first, s0, attend):
        rows = n_seq * seq_len
        cos_t, sin_t = _rope_tables(pos)
        q_b, kt, ktb, v, v_b, ga, rw, gr = _proj(x.reshape(rows, D_MODEL), ln_g[layer][None], wq_t, wk_t,
                                                 w_rest, cos_t, sin_t, qg, kg, proj_seqs, rows // proj_seqs)
        out_a = attend(q_b, kt, ktb, v, v_b, ga)
        r, w, k2, vv, nkk, bb, bonus = _rwkv_pre(rw, first, seq_len, row(shift_mu), row(w0), wup, row(a0),
                                                 aup, row(k_k), row(k_a), row(r_k), bd)
        yg, s_fin = _rwkv_scan(r, w, k2, vv, nkk, bb, _pair_state(s0), gn_g[layer], gn_b[layer],
                               n_seq, seq_len)
        y = _out_proj(out_a, yg.reshape(rows, D_RWKV), bonus, gr, x.reshape(rows, D_MODEL), wo)
        return y, kt, v, _unpair_state(s_fin), rw

    def attend_prompt(q_b, kt, ktb, v, v_b, ga):
        return _attn_prompt(q_b, ktb, v_b, ga, *lam_vecs, subg, n_b, seq)

    pos_p = jnp.tile(jnp.arange(seq), n_b)
    yp, kt_p, v_p, s_p, rw_p = branch(x_prompt, pos_p, n_b, seq, n_b,
                                      jnp.zeros((n_b, 1, SHIFT_W), F32),
                                      jnp.zeros((n_b, H_R, HS_R, HS_R), F32), attend_prompt)

    rows_s = n_d * n_dec
    kt_pages = cache_k[layer].transpose(0, 2, 3, 4, 1)
    v_pages = cache_v[layer].reshape(-1, page * H_A, DV_A)

    def attend_sample(q_b, kt, ktb, v, v_b, ga):
        q3 = q_b.reshape(n_d, n_dec, D_ATT)
        q8 = jnp.concatenate([q3, q3], axis=1)
        k_new = kt.reshape(D_ATT, rows_s).T.reshape(n_d, n_dec, D_ATT)
        out = _attn_sample(page_table, q8, k_new, v.reshape(n_d, n_dec, D_ATT),
                           ga.reshape(n_d, n_dec, D_ATT), *lam_vecs, subg, kt_pages, v_pages)
        return out.reshape(rows_s, D_ATT)

    pos_s = jnp.tile(past + jnp.arange(n_dec), n_d)
    first_s = jnp.repeat(state_shift[layer], n_dec, axis=0)

    ys, kt_s, v_s, s_s, rw_s = branch(x_sample, pos_s, n_d, n_dec, 1, first_s, state_wkv[layer],
                                      attend_sample)

    k_prompt = kt_p.reshape(n_b, H_A, 2, DK_A, seq).transpose(0, 4, 1, 2, 3)
    k_sample = kt_s.reshape(H_A, 2, DK_A, n_d, n_dec).transpose(3, 4, 0, 1, 2)
    return (yp.reshape(n_b, seq, D_MODEL), ys.reshape(n_d, n_dec, D_MODEL),
            k_prompt[None], v_p.reshape(n_b, seq, H_A, DV_A)[None], s_p[None],
            rw_p.reshape(n_b, seq, SHIFT_W)[:, -1][None],
            k_sample[None], v_s.reshape(n_d, n_dec, H_A, DV_A)[None], s_s[None],
            rw_s.reshape(n_d, n_dec, SHIFT_W)[:, -1][None])
```

```python
import functools
import math

import jax
import jax.numpy as jnp
from jax import lax
from jax.experimental import pallas as pl
from jax.experimental.pallas import tpu as pltpu

F32 = jnp.float32
BF16 = jnp.bfloat16

D_MODEL = 1024
D_ATT = 512
D_RWKV = 512
DK_A = 64
H_A = 4
DV_A = 128
HS_R = 64
H_R = 8
R_LORA = 64
SHIFT_W = 3 * D_RWKV + 2 * R_LORA
ROPE_THETA = 10000.0
RMS_EPS = 1e-6
GN_EPS = 64e-5
L2_EPS = 1e-12
LAM_INIT = 0.8 - 0.6 * math.exp(-0.3 * 0)

ROW_TILE = 256
ATT_TILE = 512
LOG2E = math.log2(math.e)
SCAN_SEQS = 4
SCAN_HEADS = 2
SCAN_CHUNK = 128
CHUNK = HS_R
PAGES_PER_STEP = 16
VMEM_LIMIT = 48 * 1024 * 1024

_NT = (((1,), (1,)), ((), ()))


def _dot(a, b):
    return jnp.dot(a, b, preferred_element_type=F32)


def _segsum(x, bd):
    hi = x.astype(BF16)
    lo = (x - hi.astype(F32)).astype(BF16)
    return _dot(hi, bd) + _dot(lo, bd)


def _diff_lambda(lq1, lk1, lq2, lk2):
    return (jnp.exp(jnp.sum(lq1 * lk1, axis=1, keepdims=True))
            - jnp.exp(jnp.sum(lq2 * lk2, axis=1, keepdims=True)) + LAM_INIT)


def _subln_gate(o0, o1, lam, subg, gate):
    y = o0 - lam * o1
    y = y * lax.rsqrt(jnp.mean(y * y, axis=-1, keepdims=True) + RMS_EPS) * subg
    return (y * (1.0 - LAM_INIT)) * gate


def _proj_kernel(x_ref, lng_ref, wq_ref, wk_ref, wr_ref, cos_ref, sin_ref, qg_ref, kg_ref,
                 q_ref, kt_ref, ktb_ref, v_ref, vb_ref, ga_ref, rw_ref, gr_ref):
    tm = x_ref.shape[0]
    x = x_ref[...]
    h = x * lax.rsqrt(jnp.mean(x * x, axis=-1, keepdims=True) + RMS_EPS) * lng_ref[...]
    h = h.astype(BF16)
    cos = cos_ref[...][None]
    sin = sin_ref[...][None]

    def norm_rope(t, g):
        t3 = t.reshape(2 * H_A, DK_A, tm)
        t3 = t3 * lax.rsqrt(jnp.mean(t3 * t3, axis=1, keepdims=True) + RMS_EPS) * g[None]
        x1 = t3[:, :DK_A // 2]
        x2 = t3[:, DK_A // 2:]
        o = jnp.concatenate([x1 * cos - x2 * sin, x2 * cos + x1 * sin], axis=1)
        return o.reshape(D_ATT, tm)

    qt = lax.dot_general(wq_ref[...], h, _NT, preferred_element_type=F32)
    qt = norm_rope(qt, qg_ref[...]) * (DK_A ** -0.5 * LOG2E)
    q_ref[...] = qt.T.astype(BF16)
    kt = lax.dot_general(wk_ref[...], h, _NT, preferred_element_type=F32)
    kt = norm_rope(kt, kg_ref[...])
    kt_ref[...] = kt
    ktb_ref[...] = kt.astype(BF16)

    v = _dot(h, wr_ref[:, 0:D_ATT])
    v_ref[...] = v
    vb_ref[...] = v.astype(BF16)
    za = _dot(h, wr_ref[:, D_ATT:2 * D_ATT])
    ga_ref[...] = za * jax.nn.sigmoid(za)
    rw_ref[...] = _dot(h, wr_ref[:, 2 * D_ATT:2 * D_ATT + SHIFT_W])
    zr = _dot(h, wr_ref[:, 2 * D_ATT + SHIFT_W:])
    gr_ref[...] = zr * jax.nn.sigmoid(zr)


def _proj(x, ln_g, wq_t, wk_t, w_rest, cos_t, sin_t, qg, kg, n_seq, seq_len):
    rows = x.shape[0]
    tm = ROW_TILE
    nsi = seq_len // tm
    tk = min(ATT_TILE, seq_len)
    per_tk = tk // tm
    const = lambda i: (0, 0)
    row_blk = lambda w: pl.BlockSpec((tm, w), lambda i: (i, 0))
    out_shape = (
        jax.ShapeDtypeStruct((rows, D_ATT), BF16),
        jax.ShapeDtypeStruct((n_seq, D_ATT, seq_len), F32),
        jax.ShapeDtypeStruct((n_seq, seq_len // tk, D_ATT, tk), BF16),
        jax.ShapeDtypeStruct((rows, D_ATT), F32),
        jax.ShapeDtypeStruct((rows, D_ATT), BF16),
        jax.ShapeDtypeStruct((rows, D_ATT), F32),
        jax.ShapeDtypeStruct((rows, SHIFT_W), F32),
        jax.ShapeDtypeStruct((rows, D_RWKV), F32),
    )
    out_specs = (
        row_blk(D_ATT),
        pl.BlockSpec((None, D_ATT, tm), lambda i: (i // nsi, 0, i % nsi)),
        pl.BlockSpec((None, None, D_ATT, tm), lambda i: (i // nsi, (i % nsi) // per_tk, 0, i % per_tk)),
        row_blk(D_ATT), row_blk(D_ATT), row_blk(D_ATT), row_blk(SHIFT_W), row_blk(D_RWKV),
    )
    in_specs = [
        row_blk(D_MODEL),
        pl.BlockSpec((1, D_MODEL), const),
        pl.BlockSpec(wq_t.shape, const),
        pl.BlockSpec(wk_t.shape, const),
        pl.BlockSpec(w_rest.shape, const),
        pl.BlockSpec((DK_A // 2, tm), lambda i: (0, i)),
        pl.BlockSpec((DK_A // 2, tm), lambda i: (0, i)),
        pl.BlockSpec((DK_A, 1), const),
        pl.BlockSpec((DK_A, 1), const),
    ]
    return pl.pallas_call(
        _proj_kernel, out_shape=out_shape, grid=(rows // tm,), in_specs=in_specs, out_specs=out_specs,
        compiler_params=pltpu.CompilerParams(dimension_semantics=("parallel",), vmem_limit_bytes=VMEM_LIMIT),
        name="proj",
    )(x, ln_g, wq_t, wk_t, w_rest, cos_t, sin_t, qg, kg)


def _attn_prompt_kernel(q_ref, kt_ref, v_ref, ga_ref, lq1_ref, lk1_ref, lq2_ref, lk2_ref, subg_ref,
                        o_ref, m_sc, acc_sc):
    tq = q_ref.shape[0]
    tk = kt_ref.shape[2]
    i = pl.program_id(2)
    q = q_ref[...]
    lane = lax.broadcasted_iota(jnp.int32, q.shape, 1)
    zero = jnp.zeros_like(q)
    qs = jnp.concatenate([jnp.where(lane < DK_A, q, zero), jnp.where(lane >= DK_A, q, zero)], axis=0)
    m_sc[...] = jnp.full(m_sc.shape, -jnp.inf, F32)
    acc_sc[...] = jnp.zeros(acc_sc.shape, F32)
    ones = jnp.ones((tk, DV_A), BF16)

    def step(j, masked):
        kj = kt_ref[j]
        vj = v_ref[pl.ds(pl.multiple_of(j * tk, tk), tk), :]
        s = _dot(qs, kj)
        if masked:
            row = lax.broadcasted_iota(jnp.int32, s.shape, 0)
            col = lax.broadcasted_iota(jnp.int32, s.shape, 1)
            qpos = jnp.where(row >= tq, row - tq, row)
            s = jnp.where(col <= qpos, s, -jnp.inf)
        m_prev = m_sc[...]
        m_new = jnp.maximum(m_prev, jnp.max(s, axis=1, keepdims=True))
        alpha = jnp.exp2(m_prev - m_new)
        p = jnp.exp2(s - jnp.tile(m_new, (1, tk // DV_A)))
        acc_sc[...] = (jnp.tile(alpha, (1, 2)) * acc_sc[...]
                       + _dot(p.astype(BF16), jnp.concatenate([vj, ones], axis=1)))
        m_sc[...] = m_new

    def body(j, c):
        step(j, False)
        return c

    lax.fori_loop(0, i, body, 0)
    step(i, True)

    acc = acc_sc[...]
    o = acc[:, :DV_A] / acc[:, DV_A:]
    lam = _diff_lambda(lq1_ref[...], lk1_ref[...], lq2_ref[...], lk2_ref[...])
    o_ref[...] = _subln_gate(o[:tq], o[tq:], lam, subg_ref[...], ga_ref[...]).astype(BF16)


def _attn_prompt(q_b, ktb, v_b, ga, lq1, lk1, lq2, lk2, subg, n_seq, seq_len):
    tq = ATT_TILE
    nq = seq_len // tq
    rows = n_seq * seq_len
    vec = lambda w: pl.BlockSpec((1, w), lambda b, h, i: (0, 0))
    in_specs = [
        pl.BlockSpec((tq, DV_A), lambda b, h, i: (b * nq + i, h)),
        pl.BlockSpec((None, nq, DV_A, tq), lambda b, h, i: (b, 0, h, 0)),
        pl.BlockSpec((seq_len, DV_A), lambda b, h, i: (b, h)),
        pl.BlockSpec((tq, DV_A), lambda b, h, i: (b * nq + i, h)),
        vec(DK_A), vec(DK_A), vec(DK_A), vec(DK_A), vec(DV_A),
    ]
    return pl.pallas_call(
        _attn_prompt_kernel,
        out_shape=jax.ShapeDtypeStruct((rows, D_ATT), BF16),
        grid=(n_seq, H_A, nq), in_specs=in_specs,
        out_specs=pl.BlockSpec((tq, DV_A), lambda b, h, i: (b * nq + i, h)),
        scratch_shapes=[pltpu.VMEM((2 * tq, DV_A), F32), pltpu.VMEM((2 * tq, 2 * DV_A), F32)],
        compiler_params=pltpu.CompilerParams(dimension_semantics=("parallel", "parallel", "arbitrary"),
                                             vmem_limit_bytes=VMEM_LIMIT),
        name="attn_prompt",
    )(q_b, ktb, v_b, ga, lq1, lk1, lq2, lk2, subg)


def _attn_sample_kernel(pt_ref, q8_ref, kn_ref, vn_ref, ga_ref, lq1_ref, lk1_ref, lq2_ref, lk2_ref,
                        subg_ref, *refs):
    npg = PAGES_PER_STEP
    k_refs = refs[:npg]
    v_refs = refs[npg:2 * npg]
    o_ref, m_sc, l_sc, acc_sc = refs[2 * npg:]
    g = pl.program_id(1)
    n_dec = q8_ref.shape[0] // 2

    @pl.when(g == 0)
    def _():
        m_sc[...] = jnp.full(m_sc.shape, -jnp.inf, F32)
        l_sc[...] = jnp.zeros(l_sc.shape, F32)
        acc_sc[...] = jnp.zeros(acc_sc.shape, F32)

    rowi = lax.broadcasted_iota(jnp.int32, (2 * n_dec, DV_A), 0)
    lane = lax.broadcasted_iota(jnp.int32, (2 * n_dec, DV_A), 1)
    sel = ((rowi < n_dec) & (lane < DK_A)) | ((rowi >= n_dec) & (lane >= DK_A))
    q8 = q8_ref[...]

    def head_q(h):
        qh = q8[:, h * DV_A:(h + 1) * DV_A]
        return jnp.where(sel, qh, jnp.zeros_like(qh))

    page = k_refs[0].shape[-1]
    s = jnp.concatenate(
        [jnp.concatenate([_dot(head_q(h), k_refs[i][h].astype(BF16).reshape(2 * DK_A, -1))
                          for i in range(npg)], axis=1) for h in range(H_A)], axis=0)
    m_prev = m_sc[...]
    m_new = jnp.maximum(m_prev, jnp.max(s, axis=1, keepdims=True))
    alpha = jnp.exp2(m_prev - m_new)
    p = jnp.exp2(s - m_new)
    l_sc[...] = alpha * l_sc[...] + jnp.sum(p, axis=1, keepdims=True)
    m_sc[...] = m_new
    pb = p.astype(BF16)
    rows_h = 2 * n_dec
    for h in range(H_A):
        acc = alpha[h * rows_h:(h + 1) * rows_h] * acc_sc[h]
        for i in range(npg):
            vh = v_refs[i][pl.ds(h, page, stride=H_A), :]
            acc = acc + _dot(pb[h * rows_h:(h + 1) * rows_h, i * page:(i + 1) * page], vh.astype(BF16))
        acc_sc[h] = acc

    @pl.when(g == pl.num_programs(1) - 1)
    def _():
        lam = _diff_lambda(lq1_ref[...], lk1_ref[...], lq2_ref[...], lk2_ref[...])
        t_idx = jnp.where(rowi[:, :1] >= n_dec, rowi[:, :1] - n_dec, rowi[:, :1])
        for h in range(H_A):
            hs = slice(h * DV_A, (h + 1) * DV_A)
            qf = head_q(h).astype(F32)
            s_new = []
            for j in range(n_dec):
                sj = jnp.sum(qf * kn_ref[j:j + 1, hs], axis=1, keepdims=True)
                s_new.append(jnp.where(j <= t_idx, sj, -jnp.inf))
            m_prev = m_sc[h * rows_h:(h + 1) * rows_h]
            m_new = m_prev
            for sj in s_new:
                m_new = jnp.maximum(m_new, sj)
            alpha = jnp.exp2(m_prev - m_new)
            l = alpha * l_sc[h * rows_h:(h + 1) * rows_h]
            acc = alpha * acc_sc[h]
            for j, sj in enumerate(s_new):
                pj = jnp.exp2(sj - m_new)
                l = l + pj
                acc = acc + pj * vn_ref[j:j + 1, hs]
            o = acc / l
            o_ref[:, hs] = _subln_gate(o[:n_dec], o[n_dec:], lam, subg_ref[...], ga_ref[:, hs])


def _attn_sample(page_table, q8, k_new, v_new, ga, lq1, lk1, lq2, lk2, subg, kt_pages, v_pages):
    n_b, n_pages = page_table.shape
    n_dec = k_new.shape[1]
    npg = PAGES_PER_STEP
    page = kt_pages.shape[-1]
    vec = lambda w: pl.BlockSpec((1, w), lambda b, g, pt: (0, 0))
    per_b = lambda r: pl.BlockSpec((None, r, D_ATT), lambda b, g, pt: (b, 0, 0))

    def k_spec(i):
        return pl.BlockSpec((None, H_A, 2, DK_A, page), lambda b, g, pt: (pt[b, g * npg + i], 0, 0, 0, 0))

    def v_spec(i):
        return pl.BlockSpec((None, page * H_A, DV_A), lambda b, g, pt: (pt[b, g * npg + i], 0, 0))

    in_specs = ([per_b(2 * n_dec), per_b(n_dec), per_b(n_dec), per_b(n_dec),
                 vec(DK_A), vec(DK_A), vec(DK_A), vec(DK_A), vec(DV_A)]
                + [k_spec(i) for i in range(npg)] + [v_spec(i) for i in range(npg)])
    grid_spec = pltpu.PrefetchScalarGridSpec(
        num_scalar_prefetch=1, grid=(n_b, n_pages // npg), in_specs=in_specs,
        out_specs=per_b(n_dec),
        scratch_shapes=[pltpu.VMEM((H_A * 2 * n_dec, 1), F32), pltpu.VMEM((H_A * 2 * n_dec, 1), F32),
                        pltpu.VMEM((H_A, 2 * n_dec, DV_A), F32)])
    return pl.pallas_call(
        _attn_sample_kernel, out_shape=jax.ShapeDtypeStruct((n_b, n_dec, D_ATT), F32), grid_spec=grid_spec,
        compiler_params=pltpu.CompilerParams(dimension_semantics=("parallel", "arbitrary"),
                                             vmem_limit_bytes=VMEM_LIMIT),
        name="attn_sample",
    )(page_table, q8, k_new, v_new, ga, lq1, lk1, lq2, lk2, subg,
      *([kt_pages] * npg), *([v_pages] * npg))


def _rwkv_pre_kernel(seq_len, rw_ref, prev_ref, first_ref, mu_ref, w0_ref, wup_ref, a0_ref, aup_ref,
                     kk_ref, ka_ref, rk_ref, bd_ref,
                     r_out, w_out, k_out, v_out, nkk_out, b_out, bonus_out):
    tm = rw_ref.shape[0]
    rw = rw_ref[...]
    rolled = pltpu.roll(rw, 1, axis=0)
    rowi = lax.broadcasted_iota(jnp.int32, (tm, 1), 0)
    if seq_len >= tm:
        at_start = (pl.program_id(0) % (seq_len // tm)) == 0
        row0 = jnp.where(at_start, first_ref[...], prev_ref[7:8, :])
        prev = jnp.where(rowi == 0, row0, rolled)
    else:
        prev = jnp.where(rowi % seq_len == 0, first_ref[...], rolled)
    u = rw + (prev - rw) * mu_ref[...]
    r = u[:, 0:D_RWKV]
    k = u[:, D_RWKV:2 * D_RWKV]
    v = u[:, 2 * D_RWKV:3 * D_RWKV]
    wd = u[:, 3 * D_RWKV:3 * D_RWKV + R_LORA]
    ad = u[:, 3 * D_RWKV + R_LORA:]
    w_log = -jax.nn.softplus(-(w0_ref[...] + _dot(jnp.tanh(wd).astype(BF16), wup_ref[...]))) - 0.5
    log_decay = -jnp.exp(w_log)
    a = jax.nn.sigmoid(a0_ref[...] + _dot(ad.astype(BF16), aup_ref[...]))
    bd = bd_ref[...]
    kk = k * kk_ref[...]
    kk = kk / jnp.maximum(jnp.sqrt(_segsum(kk * kk, bd)), L2_EPS)
    k2 = k * (1.0 + (a - 1.0) * ka_ref[...])
    r_out[...] = r
    w_out[...] = log_decay
    k_out[...] = k2
    v_out[...] = v
    nkk_out[...] = -kk
    b_out[...] = kk * a
    bonus_out[...] = _segsum(r * k2 * rk_ref[...], bd) * v


def _rwkv_pre(rw, first, seq_len, mu, w0, wup, a0, aup, k_k, k_a, r_k, bd):
    rows = rw.shape[0]
    tm = ROW_TILE
    const = lambda i: (0, 0)
    row_blk = lambda w: pl.BlockSpec((tm, w), lambda i: (i, 0))
    if seq_len >= tm:
        first_spec = pl.BlockSpec((None, 1, SHIFT_W), lambda i: (i * tm // seq_len, 0, 0))
    else:
        first_spec = row_blk(SHIFT_W)
    in_specs = [
        row_blk(SHIFT_W),
        pl.BlockSpec((8, SHIFT_W), lambda i: (jnp.maximum(i * (tm // 8) - 1, 0), 0)),
        first_spec,
        pl.BlockSpec((1, SHIFT_W), const),
        pl.BlockSpec((1, D_RWKV), const), pl.BlockSpec((R_LORA, D_RWKV), const),
        pl.BlockSpec((1, D_RWKV), const), pl.BlockSpec((R_LORA, D_RWKV), const),
        pl.BlockSpec((1, D_RWKV), const), pl.BlockSpec((1, D_RWKV), const), pl.BlockSpec((1, D_RWKV), const),
        pl.BlockSpec((D_RWKV, D_RWKV), const),
    ]
    out = jax.ShapeDtypeStruct((rows, D_RWKV), F32)
    return pl.pallas_call(
        functools.partial(_rwkv_pre_kernel, seq_len),
        out_shape=(out,) * 7, grid=(rows // tm,), in_specs=in_specs, out_specs=(row_blk(D_RWKV),) * 7,
        compiler_params=pltpu.CompilerParams(dimension_semantics=("parallel",), vmem_limit_bytes=VMEM_LIMIT),
        name="rwkv_pre",
    )(rw, rw, first, mu, w0, wup, a0, aup, k_k, k_a, r_k, bd)


def _rwkv_scan_kernel(r_ref, w_ref, k_ref, v_ref, nkk_ref, b_ref, s0_ref, gng_ref, gnb_ref,
                      y_ref, s_ref):
    nb, tc = r_ref.shape[0], r_ref.shape[1]
    n_grp = H_R // SCAN_HEADS
    gw = SCAN_HEADS * HS_R
    c = pl.program_id(1)

    @pl.when(c == 0)
    def _():
        s_ref[...] = s0_ref[...]

    lane = lax.broadcasted_iota(jnp.int32, (HS_R, gw), 1)
    row = lax.broadcasted_iota(jnp.int32, (HS_R, gw), 0)
    head_of_lane = lane // HS_R
    diag_of_head = [lane == row + e * HS_R for e in range(SCAN_HEADS)]
    head_of_lane1 = lax.broadcasted_iota(jnp.int32, (1, gw), 1) // HS_R
    row8 = lax.broadcasted_iota(jnp.int32, (H_R, gw), 0)
    lane8 = lax.broadcasted_iota(jnp.int32, (H_R, gw), 1)

    unroll = min(8, tc)

    def steps(g, carry):
        t0 = pl.multiple_of(g * unroll, unroll)
        rows = pl.ds(t0, unroll)
        for j in range(unroll):
            for b in range(nb):
                y8 = None
                for p in range(n_grp):
                    ls = slice(p * gw, (p + 1) * gw)
                    row = lambda ref: ref[b, rows, ls][j:j + 1]
                    s = s_ref[b, p]
                    nkk = row(nkk_ref)
                    sa = None
                    for e in range(SCAN_HEADS):
                        in_head = head_of_lane1 == e
                        part = jnp.sum(s * jnp.where(in_head, nkk, 0.0), axis=1, keepdims=True)
                        sa = part if sa is None else jnp.where(head_of_lane == e, part, sa)
                    vr = row(v_ref)
                    vc = None
                    for e in range(SCAN_HEADS):
                        part = jnp.sum(jnp.where(diag_of_head[e], vr, 0.0), axis=1, keepdims=True)
                        vc = part if vc is None else jnp.where(head_of_lane == e, part, vc)
                    s = s * jnp.exp(row(w_ref)) + sa * row(b_ref) + vc * row(k_ref)
                    s_ref[b, p] = s
                    sel = row8 == p * SCAN_HEADS + lane8 // HS_R
                    r2 = jnp.where(sel, jnp.broadcast_to(row(r_ref), (H_R, gw)), 0.0).astype(BF16)
                    yp = lax.dot_general(r2, s.astype(BF16), _NT, preferred_element_type=F32)
                    y8 = yp if y8 is None else y8 + yp
                y_ref[b, t0 + j] = y8
        return carry

    lax.fori_loop(0, tc // unroll, steps, 0)

    y = y_ref[...]
    mean = jnp.mean(y, axis=-1, keepdims=True)
    var = jnp.mean(jnp.square(y - mean), axis=-1, keepdims=True)
    y_ref[...] = (y - mean) * lax.rsqrt(var + GN_EPS) * gng_ref[...] + gnb_ref[...]


def _rwkv_scan(r, w, k, v, nkk, b, s0, gn_g, gn_b, n_seq, seq_len):
    nb = SCAN_SEQS
    tc = min(SCAN_CHUNK, seq_len)
    n_grp = H_R // SCAN_HEADS
    gw = SCAN_HEADS * HS_R
    seq = lambda a: a.reshape(n_seq, seq_len, D_RWKV)
    in_blk = pl.BlockSpec((nb, tc, D_RWKV), lambda g, c: (g, c, 0))
    st_blk = pl.BlockSpec((nb, n_grp, HS_R, gw), lambda g, c: (g, 0, 0, 0))
    gn_blk = pl.BlockSpec((1, 1, H_R, HS_R), lambda g, c: (0, 0, 0, 0))
    return pl.pallas_call(
        _rwkv_scan_kernel,
        out_shape=(jax.ShapeDtypeStruct((n_seq, seq_len, H_R, HS_R), F32),
                   jax.ShapeDtypeStruct((n_seq, n_grp, HS_R, gw), F32)),
        grid=(n_seq // nb, seq_len // tc),
        in_specs=[in_blk] * 6 + [st_blk, gn_blk, gn_blk],
        out_specs=(pl.BlockSpec((nb, tc, H_R, HS_R), lambda g, c: (g, c, 0, 0)), st_blk),
        compiler_params=pltpu.CompilerParams(dimension_semantics=("parallel", "arbitrary"),
                                             vmem_limit_bytes=VMEM_LIMIT),
        name="rwkv_scan",
    )(seq(r), seq(w), seq(k), seq(v), seq(nkk), seq(b), s0,
      gn_g.reshape(1, 1, H_R, HS_R), gn_b.reshape(1, 1, H_R, HS_R))


def _rwkv_chunk_kernel(r_ref, lw_ref, k_ref, v_ref, nkk_ref, b_ref, gng_ref, gnb_ref, y_ref, s_ref):
    nb, C = r_ref.shape[0], r_ref.shape[1]
    n_pair = H_R // 2
    pw = 2 * HS_R
    c = pl.program_id(1)

    @pl.when(c == 0)
    def _():
        s_ref[...] = jnp.zeros(s_ref.shape, F32)

    lane = lax.broadcasted_iota(jnp.int32, (C, pw), 1)
    t_idx = lax.broadcasted_iota(jnp.int32, (C, pw), 0)
    head0 = lane < HS_R
    s_idx = lane % C
    strict = t_idx > s_idx
    incl = t_idx >= s_idx
    eye = jnp.where(t_idx == s_idx, 1.0, 0.0).astype(F32)
    tri = jnp.where(lax.broadcasted_iota(jnp.int32, (C, C), 0) >= lax.broadcasted_iota(jnp.int32, (C, C), 1),
                    1.0, 0.0).astype(BF16)
    same_head = (lax.broadcasted_iota(jnp.int32, (pw, pw), 0) // HS_R
                 == lax.broadcasted_iota(jnp.int32, (pw, pw), 1) // HS_R)

    def stack(x):
        return jnp.concatenate([jnp.where(head0, x, 0.0), jnp.where(head0, 0.0, x)], axis=0).astype(BF16)

    def per_head(coef, x):
        return _dot(coef.astype(BF16), stack(x))

    def head_mean(x):
        return jnp.where(head0, jnp.sum(jnp.where(head0, x, 0.0), axis=1, keepdims=True),
                         jnp.sum(jnp.where(head0, 0.0, x), axis=1, keepdims=True)) * (1.0 / HS_R)

    units = [(b, p) for b in range(nb) for p in range(n_pair)]
    ls = lambda p: slice(p * pw, (p + 1) * pw)
    load = lambda ref: [ref[b, :, ls(p)] for b, p in units]
    lw = load(lw_ref)
    lw_hi = [x.astype(BF16) for x in lw]
    cum = [_dot(tri, h) + _dot(tri, (x - h.astype(F32)).astype(BF16)) for x, h in zip(lw, lw_hi)]
    p_in = [jnp.exp(x) for x in cum]
    p_inv = [jnp.exp(-x) for x in cum]
    nt = [x * jnp.exp(cm - l) for x, cm, l in zip(load(nkk_ref), cum, lw)]
    bt = [x * pi for x, pi in zip(load(b_ref), p_inv)]
    kt = [x * pi for x, pi in zip(load(k_ref), p_inv)]
    rt = [x * pp for x, pp in zip(load(r_ref), p_in)]
    vv = load(v_ref)

    lhs = [jnp.concatenate([n_, r_], axis=0).astype(BF16) for n_, r_ in zip(nt, rt)]
    coef = [lax.dot_general(l_, jnp.concatenate([stack(b_), stack(k_)], axis=0), _NT,
                            preferred_element_type=F32) for l_, b_, k_ in zip(lhs, bt, kt)]
    a_pow = [jnp.where(strict, cf[:C, :pw], 0.0) for cf in coef]
    b_m = [jnp.where(strict, cf[:C, pw:], 0.0) for cf in coef]
    ab_m = [jnp.where(incl, cf[C:, :pw], 0.0) for cf in coef]
    ak_m = [jnp.where(incl, cf[C:, pw:], 0.0) for cf in coef]

    inv = [eye + a for a in a_pow]
    for _ in range(C.bit_length() - 2):
        a_pow = [per_head(a, a) for a in a_pow]
        inv = [per_head(i_, eye + a) for i_, a in zip(inv, a_pow)]

    v_stack = [stack(x) for x in vv]
    bv = [_dot(m.astype(BF16), vs) for m, vs in zip(b_m, v_stack)]
    akv = [_dot(m.astype(BF16), vs) for m, vs in zip(ak_m, v_stack)]

    st = [s_ref[b, p] for b, p in units]
    ns_rs = [lax.dot_general(l_, s_.astype(BF16), _NT, preferred_element_type=F32)
             for l_, s_ in zip(lhs, st)]
    u = [per_head(i_, x[:C] + bv_) for i_, x, bv_ in zip(inv, ns_rs, bv)]
    y = [x[C:] + per_head(m, u_) + akv_ for x, m, u_, akv_ in zip(ns_rs, ab_m, u, akv)]

    for i, (b, p) in enumerate(units):
        uv_t = jnp.concatenate([u[i], vv[i]], axis=0).T.astype(BF16)
        upd = _dot(uv_t, jnp.concatenate([bt[i], kt[i]], axis=0).astype(BF16))
        s_ref[b, p] = jnp.where(same_head, (st[i] + upd) * p_in[i][C - 1:C, :], 0.0)

    for i, (b, p) in enumerate(units):
        d = y[i] - head_mean(y[i])
        var = head_mean(d * d)
        y_ref[b, :, ls(p)] = d * lax.rsqrt(var + GN_EPS) * gng_ref[:, ls(p)] + gnb_ref[:, ls(p)]


def _rwkv_chunk(r, lw, k, v, nkk, b, gn_g, gn_b, n_seq, seq_len):
    nb = SCAN_SEQS
    C = CHUNK
    n_pair = H_R // 2
    seq = lambda a: a.reshape(n_seq, seq_len, D_RWKV)
    in_blk = pl.BlockSpec((nb, C, D_RWKV), lambda g, c: (g, c, 0))
    gn_blk = pl.BlockSpec((1, D_RWKV), lambda g, c: (0, 0))
    return pl.pallas_call(
        _rwkv_chunk_kernel,
        out_shape=(jax.ShapeDtypeStruct((n_seq, seq_len, D_RWKV), F32),
                   jax.ShapeDtypeStruct((n_seq, n_pair, 2 * HS_R, 2 * HS_R), F32)),
        grid=(n_seq // nb, seq_len // C),
        in_specs=[in_blk] * 6 + [gn_blk, gn_blk],
        out_specs=(in_blk, pl.BlockSpec((nb, n_pair, 2 * HS_R, 2 * HS_R), lambda g, c: (g, 0, 0, 0))),
        compiler_params=pltpu.CompilerParams(dimension_semantics=("parallel", "arbitrary"),
                                             vmem_limit_bytes=VMEM_LIMIT),
        name="rwkv_chunk",
    )(seq(r), seq(lw), seq(k), seq(v), seq(nkk), seq(b), gn_g[None], gn_b[None])


def _out_proj_kernel(oa_ref, yg_ref, bonus_ref, gr_ref, x_ref, wo_ref, y_ref):
    mix_r = ((yg_ref[...] + bonus_ref[...]) * gr_ref[...]).astype(BF16)
    y_ref[...] = (x_ref[...] + _dot(oa_ref[...].astype(BF16), wo_ref[0:D_ATT, :])
                  + _dot(mix_r, wo_ref[D_ATT:, :]))


def _out_proj(oa, yg, bonus, gr, x, wo):
    rows = x.shape[0]
    tm = ROW_TILE
    row_blk = lambda w: pl.BlockSpec((tm, w), lambda i: (i, 0))
    return pl.pallas_call(
        _out_proj_kernel, out_shape=jax.ShapeDtypeStruct((rows, D_MODEL), F32), grid=(rows // tm,),
        in_specs=[row_blk(D_ATT), row_blk(D_RWKV), row_blk(D_RWKV), row_blk(D_RWKV), row_blk(D_MODEL),
                  pl.BlockSpec(wo.shape, lambda i: (0, 0))],
        out_specs=row_blk(D_MODEL),
        compiler_params=pltpu.CompilerParams(dimension_semantics=("parallel",), vmem_limit_bytes=VMEM_LIMIT),
        name="out_proj",
    )(oa, yg, bonus, gr, x, wo)


def _rope_tables(pos):
    half = DK_A // 2
    inv = 1.0 / (ROPE_THETA ** (jnp.arange(half, dtype=F32) / half))
    ang = inv[:, None] * pos.astype(F32)[None, :]
    return jnp.cos(ang), jnp.sin(ang)


def _pair_state(s):
    n = s.shape[0]
    g = H_R // SCAN_HEADS
    return (s.reshape(n, g, SCAN_HEADS, HS_R, HS_R).transpose(0, 1, 3, 2, 4)
            .reshape(n, g, HS_R, SCAN_HEADS * HS_R))


def _unpair_state(s):
    n = s.shape[0]
    g = H_R // SCAN_HEADS
    return s.reshape(n, g, HS_R, SCAN_HEADS, HS_R).transpose(0, 1, 3, 2, 4).reshape(n, H_R, HS_R, HS_R)


def kernel(x_prompt, x_sample, cache_k, cache_v, state_wkv, state_shift, page_table, ln_g, w_in, q_norm_g,
           k_norm_g, lambda_q1, lambda_k1, lambda_q2, lambda_k2, subln_g, shift_mu, w0, w_lora_up, a0,
           a_lora_up, k_k, k_a, r_k, gn_g, gn_b, w_out):
    n_b, seq = x_prompt.shape[:2]
    n_d, n_dec = x_sample.shape[:2]
    page = cache_k.shape[2]
    past = page_table.shape[1] * page
    layer = 0

    w_in_b = w_in[layer].astype(BF16)
    wq_t = w_in_b[:, 0:D_ATT].T
    wk_t = w_in_b[:, D_ATT:2 * D_ATT].T
    w_rest = w_in_b[:, 2 * D_ATT:]
    wo = w_out[layer].astype(BF16)
    wup = w_lora_up[layer].astype(BF16)
    aup = a_lora_up[layer].astype(BF16)
    qg = q_norm_g[layer].reshape(DK_A, 1)
    kg = k_norm_g[layer].reshape(DK_A, 1)
    seg = jnp.arange(D_RWKV) // HS_R
    bd = (seg[:, None] == seg[None, :]).astype(BF16)
    lam_vecs = (lambda_q1[layer][None], lambda_k1[layer][None], lambda_q2[layer][None], lambda_k2[layer][None])
    subg = subln_g[layer][None]
    row = lambda a: a[layer][None]

    def scan_short(r, lw, k2, vv, nkk, bb, s0, n_seq, seq_len):
        yg, s_fin = _rwkv_scan(r, lw, k2, vv, nkk, bb, _pair_state(s0), gn_g[layer], gn_b[layer],
                               n_seq, seq_len)
        return yg.reshape(n_seq * seq_len, D_RWKV), _unpair_state(s_fin)

    def scan_chunked(r, lw, k2, vv, nkk, bb, s0, n_seq, seq_len):
        yg, s_fin = _rwkv_chunk(r, lw, k2, vv, nkk, bb, gn_g[layer], gn_b[layer], n_seq, seq_len)
        s6 = s_fin.reshape(n_seq, H_R // 2, 2, HS_R, 2, HS_R)
        s_heads = jnp.stack([s6[:, :, e, :, e, :] for e in range(2)], axis=2)
        return yg.reshape(n_seq * seq_len, D_RWKV), s_heads.reshape(n_seq, H_R, HS_R, HS_R)

    def branch(x, pos, n_seq, seq_len, proj_seqs, first, s0, attend, scan):
        rows = n_seq * seq_len
        cos_t, sin_t = _rope_tables(pos)
        q_b, kt, ktb, v, v_b, ga, rw, gr = _proj(x.reshape(rows, D_MODEL), ln_g[layer][None], wq_t, wk_t,
                                                 w_rest, cos_t, sin_t, qg, kg, proj_seqs, rows // proj_seqs)
        out_a = attend(q_b, kt, ktb, v, v_b, ga)
        r, w, k2, vv, nkk, bb, bonus = _rwkv_pre(rw, first, seq_len, row(shift_mu), row(w0), wup, row(a0),
                                                 aup, row(k_k), row(k_a), row(r_k), bd)
        yg, s_fin = scan(r, w, k2, vv, nkk, bb, s0, n_seq, seq_len)
        y = _out_proj(out_a, yg, bonus, gr, x.reshape(rows, D_MODEL), wo)
        return y, kt, v, s_fin, rw

    def attend_prompt(q_b, kt, ktb, v, v_b, ga):
        return _attn_prompt(q_b, ktb, v_b, ga, *lam_vecs, subg, n_b, seq)

    pos_p = jnp.tile(jnp.arange(seq), n_b)
    yp, kt_p, v_p, s_p, rw_p = branch(x_prompt, pos_p, n_b, seq, n_b,
                                      jnp.zeros((n_b, 1, SHIFT_W), F32), None, attend_prompt, scan_chunked)

    rows_s = n_d * n_dec
    kt_pages = cache_k[layer].transpose(0, 2, 3, 4, 1)
    v_pages = cache_v[layer].reshape(-1, page * H_A, DV_A)

    def attend_sample(q_b, kt, ktb, v, v_b, ga):
        q3 = q_b.reshape(n_d, n_dec, D_ATT)
        q8 = jnp.concatenate([q3, q3], axis=1)
        k_new = kt.reshape(D_ATT, rows_s).T.reshape(n_d, n_dec, D_ATT)
        out = _attn_sample(page_table, q8, k_new, v.reshape(n_d, n_dec, D_ATT),
                           ga.reshape(n_d, n_dec, D_ATT), *lam_vecs, subg, kt_pages, v_pages)
        return out.reshape(rows_s, D_ATT)

    pos_s = jnp.tile(past + jnp.arange(n_dec), n_d)
    first_s = jnp.repeat(state_shift[layer], n_dec, axis=0)

    ys, kt_s, v_s, s_s, rw_s = branch(x_sample, pos_s, n_d, n_dec, 1, first_s, state_wkv[layer],
                                      attend_sample, scan_short)

    k_prompt = kt_p.reshape(n_b, H_A, 2, DK_A, seq).transpose(0, 4, 1, 2, 3)
    k_sample = kt_s.reshape(H_A, 2, DK_A, n_d, n_dec).transpose(3, 4, 0, 1, 2)
    return (yp.reshape(n_b, seq, D_MODEL), ys.reshape(n_d, n_dec, D_MODEL),
            k_prompt[None], v_p.reshape(n_b, seq, H_A, DV_A)[None], s_p[None],
            rw_p.reshape(n_b, seq, SHIFT_W)[:, -1][None],
            k_sample[None], v_s.reshape(n_d, n_dec, H_A, DV_A)[None], s_s[None],
            rw_s.reshape(n_d, n_dec, SHIFT_W)[:, -1][None])
```

```python
import functools
import math

import jax
import jax.numpy as jnp
from jax import lax
from jax.experimental import pallas as pl
from jax.experimental.pallas import tpu as pltpu

F32 = jnp.float32
BF16 = jnp.bfloat16

D_MODEL = 1024
D_ATT = 512
D_RWKV = 512
DK_A = 64
H_A = 4
DV_A = 128
HS_R = 64
H_R = 8
R_LORA = 64
SHIFT_W = 3 * D_RWKV + 2 * R_LORA
ROPE_THETA = 10000.0
RMS_EPS = 1e-6
GN_EPS = 64e-5
L2_EPS = 1e-12
LAM_INIT = 0.8 - 0.6 * math.exp(-0.3 * 0)

ROW_TILE = 256
ATT_TILE = 512
LOG2E = math.log2(math.e)
SCAN_SEQS = 4
SCAN_HEADS = 2
SCAN_CHUNK = 128
CHUNK = HS_R
PAGES_PER_STEP = 8
VMEM_LIMIT = 48 * 1024 * 1024

_NT = (((1,), (1,)), ((), ()))


def _dot(a, b):
    return jnp.dot(a, b, preferred_element_type=F32)


def _segsum(x, bd):
    hi = x.astype(BF16)
    lo = (x - hi.astype(F32)).astype(BF16)
    return _dot(hi, bd) + _dot(lo, bd)


def _diff_lambda(lq1, lk1, lq2, lk2):
    return (jnp.exp(jnp.sum(lq1 * lk1, axis=1, keepdims=True))
            - jnp.exp(jnp.sum(lq2 * lk2, axis=1, keepdims=True)) + LAM_INIT)


def _subln_gate(o0, o1, lam, subg, gate):
    y = o0 - lam * o1
    y = y * lax.rsqrt(jnp.mean(y * y, axis=-1, keepdims=True) + RMS_EPS) * subg
    return (y * (1.0 - LAM_INIT)) * gate


def _proj_kernel(x_ref, lng_ref, wq_ref, wk_ref, wr_ref, cos_ref, sin_ref, qg_ref, kg_ref,
                 q_ref, kt_ref, ktb_ref, v_ref, vb_ref, ga_ref, rw_ref, gr_ref):
    tm = x_ref.shape[0]
    x = x_ref[...]
    h = x * lax.rsqrt(jnp.mean(x * x, axis=-1, keepdims=True) + RMS_EPS) * lng_ref[...]
    h = h.astype(BF16)
    cos = cos_ref[...][None]
    sin = sin_ref[...][None]

    def norm_rope(t, g):
        t3 = t.reshape(2 * H_A, DK_A, tm)
        t3 = t3 * lax.rsqrt(jnp.mean(t3 * t3, axis=1, keepdims=True) + RMS_EPS) * g[None]
        x1 = t3[:, :DK_A // 2]
        x2 = t3[:, DK_A // 2:]
        o = jnp.concatenate([x1 * cos - x2 * sin, x2 * cos + x1 * sin], axis=1)
        return o.reshape(D_ATT, tm)

    qt = lax.dot_general(wq_ref[...], h, _NT, preferred_element_type=F32)
    qt = norm_rope(qt, qg_ref[...]) * (DK_A ** -0.5 * LOG2E)
    q_ref[...] = qt.T.astype(BF16)
    kt = lax.dot_general(wk_ref[...], h, _NT, preferred_element_type=F32)
    kt = norm_rope(kt, kg_ref[...])
    kt_ref[...] = kt
    ktb_ref[...] = kt.astype(BF16)

    v = _dot(h, wr_ref[:, 0:D_ATT])
    v_ref[...] = v
    vb_ref[...] = v.astype(BF16)
    za = _dot(h, wr_ref[:, D_ATT:2 * D_ATT])
    ga_ref[...] = za * jax.nn.sigmoid(za)
    rw_ref[...] = _dot(h, wr_ref[:, 2 * D_ATT:2 * D_ATT + SHIFT_W])
    zr = _dot(h, wr_ref[:, 2 * D_ATT + SHIFT_W:])
    gr_ref[...] = zr * jax.nn.sigmoid(zr)


def _proj(x, ln_g, wq_t, wk_t, w_rest, cos_t, sin_t, qg, kg, n_seq, seq_len):
    rows = x.shape[0]
    tm = ROW_TILE
    nsi = seq_len // tm
    tk = min(ATT_TILE, seq_len)
    per_tk = tk // tm
    const = lambda i: (0, 0)
    row_blk = lambda w: pl.BlockSpec((tm, w), lambda i: (i, 0))
    out_shape = (
        jax.ShapeDtypeStruct((rows, D_ATT), BF16),
        jax.ShapeDtypeStruct((n_seq, D_ATT, seq_len), F32),
        jax.ShapeDtypeStruct((n_seq, seq_len // tk, D_ATT, tk), BF16),
        jax.ShapeDtypeStruct((rows, D_ATT), F32),
        jax.ShapeDtypeStruct((rows, D_ATT), BF16),
        jax.ShapeDtypeStruct((rows, D_ATT), F32),
        jax.ShapeDtypeStruct((rows, SHIFT_W), F32),
        jax.ShapeDtypeStruct((rows, D_RWKV), F32),
    )
    out_specs = (
        row_blk(D_ATT),
        pl.BlockSpec((None, D_ATT, tm), lambda i: (i // nsi, 0, i % nsi)),
        pl.BlockSpec((None, None, D_ATT, tm), lambda i: (i // nsi, (i % nsi) // per_tk, 0, i % per_tk)),
        row_blk(D_ATT), row_blk(D_ATT), row_blk(D_ATT), row_blk(SHIFT_W), row_blk(D_RWKV),
    )
    in_specs = [
        row_blk(D_MODEL),
        pl.BlockSpec((1, D_MODEL), const),
        pl.BlockSpec(wq_t.shape, const),
        pl.BlockSpec(wk_t.shape, const),
        pl.BlockSpec(w_rest.shape, const),
        pl.BlockSpec((DK_A // 2, tm), lambda i: (0, i)),
        pl.BlockSpec((DK_A // 2, tm), lambda i: (0, i)),
        pl.BlockSpec((DK_A, 1), const),
        pl.BlockSpec((DK_A, 1), const),
    ]
    return pl.pallas_call(
        _proj_kernel, out_shape=out_shape, grid=(rows // tm,), in_specs=in_specs, out_specs=out_specs,
        compiler_params=pltpu.CompilerParams(dimension_semantics=("parallel",), vmem_limit_bytes=VMEM_LIMIT),
        name="proj",
    )(x, ln_g, wq_t, wk_t, w_rest, cos_t, sin_t, qg, kg)


def _attn_kernel(pt_ref, q8_ref, kn_ref, vn_ref, gas_ref, qp_ref, ktp_ref, vp_ref, gap_ref,
                 lq1_ref, lk1_ref, lq2_ref, lk2_ref, subg_ref, *refs):
    npg = PAGES_PER_STEP
    k_refs = refs[:npg]
    v_refs = refs[npg:2 * npg]
    os_ref, op_ref, m_sc, l_sc, acc_sc, qs_sc, m_p, acc_p = refs[2 * npg:]
    i = pl.program_id(2)
    j = pl.program_id(3)
    g = j
    n_dec = q8_ref.shape[0] // 2
    tq = qp_ref.shape[0]
    tk = ktp_ref.shape[1]

    @pl.when(j == 0)
    def _():
        m_sc[...] = jnp.full(m_sc.shape, -jnp.inf, F32)
        l_sc[...] = jnp.zeros(l_sc.shape, F32)
        acc_sc[...] = jnp.zeros(acc_sc.shape, F32)
        q = qp_ref[...]
        lane_q = lax.broadcasted_iota(jnp.int32, q.shape, 1)
        zero = jnp.zeros_like(q)
        qs_sc[...] = jnp.concatenate([jnp.where(lane_q < DK_A, q, zero), jnp.where(lane_q >= DK_A, q, zero)],
                                     axis=0)
        m_p[...] = jnp.full(m_p.shape, -jnp.inf, F32)
        acc_p[...] = jnp.zeros(acc_p.shape, F32)

    def prompt_tile(diagonal):
        sp = _dot(qs_sc[...], ktp_ref[...])
        if diagonal:
            row_p = lax.broadcasted_iota(jnp.int32, sp.shape, 0)
            col_p = lax.broadcasted_iota(jnp.int32, sp.shape, 1)
            sp = jnp.where(col_p <= jnp.where(row_p >= tq, row_p - tq, row_p), sp, -jnp.inf)
        mp_prev = m_p[...]
        mp_new = jnp.maximum(mp_prev, jnp.max(sp, axis=1, keepdims=True))
        alpha_p = jnp.exp2(mp_prev - mp_new)
        pp = jnp.exp2(sp - jnp.tile(mp_new, (1, tk // DV_A)))
        v_ext = jnp.concatenate([vp_ref[...], jnp.ones((tk, DV_A), BF16)], axis=1)
        acc_p[...] = jnp.tile(alpha_p, (1, 2)) * acc_p[...] + _dot(pp.astype(BF16), v_ext)
        m_p[...] = mp_new

    o_ref, ga_ref = os_ref, gas_ref
    rowi = lax.broadcasted_iota(jnp.int32, (2 * n_dec, DV_A), 0)
    lane = lax.broadcasted_iota(jnp.int32, (2 * n_dec, DV_A), 1)
    sel = ((rowi < n_dec) & (lane < DK_A)) | ((rowi >= n_dec) & (lane >= DK_A))
    rows_h = 2 * n_dec

    def head_q(h):
        qh = q8_ref[:, h * DV_A:(h + 1) * DV_A]
        return jnp.where(sel, qh, jnp.zeros_like(qh))

    def decode_pages():
        page = k_refs[0].shape[-1]
        s = jnp.concatenate(
            [jnp.concatenate([_dot(head_q(h), k_refs[n][h].astype(BF16).reshape(2 * DK_A, -1))
                              for n in range(npg)], axis=1) for h in range(H_A)], axis=0)
        m_prev = m_sc[...]
        m_new = jnp.maximum(m_prev, jnp.max(s, axis=1, keepdims=True))
        alpha = jnp.exp2(m_prev - m_new)
        p = jnp.exp2(s - m_new)
        l_sc[...] = alpha * l_sc[...] + jnp.sum(p, axis=1, keepdims=True)
        m_sc[...] = m_new
        pb = p.astype(BF16)
        for h in range(H_A):
            acc = alpha[h * rows_h:(h + 1) * rows_h] * acc_sc[h]
            for n in range(npg):
                vh = v_refs[n][pl.ds(h, page, stride=H_A), :]
                acc = acc + _dot(pb[h * rows_h:(h + 1) * rows_h, n * page:(n + 1) * page], vh.astype(BF16))
            acc_sc[h] = acc

    @pl.when(j < i)
    def _():
        prompt_tile(False)
        decode_pages()

    @pl.when(j == i)
    def _():
        prompt_tile(True)
        decode_pages()
        acc = acc_p[...]
        o = acc[:, :DV_A] / acc[:, DV_A:]
        lam = _diff_lambda(lq1_ref[...], lk1_ref[...], lq2_ref[...], lk2_ref[...])
        op_ref[...] = _subln_gate(o[:tq], o[tq:], lam, subg_ref[...], gap_ref[...]).astype(BF16)

    @pl.when(j > i)
    def _():
        decode_pages()

    @pl.when(g == pl.num_programs(3) - 1)
    def _():
        lam = _diff_lambda(lq1_ref[...], lk1_ref[...], lq2_ref[...], lk2_ref[...])
        t_idx = jnp.where(rowi[:, :1] >= n_dec, rowi[:, :1] - n_dec, rowi[:, :1])
        for h in range(H_A):
            hs = slice(h * DV_A, (h + 1) * DV_A)
            qf = head_q(h).astype(F32)
            s_new = []
            for j in range(n_dec):
                sj = jnp.sum(qf * kn_ref[j:j + 1, hs], axis=1, keepdims=True)
                s_new.append(jnp.where(j <= t_idx, sj, -jnp.inf))
            m_prev = m_sc[h * rows_h:(h + 1) * rows_h]
            m_new = m_prev
            for sj in s_new:
                m_new = jnp.maximum(m_new, sj)
            alpha = jnp.exp2(m_prev - m_new)
            l = alpha * l_sc[h * rows_h:(h + 1) * rows_h]
            acc = alpha * acc_sc[h]
            for j, sj in enumerate(s_new):
                pj = jnp.exp2(sj - m_new)
                l = l + pj
                acc = acc + pj * vn_ref[j:j + 1, hs]
            o = acc / l
            o_ref[:, hs] = _subln_gate(o[:n_dec], o[n_dec:], lam, subg_ref[...], ga_ref[:, hs])


def _attn(page_table, q8, k_new, v_new, ga_s, q_p, ktb_p, v_p, ga_p, lq1, lk1, lq2, lk2, subg,
          kt_pages, v_pages, n_seq, seq_len):
    n_d, n_pages = page_table.shape
    n_dec = k_new.shape[1]
    npg = PAGES_PER_STEP
    page = kt_pages.shape[-1]
    tq = ATT_TILE
    nq = seq_len // tq
    assert n_d == n_seq * H_A * nq and n_pages == nq * npg
    dec = lambda b, h, i: (b * H_A + h) * nq + i
    vec = lambda w: pl.BlockSpec((1, w), lambda b, h, i, j, pt: (0, 0))
    per_d = lambda r: pl.BlockSpec((None, r, D_ATT), lambda b, h, i, j, pt: (dec(b, h, i), 0, 0))
    q_tile = pl.BlockSpec((tq, DV_A), lambda b, h, i, j, pt: (b * nq + i, h))

    def k_spec(n):
        return pl.BlockSpec((None, H_A, 2, DK_A, page),
                            lambda b, h, i, j, pt: (pt[dec(b, h, i), j * npg + n], 0, 0, 0, 0))

    def v_spec(n):
        return pl.BlockSpec((None, page * H_A, DV_A),
                            lambda b, h, i, j, pt: (pt[dec(b, h, i), j * npg + n], 0, 0))

    in_specs = ([per_d(2 * n_dec), per_d(n_dec), per_d(n_dec), per_d(n_dec),
                 q_tile,
                 pl.BlockSpec((None, None, DV_A, tq), lambda b, h, i, j, pt: (b, jnp.minimum(j, i), h, 0)),
                 pl.BlockSpec((tq, DV_A), lambda b, h, i, j, pt: (b * nq + jnp.minimum(j, i), h)),
                 q_tile,
                 vec(DK_A), vec(DK_A), vec(DK_A), vec(DK_A), vec(DV_A)]
                + [k_spec(n) for n in range(npg)] + [v_spec(n) for n in range(npg)])
    grid_spec = pltpu.PrefetchScalarGridSpec(
        num_scalar_prefetch=1, grid=(n_seq, H_A, nq, nq), in_specs=in_specs,
        out_specs=(per_d(n_dec), q_tile),
        scratch_shapes=[pltpu.VMEM((H_A * 2 * n_dec, 1), F32), pltpu.VMEM((H_A * 2 * n_dec, 1), F32),
                        pltpu.VMEM((H_A, 2 * n_dec, DV_A), F32),
                        pltpu.VMEM((2 * tq, DV_A), BF16), pltpu.VMEM((2 * tq, DV_A), F32),
                        pltpu.VMEM((2 * tq, 2 * DV_A), F32)])
    return pl.pallas_call(
        _attn_kernel,
        out_shape=(jax.ShapeDtypeStruct((n_d, n_dec, D_ATT), F32),
                   jax.ShapeDtypeStruct((n_seq * seq_len, D_ATT), BF16)),
        grid_spec=grid_spec,
        compiler_params=pltpu.CompilerParams(
            dimension_semantics=("arbitrary", "arbitrary", "arbitrary", "arbitrary"),
            vmem_limit_bytes=VMEM_LIMIT),
        name="attn",
    )(page_table, q8, k_new, v_new, ga_s, q_p, ktb_p, v_p, ga_p, lq1, lk1, lq2, lk2, subg,
      *([kt_pages] * npg), *([v_pages] * npg))


def _rwkv_pre_kernel(seq_len, rw_ref, prev_ref, first_ref, mu_ref, w0_ref, wup_ref, a0_ref, aup_ref,
                     kk_ref, ka_ref, rk_ref, bd_ref,
                     r_out, w_out, k_out, v_out, nkk_out, b_out, bonus_out):
    tm = rw_ref.shape[0]
    rw = rw_ref[...]
    rolled = pltpu.roll(rw, 1, axis=0)
    rowi = lax.broadcasted_iota(jnp.int32, (tm, 1), 0)
    if seq_len >= tm:
        at_start = (pl.program_id(0) % (seq_len // tm)) == 0
        row0 = jnp.where(at_start, first_ref[...], prev_ref[7:8, :])
        prev = jnp.where(rowi == 0, row0, rolled)
    else:
        prev = jnp.where(rowi % seq_len == 0, first_ref[...], rolled)
    u = rw + (prev - rw) * mu_ref[...]
    r = u[:, 0:D_RWKV]
    k = u[:, D_RWKV:2 * D_RWKV]
    v = u[:, 2 * D_RWKV:3 * D_RWKV]
    wd = u[:, 3 * D_RWKV:3 * D_RWKV + R_LORA]
    ad = u[:, 3 * D_RWKV + R_LORA:]
    w_log = -jax.nn.softplus(-(w0_ref[...] + _dot(jnp.tanh(wd).astype(BF16), wup_ref[...]))) - 0.5
    log_decay = -jnp.exp(w_log)
    a = jax.nn.sigmoid(a0_ref[...] + _dot(ad.astype(BF16), aup_ref[...]))
    bd = bd_ref[...]
    kk = k * kk_ref[...]
    kk = kk / jnp.maximum(jnp.sqrt(_segsum(kk * kk, bd)), L2_EPS)
    k2 = k * (1.0 + (a - 1.0) * ka_ref[...])
    r_out[...] = r
    w_out[...] = log_decay
    k_out[...] = k2
    v_out[...] = v
    nkk_out[...] = -kk
    b_out[...] = kk * a
    bonus_out[...] = _segsum(r * k2 * rk_ref[...], bd) * v


def _rwkv_pre(rw, first, seq_len, mu, w0, wup, a0, aup, k_k, k_a, r_k, bd):
    rows = rw.shape[0]
    tm = ROW_TILE
    const = lambda i: (0, 0)
    row_blk = lambda w: pl.BlockSpec((tm, w), lambda i: (i, 0))
    if seq_len >= tm:
        first_spec = pl.BlockSpec((None, 1, SHIFT_W), lambda i: (i * tm // seq_len, 0, 0))
    else:
        first_spec = row_blk(SHIFT_W)
    in_specs = [
        row_blk(SHIFT_W),
        pl.BlockSpec((8, SHIFT_W), lambda i: (jnp.maximum(i * (tm // 8) - 1, 0), 0)),
        first_spec,
        pl.BlockSpec((1, SHIFT_W), const),
        pl.BlockSpec((1, D_RWKV), const), pl.BlockSpec((R_LORA, D_RWKV), const),
        pl.BlockSpec((1, D_RWKV), const), pl.BlockSpec((R_LORA, D_RWKV), const),
        pl.BlockSpec((1, D_RWKV), const), pl.BlockSpec((1, D_RWKV), const), pl.BlockSpec((1, D_RWKV), const),
        pl.BlockSpec((D_RWKV, D_RWKV), const),
    ]
    out = jax.ShapeDtypeStruct((rows, D_RWKV), F32)
    return pl.pallas_call(
        functools.partial(_rwkv_pre_kernel, seq_len),
        out_shape=(out,) * 7, grid=(rows // tm,), in_specs=in_specs, out_specs=(row_blk(D_RWKV),) * 7,
        compiler_params=pltpu.CompilerParams(dimension_semantics=("parallel",), vmem_limit_bytes=VMEM_LIMIT),
        name="rwkv_pre",
    )(rw, rw, first, mu, w0, wup, a0, aup, k_k, k_a, r_k, bd)


def _rwkv_scan_kernel(r_ref, w_ref, k_ref, v_ref, nkk_ref, b_ref, s0_ref, gng_ref, gnb_ref,
                      y_ref, s_ref):
    nb, tc = r_ref.shape[0], r_ref.shape[1]
    n_grp = H_R // SCAN_HEADS
    gw = SCAN_HEADS * HS_R
    c = pl.program_id(1)

    @pl.when(c == 0)
    def _():
        s_ref[...] = s0_ref[...]

    lane = lax.broadcasted_iota(jnp.int32, (HS_R, gw), 1)
    row = lax.broadcasted_iota(jnp.int32, (HS_R, gw), 0)
    head_of_lane = lane // HS_R
    diag_of_head = [lane == row + e * HS_R for e in range(SCAN_HEADS)]
    head_of_lane1 = lax.broadcasted_iota(jnp.int32, (1, gw), 1) // HS_R
    row8 = lax.broadcasted_iota(jnp.int32, (H_R, gw), 0)
    lane8 = lax.broadcasted_iota(jnp.int32, (H_R, gw), 1)

    unroll = min(8, tc)

    def steps(g, carry):
        t0 = pl.multiple_of(g * unroll, unroll)
        rows = pl.ds(t0, unroll)
        for j in range(unroll):
            for b in range(nb):
                y8 = None
                for p in range(n_grp):
                    ls = slice(p * gw, (p + 1) * gw)
                    row = lambda ref: ref[b, rows, ls][j:j + 1]
                    s = s_ref[b, p]
                    nkk = row(nkk_ref)
                    sa = None
                    for e in range(SCAN_HEADS):
                        in_head = head_of_lane1 == e
                        part = jnp.sum(s * jnp.where(in_head, nkk, 0.0), axis=1, keepdims=True)
                        sa = part if sa is None else jnp.where(head_of_lane == e, part, sa)
                    vr = row(v_ref)
                    vc = None
                    for e in range(SCAN_HEADS):
                        part = jnp.sum(jnp.where(diag_of_head[e], vr, 0.0), axis=1, keepdims=True)
                        vc = part if vc is None else jnp.where(head_of_lane == e, part, vc)
                    s = s * jnp.exp(row(w_ref)) + sa * row(b_ref) + vc * row(k_ref)
                    s_ref[b, p] = s
                    sel = row8 == p * SCAN_HEADS + lane8 // HS_R
                    r2 = jnp.where(sel, jnp.broadcast_to(row(r_ref), (H_R, gw)), 0.0).astype(BF16)
                    yp = lax.dot_general(r2, s.astype(BF16), _NT, preferred_element_type=F32)
                    y8 = yp if y8 is None else y8 + yp
                y_ref[b, t0 + j] = y8
        return carry

    lax.fori_loop(0, tc // unroll, steps, 0)

    y = y_ref[...]
    mean = jnp.mean(y, axis=-1, keepdims=True)
    var = jnp.mean(jnp.square(y - mean), axis=-1, keepdims=True)
    y_ref[...] = (y - mean) * lax.rsqrt(var + GN_EPS) * gng_ref[...] + gnb_ref[...]


def _rwkv_scan(r, w, k, v, nkk, b, s0, gn_g, gn_b, n_seq, seq_len):
    nb = SCAN_SEQS
    tc = min(SCAN_CHUNK, seq_len)
    n_grp = H_R // SCAN_HEADS
    gw = SCAN_HEADS * HS_R
    seq = lambda a: a.reshape(n_seq, seq_len, D_RWKV)
    in_blk = pl.BlockSpec((nb, tc, D_RWKV), lambda g, c: (g, c, 0))
    st_blk = pl.BlockSpec((nb, n_grp, HS_R, gw), lambda g, c: (g, 0, 0, 0))
    gn_blk = pl.BlockSpec((1, 1, H_R, HS_R), lambda g, c: (0, 0, 0, 0))
    return pl.pallas_call(
        _rwkv_scan_kernel,
        out_shape=(jax.ShapeDtypeStruct((n_seq, seq_len, H_R, HS_R), F32),
                   jax.ShapeDtypeStruct((n_seq, n_grp, HS_R, gw), F32)),
        grid=(n_seq // nb, seq_len // tc),
        in_specs=[in_blk] * 6 + [st_blk, gn_blk, gn_blk],
        out_specs=(pl.BlockSpec((nb, tc, H_R, HS_R), lambda g, c: (g, c, 0, 0)), st_blk),
        compiler_params=pltpu.CompilerParams(dimension_semantics=("parallel", "arbitrary"),
                                             vmem_limit_bytes=VMEM_LIMIT),
        name="rwkv_scan",
    )(seq(r), seq(w), seq(k), seq(v), seq(nkk), seq(b), s0,
      gn_g.reshape(1, 1, H_R, HS_R), gn_b.reshape(1, 1, H_R, HS_R))


def _rwkv_chunk_kernel(r_ref, lw_ref, k_ref, v_ref, nkk_ref, b_ref, gng_ref, gnb_ref, y_ref, s_ref):
    nb, C = r_ref.shape[0], r_ref.shape[1]
    n_pair = H_R // 2
    pw = 2 * HS_R
    c = pl.program_id(1)

    @pl.when(c == 0)
    def _():
        s_ref[...] = jnp.zeros(s_ref.shape, F32)

    lane = lax.broadcasted_iota(jnp.int32, (C, pw), 1)
    t_idx = lax.broadcasted_iota(jnp.int32, (C, pw), 0)
    head0 = lane < HS_R
    s_idx = lane % C
    strict = t_idx > s_idx
    incl = t_idx >= s_idx
    eye = jnp.where(t_idx == s_idx, 1.0, 0.0).astype(F32)
    tri = jnp.where(lax.broadcasted_iota(jnp.int32, (C, C), 0) >= lax.broadcasted_iota(jnp.int32, (C, C), 1),
                    1.0, 0.0).astype(BF16)
    same_head = (lax.broadcasted_iota(jnp.int32, (pw, pw), 0) // HS_R
                 == lax.broadcasted_iota(jnp.int32, (pw, pw), 1) // HS_R)

    def stack(x):
        return jnp.concatenate([jnp.where(head0, x, 0.0), jnp.where(head0, 0.0, x)], axis=0).astype(BF16)

    def per_head(coef, x):
        return _dot(coef.astype(BF16), stack(x))

    def head_mean(x):
        return jnp.where(head0, jnp.sum(jnp.where(head0, x, 0.0), axis=1, keepdims=True),
                         jnp.sum(jnp.where(head0, 0.0, x), axis=1, keepdims=True)) * (1.0 / HS_R)

    units = [(b, p) for b in range(nb) for p in range(n_pair)]
    ls = lambda p: slice(p * pw, (p + 1) * pw)
    load = lambda ref: [ref[b, :, ls(p)] for b, p in units]
    lw = load(lw_ref)
    lw_hi = [x.astype(BF16) for x in lw]
    cum = [_dot(tri, h) + _dot(tri, (x - h.astype(F32)).astype(BF16)) for x, h in zip(lw, lw_hi)]
    p_in = [jnp.exp(x) for x in cum]
    p_inv = [jnp.exp(-x) for x in cum]
    nt = [x * jnp.exp(cm - l) for x, cm, l in zip(load(nkk_ref), cum, lw)]
    bt = [x * pi for x, pi in zip(load(b_ref), p_inv)]
    kt = [x * pi for x, pi in zip(load(k_ref), p_inv)]
    rt = [x * pp for x, pp in zip(load(r_ref), p_in)]
    vv = load(v_ref)

    lhs = [jnp.concatenate([n_, r_], axis=0).astype(BF16) for n_, r_ in zip(nt, rt)]
    coef = [lax.dot_general(l_, jnp.concatenate([stack(b_), stack(k_)], axis=0), _NT,
                            preferred_element_type=F32) for l_, b_, k_ in zip(lhs, bt, kt)]
    a_pow = [jnp.where(strict, cf[:C, :pw], 0.0) for cf in coef]
    b_m = [jnp.where(strict, cf[:C, pw:], 0.0) for cf in coef]
    ab_m = [jnp.where(incl, cf[C:, :pw], 0.0) for cf in coef]
    ak_m = [jnp.where(incl, cf[C:, pw:], 0.0) for cf in coef]

    inv = [eye + a for a in a_pow]
    for _ in range(C.bit_length() - 2):
        a_pow = [per_head(a, a) for a in a_pow]
        inv = [per_head(i_, eye + a) for i_, a in zip(inv, a_pow)]

    v_stack = [stack(x) for x in vv]
    bv = [_dot(m.astype(BF16), vs) for m, vs in zip(b_m, v_stack)]
    akv = [_dot(m.astype(BF16), vs) for m, vs in zip(ak_m, v_stack)]

    st = [s_ref[b, p] for b, p in units]
    ns_rs = [lax.dot_general(l_, s_.astype(BF16), _NT, preferred_element_type=F32)
             for l_, s_ in zip(lhs, st)]
    u = [per_head(i_, x[:C] + bv_) for i_, x, bv_ in zip(inv, ns_rs, bv)]
    y = [x[C:] + per_head(m, u_) + akv_ for x, m, u_, akv_ in zip(ns_rs, ab_m, u, akv)]

    for i, (b, p) in enumerate(units):
        uv_t = jnp.concatenate([u[i], vv[i]], axis=0).T.astype(BF16)
        upd = _dot(uv_t, jnp.concatenate([bt[i], kt[i]], axis=0).astype(BF16))
        s_ref[b, p] = jnp.where(same_head, (st[i] + upd) * p_in[i][C - 1:C, :], 0.0)

    for i, (b, p) in enumerate(units):
        d = y[i] - head_mean(y[i])
        var = head_mean(d * d)
        y_ref[b, :, ls(p)] = d * lax.rsqrt(var + GN_EPS) * gng_ref[:, ls(p)] + gnb_ref[:, ls(p)]


def _rwkv_chunk(r, lw, k, v, nkk, b, gn_g, gn_b, n_seq, seq_len):
    nb = SCAN_SEQS
    C = CHUNK
    n_pair = H_R // 2
    seq = lambda a: a.reshape(n_seq, seq_len, D_RWKV)
    in_blk = pl.BlockSpec((nb, C, D_RWKV), lambda g, c: (g, c, 0))
    gn_blk = pl.BlockSpec((1, D_RWKV), lambda g, c: (0, 0))
    return pl.pallas_call(
        _rwkv_chunk_kernel,
        out_shape=(jax.ShapeDtypeStruct((n_seq, seq_len, D_RWKV), F32),
                   jax.ShapeDtypeStruct((n_seq, n_pair, 2 * HS_R, 2 * HS_R), F32)),
        grid=(n_seq // nb, seq_len // C),
        in_specs=[in_blk] * 6 + [gn_blk, gn_blk],
        out_specs=(in_blk, pl.BlockSpec((nb, n_pair, 2 * HS_R, 2 * HS_R), lambda g, c: (g, 0, 0, 0))),
        compiler_params=pltpu.CompilerParams(dimension_semantics=("parallel", "arbitrary"),
                                             vmem_limit_bytes=VMEM_LIMIT),
        name="rwkv_chunk",
    )(seq(r), seq(lw), seq(k), seq(v), seq(nkk), seq(b), gn_g[None], gn_b[None])


def _out_proj_kernel(oa_ref, yg_ref, bonus_ref, gr_ref, x_ref, wo_ref, y_ref):
    mix_r = ((yg_ref[...] + bonus_ref[...]) * gr_ref[...]).astype(BF16)
    y_ref[...] = (x_ref[...] + _dot(oa_ref[...].astype(BF16), wo_ref[0:D_ATT, :])
                  + _dot(mix_r, wo_ref[D_ATT:, :]))


def _out_proj(oa, yg, bonus, gr, x, wo):
    rows = x.shape[0]
    tm = ROW_TILE
    row_blk = lambda w: pl.BlockSpec((tm, w), lambda i: (i, 0))
    return pl.pallas_call(
        _out_proj_kernel, out_shape=jax.ShapeDtypeStruct((rows, D_MODEL), F32), grid=(rows // tm,),
        in_specs=[row_blk(D_ATT), row_blk(D_RWKV), row_blk(D_RWKV), row_blk(D_RWKV), row_blk(D_MODEL),
                  pl.BlockSpec(wo.shape, lambda i: (0, 0))],
        out_specs=row_blk(D_MODEL),
        compiler_params=pltpu.CompilerParams(dimension_semantics=("parallel",), vmem_limit_bytes=VMEM_LIMIT),
        name="out_proj",
    )(oa, yg, bonus, gr, x, wo)


def _rope_tables(pos):
    half = DK_A // 2
    inv = 1.0 / (ROPE_THETA ** (jnp.arange(half, dtype=F32) / half))
    ang = inv[:, None] * pos.astype(F32)[None, :]
    return jnp.cos(ang), jnp.sin(ang)


def _pair_state(s):
    n = s.shape[0]
    g = H_R // SCAN_HEADS
    return (s.reshape(n, g, SCAN_HEADS, HS_R, HS_R).transpose(0, 1, 3, 2, 4)
            .reshape(n, g, HS_R, SCAN_HEADS * HS_R))


def _unpair_state(s):
    n = s.shape[0]
    g = H_R // SCAN_HEADS
    return s.reshape(n, g, HS_R, SCAN_HEADS, HS_R).transpose(0, 1, 3, 2, 4).reshape(n, H_R, HS_R, HS_R)


def kernel(x_prompt, x_sample, cache_k, cache_v, state_wkv, state_shift, page_table, ln_g, w_in, q_norm_g,
           k_norm_g, lambda_q1, lambda_k1, lambda_q2, lambda_k2, subln_g, shift_mu, w0, w_lora_up, a0,
           a_lora_up, k_k, k_a, r_k, gn_g, gn_b, w_out):
    n_b, seq = x_prompt.shape[:2]
    n_d, n_dec = x_sample.shape[:2]
    page = cache_k.shape[2]
    past = page_table.shape[1] * page
    layer = 0

    w_in_b = w_in[layer].astype(BF16)
    wq_t = w_in_b[:, 0:D_ATT].T
    wk_t = w_in_b[:, D_ATT:2 * D_ATT].T
    w_rest = w_in_b[:, 2 * D_ATT:]
    wo = w_out[layer].astype(BF16)
    wup = w_lora_up[layer].astype(BF16)
    aup = a_lora_up[layer].astype(BF16)
    qg = q_norm_g[layer].reshape(DK_A, 1)
    kg = k_norm_g[layer].reshape(DK_A, 1)
    seg = jnp.arange(D_RWKV) // HS_R
    bd = (seg[:, None] == seg[None, :]).astype(BF16)
    lam_vecs = (lambda_q1[layer][None], lambda_k1[layer][None], lambda_q2[layer][None], lambda_k2[layer][None])
    subg = subln_g[layer][None]
    row = lambda a: a[layer][None]

    def scan_short(r, lw, k2, vv, nkk, bb, s0, n_seq, seq_len):
        yg, s_fin = _rwkv_scan(r, lw, k2, vv, nkk, bb, _pair_state(s0), gn_g[layer], gn_b[layer],
                               n_seq, seq_len)
        return yg.reshape(n_seq * seq_len, D_RWKV), _unpair_state(s_fin)

    def scan_chunked(r, lw, k2, vv, nkk, bb, s0, n_seq, seq_len):
        yg, s_fin = _rwkv_chunk(r, lw, k2, vv, nkk, bb, gn_g[layer], gn_b[layer], n_seq, seq_len)
        s6 = s_fin.reshape(n_seq, H_R // 2, 2, HS_R, 2, HS_R)
        s_heads = jnp.stack([s6[:, :, e, :, e, :] for e in range(2)], axis=2)
        return yg.reshape(n_seq * seq_len, D_RWKV), s_heads.reshape(n_seq, H_R, HS_R, HS_R)

    def project(x, pos, rows, proj_seqs):
        cos_t, sin_t = _rope_tables(pos)
        return _proj(x.reshape(rows, D_MODEL), ln_g[layer][None], wq_t, wk_t, w_rest, cos_t, sin_t, qg, kg,
                     proj_seqs, rows // proj_seqs)

    def mix_out(x, out_a, rw, gr, n_seq, seq_len, first, s0, scan):
        rows = n_seq * seq_len
        r, lw, k2, vv, nkk, bb, bonus = _rwkv_pre(rw, first, seq_len, row(shift_mu), row(w0), wup, row(a0),
                                                  aup, row(k_k), row(k_a), row(r_k), bd)
        yg, s_fin = scan(r, lw, k2, vv, nkk, bb, s0, n_seq, seq_len)
        return _out_proj(out_a, yg, bonus, gr, x.reshape(rows, D_MODEL), wo), s_fin

    rows_p = n_b * seq
    rows_s = n_d * n_dec
    q_p, kt_p, ktb_p, v_p, vb_p, ga_p, rw_p, gr_p = project(x_prompt, jnp.tile(jnp.arange(seq), n_b), rows_p, n_b)
    q_s, kt_s, _, v_s, _, ga_s, rw_s, gr_s = project(x_sample, jnp.tile(past + jnp.arange(n_dec), n_d), rows_s, 1)

    kt_pages = cache_k[layer].transpose(0, 2, 3, 4, 1)
    v_pages = cache_v[layer].reshape(-1, page * H_A, DV_A)
    q3 = q_s.reshape(n_d, n_dec, D_ATT)
    k_new = kt_s.reshape(D_ATT, rows_s).T.reshape(n_d, n_dec, D_ATT)
    oa_s, oa_p = _attn(page_table, jnp.concatenate([q3, q3], axis=1), k_new, v_s.reshape(n_d, n_dec, D_ATT),
                       ga_s.reshape(n_d, n_dec, D_ATT), q_p, ktb_p, vb_p, ga_p, *lam_vecs, subg,
                       kt_pages, v_pages, n_b, seq)

    yp, s_p = mix_out(x_prompt, oa_p, rw_p, gr_p, n_b, seq, jnp.zeros((n_b, 1, SHIFT_W), F32), None,
                      scan_chunked)
    first_s = jnp.repeat(state_shift[layer], n_dec, axis=0)
    ys, s_s = mix_out(x_sample, oa_s.reshape(rows_s, D_ATT), rw_s, gr_s, n_d, n_dec, first_s,
                      state_wkv[layer], scan_short)

    k_prompt = kt_p.reshape(n_b, H_A, 2, DK_A, seq).transpose(0, 4, 1, 2, 3)
    k_sample = kt_s.reshape(H_A, 2, DK_A, n_d, n_dec).transpose(3, 4, 0, 1, 2)
    return (yp.reshape(n_b, seq, D_MODEL), ys.reshape(n_d, n_dec, D_MODEL),
            k_prompt[None], v_p.reshape(n_b, seq, H_A, DV_A)[None], s_p[None],
            rw_p.reshape(n_b, seq, SHIFT_W)[:, -1][None],
            k_sample[None], v_s.reshape(n_d, n_dec, H_A, DV_A)[None], s_s[None],
            rw_s.reshape(n_d, n_dec, SHIFT_W)[:, -1][None])
```

```python
import functools
import math

import jax
import jax.numpy as jnp
from jax import lax
from jax.experimental import pallas as pl
from jax.experimental.pallas import tpu as pltpu

F32 = jnp.float32
BF16 = jnp.bfloat16

D_MODEL = 1024
D_ATT = 512
D_RWKV = 512
DK_A = 64
H_A = 4
DV_A = 128
HS_R = 64
H_R = 8
R_LORA = 64
SHIFT_W = 3 * D_RWKV + 2 * R_LORA
ROPE_THETA = 10000.0
RMS_EPS = 1e-6
GN_EPS = 64e-5
L2_EPS = 1e-12
LAM_INIT = 0.8 - 0.6 * math.exp(-0.3 * 0)

ROW_TILE = 256
ATT_TILE = 512
KEY_TILE = 1024
LOG2E = math.log2(math.e)
SCAN_SEQS = 4
SCAN_HEADS = 2
SCAN_CHUNK = 128
CHUNK = HS_R
PAGES_PER_STEP = 16
VMEM_LIMIT = 56 * 1024 * 1024

_NT = (((1,), (1,)), ((), ()))


def _dot(a, b):
    return jnp.dot(a, b, preferred_element_type=F32)


def _segsum(x, bd):
    hi = x.astype(BF16)
    lo = (x - hi.astype(F32)).astype(BF16)
    return _dot(hi, bd) + _dot(lo, bd)


def _diff_lambda(lq1, lk1, lq2, lk2):
    return (jnp.exp(jnp.sum(lq1 * lk1, axis=1, keepdims=True))
            - jnp.exp(jnp.sum(lq2 * lk2, axis=1, keepdims=True)) + LAM_INIT)


def _subln_gate(o0, o1, lam, subg, gate):
    y = o0 - lam * o1
    y = y * lax.rsqrt(jnp.mean(y * y, axis=-1, keepdims=True) + RMS_EPS) * subg
    return (y * (1.0 - LAM_INIT)) * gate


def _proj_kernel(x_ref, lng_ref, wq_ref, wk_ref, wr_ref, cos_ref, sin_ref, qg_ref, kg_ref,
                 q_ref, kt_ref, ktb_ref, v_ref, vb_ref, ga_ref, rw_ref, gr_ref):
    tm = x_ref.shape[0]
    x = x_ref[...]
    h = x * lax.rsqrt(jnp.mean(x * x, axis=-1, keepdims=True) + RMS_EPS) * lng_ref[...]
    h = h.astype(BF16)
    cos = cos_ref[...][None]
    sin = sin_ref[...][None]

    def norm_rope(t, g):
        t3 = t.reshape(2 * H_A, DK_A, tm)
        t3 = t3 * lax.rsqrt(jnp.mean(t3 * t3, axis=1, keepdims=True) + RMS_EPS) * g[None]
        x1 = t3[:, :DK_A // 2]
        x2 = t3[:, DK_A // 2:]
        o = jnp.concatenate([x1 * cos - x2 * sin, x2 * cos + x1 * sin], axis=1)
        return o.reshape(D_ATT, tm)

    qt = lax.dot_general(wq_ref[...], h, _NT, preferred_element_type=F32)
    qt = norm_rope(qt, qg_ref[...]) * (DK_A ** -0.5 * LOG2E)
    q_ref[...] = qt.T.astype(BF16)
    kt = lax.dot_general(wk_ref[...], h, _NT, preferred_element_type=F32)
    kt = norm_rope(kt, kg_ref[...])
    kt_ref[...] = kt
    ktb_ref[...] = kt.astype(BF16)

    v = _dot(h, wr_ref[:, 0:D_ATT])
    v_ref[...] = v
    vb_ref[...] = v.astype(BF16)
    za = _dot(h, wr_ref[:, D_ATT:2 * D_ATT])
    ga_ref[...] = za * jax.nn.sigmoid(za)
    rw_ref[...] = _dot(h, wr_ref[:, 2 * D_ATT:2 * D_ATT + SHIFT_W])
    zr = _dot(h, wr_ref[:, 2 * D_ATT + SHIFT_W:])
    gr_ref[...] = zr * jax.nn.sigmoid(zr)


def _proj(x, ln_g, wq_t, wk_t, w_rest, cos_t, sin_t, qg, kg, n_seq, seq_len):
    rows = x.shape[0]
    tm = ROW_TILE
    nsi = seq_len // tm
    tk = min(KEY_TILE, seq_len)
    per_tk = tk // tm
    const = lambda i: (0, 0)
    row_blk = lambda w: pl.BlockSpec((tm, w), lambda i: (i, 0))
    out_shape = (
        jax.ShapeDtypeStruct((rows, D_ATT), BF16),
        jax.ShapeDtypeStruct((n_seq, D_ATT, seq_len), F32),
        jax.ShapeDtypeStruct((n_seq, seq_len // tk, D_ATT, tk), BF16),
        jax.ShapeDtypeStruct((rows, D_ATT), F32),
        jax.ShapeDtypeStruct((rows, D_ATT), BF16),
        jax.ShapeDtypeStruct((rows, D_ATT), F32),
        jax.ShapeDtypeStruct((rows, SHIFT_W), F32),
        jax.ShapeDtypeStruct((rows, D_RWKV), F32),
    )
    out_specs = (
        row_blk(D_ATT),
        pl.BlockSpec((None, D_ATT, tm), lambda i: (i // nsi, 0, i % nsi)),
        pl.BlockSpec((None, None, D_ATT, tm), lambda i: (i // nsi, (i % nsi) // per_tk, 0, i % per_tk)),
        row_blk(D_ATT), row_blk(D_ATT), row_blk(D_ATT), row_blk(SHIFT_W), row_blk(D_RWKV),
    )
    in_specs = [
        row_blk(D_MODEL),
        pl.BlockSpec((1, D_MODEL), const),
        pl.BlockSpec(wq_t.shape, const),
        pl.BlockSpec(wk_t.shape, const),
        pl.BlockSpec(w_rest.shape, const),
        pl.BlockSpec((DK_A // 2, tm), lambda i: (0, i)),
        pl.BlockSpec((DK_A // 2, tm), lambda i: (0, i)),
        pl.BlockSpec((DK_A, 1), const),
        pl.BlockSpec((DK_A, 1), const),
    ]
    return pl.pallas_call(
        _proj_kernel, out_shape=out_shape, grid=(rows // tm,), in_specs=in_specs, out_specs=out_specs,
        compiler_params=pltpu.CompilerParams(dimension_semantics=("parallel",), vmem_limit_bytes=VMEM_LIMIT),
        name="proj",
    )(x, ln_g, wq_t, wk_t, w_rest, cos_t, sin_t, qg, kg)


def _attn_kernel(pt_ref, q8_ref, kn_ref, vn_ref, gas_ref, qp_ref, ktp_ref, vp_ref, gap_ref,
                 lq1_ref, lk1_ref, lq2_ref, lk2_ref, subg_ref, *refs):
    npg = PAGES_PER_STEP
    k_refs = refs[:npg]
    v_refs = refs[npg:2 * npg]
    os_ref, op_ref, m_sc, l_sc, acc_sc, qs_sc, m_p, acc_p = refs[2 * npg:]
    i = pl.program_id(2)
    j = pl.program_id(3)
    g = j
    n_dec = q8_ref.shape[0] // 2
    tq = qp_ref.shape[0]
    tk = ktp_ref.shape[1]

    @pl.when(j == 0)
    def _():
        m_sc[...] = jnp.full(m_sc.shape, -jnp.inf, F32)
        l_sc[...] = jnp.zeros(l_sc.shape, F32)
        acc_sc[...] = jnp.zeros(acc_sc.shape, F32)
        q = qp_ref[...]
        lane_q = lax.broadcasted_iota(jnp.int32, q.shape, 1)
        zero = jnp.zeros_like(q)
        qs_sc[...] = jnp.concatenate([jnp.where(lane_q < DK_A, q, zero), jnp.where(lane_q >= DK_A, q, zero)],
                                     axis=0)
        m_p[...] = jnp.full(m_p.shape, -jnp.inf, F32)
        acc_p[...] = jnp.zeros(acc_p.shape, F32)

    per_key = tk // tq
    jd = i // per_key

    def prompt_tile(diagonal):
        sp = _dot(qs_sc[...], ktp_ref[...])
        if diagonal:
            row_p = lax.broadcasted_iota(jnp.int32, sp.shape, 0)
            col_p = lax.broadcasted_iota(jnp.int32, sp.shape, 1)
            q_off = (i % per_key) * tq
            sp = jnp.where(col_p <= jnp.where(row_p >= tq, row_p - tq, row_p) + q_off, sp, -jnp.inf)
        mp_prev = m_p[...]
        mp_new = jnp.maximum(mp_prev, jnp.max(sp, axis=1, keepdims=True))
        alpha_p = jnp.exp2(mp_prev - mp_new)
        pp = jnp.exp2(sp - jnp.tile(mp_new, (1, tk // DV_A)))
        v_ext = jnp.concatenate([vp_ref[...], jnp.ones((tk, DV_A), BF16)], axis=1)
        acc_p[...] = jnp.tile(alpha_p, (1, 2)) * acc_p[...] + _dot(pp.astype(BF16), v_ext)
        m_p[...] = mp_new

    o_ref, ga_ref = os_ref, gas_ref
    rowi = lax.broadcasted_iota(jnp.int32, (2 * n_dec, DV_A), 0)
    lane = lax.broadcasted_iota(jnp.int32, (2 * n_dec, DV_A), 1)
    sel = ((rowi < n_dec) & (lane < DK_A)) | ((rowi >= n_dec) & (lane >= DK_A))
    rows_h = 2 * n_dec

    def head_q(h):
        qh = q8_ref[:, h * DV_A:(h + 1) * DV_A]
        return jnp.where(sel, qh, jnp.zeros_like(qh))

    def decode_pages():
        page = k_refs[0].shape[-1]
        s = jnp.concatenate(
            [jnp.concatenate([_dot(head_q(h), k_refs[n][h].astype(BF16).reshape(2 * DK_A, -1))
                              for n in range(npg)], axis=1) for h in range(H_A)], axis=0)
        m_prev = m_sc[...]
        m_new = jnp.maximum(m_prev, jnp.max(s, axis=1, keepdims=True))
        alpha = jnp.exp2(m_prev - m_new)
        p = jnp.exp2(s - m_new)
        l_sc[...] = alpha * l_sc[...] + jnp.sum(p, axis=1, keepdims=True)
        m_sc[...] = m_new
        pb = p.astype(BF16)
        for h in range(H_A):
            acc = alpha[h * rows_h:(h + 1) * rows_h] * acc_sc[h]
            for n in range(npg):
                vh = v_refs[n][pl.ds(h, page, stride=H_A), :]
                acc = acc + _dot(pb[h * rows_h:(h + 1) * rows_h, n * page:(n + 1) * page], vh.astype(BF16))
            acc_sc[h] = acc

    @pl.when(j < jd)
    def _():
        prompt_tile(False)
        decode_pages()

    @pl.when(j == jd)
    def _():
        prompt_tile(True)
        decode_pages()
        acc = acc_p[...]
        o = acc[:, :DV_A] / acc[:, DV_A:]
        lam = _diff_lambda(lq1_ref[...], lk1_ref[...], lq2_ref[...], lk2_ref[...])
        op_ref[...] = _subln_gate(o[:tq], o[tq:], lam, subg_ref[...], gap_ref[...]).astype(BF16)

    @pl.when(j > jd)
    def _():
        decode_pages()

    @pl.when(g == pl.num_programs(3) - 1)
    def _():
        lam = _diff_lambda(lq1_ref[...], lk1_ref[...], lq2_ref[...], lk2_ref[...])
        t_idx = jnp.where(rowi[:, :1] >= n_dec, rowi[:, :1] - n_dec, rowi[:, :1])
        for h in range(H_A):
            hs = slice(h * DV_A, (h + 1) * DV_A)
            qf = head_q(h).astype(F32)
            s_new = []
            for j in range(n_dec):
                sj = jnp.sum(qf * kn_ref[j:j + 1, hs], axis=1, keepdims=True)
                s_new.append(jnp.where(j <= t_idx, sj, -jnp.inf))
            m_prev = m_sc[h * rows_h:(h + 1) * rows_h]
            m_new = m_prev
            for sj in s_new:
                m_new = jnp.maximum(m_new, sj)
            alpha = jnp.exp2(m_prev - m_new)
            l = alpha * l_sc[h * rows_h:(h + 1) * rows_h]
            acc = alpha * acc_sc[h]
            for j, sj in enumerate(s_new):
                pj = jnp.exp2(sj - m_new)
                l = l + pj
                acc = acc + pj * vn_ref[j:j + 1, hs]
            o = acc / l
            o_ref[:, hs] = _subln_gate(o[:n_dec], o[n_dec:], lam, subg_ref[...], ga_ref[:, hs])


def _attn(page_table, q8, k_new, v_new, ga_s, q_p, ktb_p, v_p, ga_p, lq1, lk1, lq2, lk2, subg,
          kt_pages, v_pages, n_seq, seq_len):
    n_d, n_pages = page_table.shape
    n_dec = k_new.shape[1]
    npg = PAGES_PER_STEP
    page = kt_pages.shape[-1]
    tq = ATT_TILE
    tk = KEY_TILE
    nq = seq_len // tq
    nk = seq_len // tk
    per_key = tk // tq
    assert n_d == n_seq * H_A * nq and n_pages == nk * npg
    dec = lambda b, h, i: (b * H_A + h) * nq + i
    key_tile = lambda i, j: jnp.minimum(j, i // per_key)
    vec = lambda w: pl.BlockSpec((1, w), lambda b, h, i, j, pt: (0, 0))
    per_d = lambda r: pl.BlockSpec((None, r, D_ATT), lambda b, h, i, j, pt: (dec(b, h, i), 0, 0))
    q_tile = pl.BlockSpec((tq, DV_A), lambda b, h, i, j, pt: (b * nq + i, h))

    def k_spec(n):
        return pl.BlockSpec((None, H_A, 2, DK_A, page),
                            lambda b, h, i, j, pt: (pt[dec(b, h, i), j * npg + n], 0, 0, 0, 0))

    def v_spec(n):
        return pl.BlockSpec((None, page * H_A, DV_A),
                            lambda b, h, i, j, pt: (pt[dec(b, h, i), j * npg + n], 0, 0))

    in_specs = ([per_d(2 * n_dec), per_d(n_dec), per_d(n_dec), per_d(n_dec),
                 q_tile,
                 pl.BlockSpec((None, None, DV_A, tk), lambda b, h, i, j, pt: (b, key_tile(i, j), h, 0)),
                 pl.BlockSpec((tk, DV_A), lambda b, h, i, j, pt: (b * nk + key_tile(i, j), h)),
                 q_tile,
                 vec(DK_A), vec(DK_A), vec(DK_A), vec(DK_A), vec(DV_A)]
                + [k_spec(n) for n in range(npg)] + [v_spec(n) for n in range(npg)])
    grid_spec = pltpu.PrefetchScalarGridSpec(
        num_scalar_prefetch=1, grid=(n_seq, H_A, nq, nk), in_specs=in_specs,
        out_specs=(per_d(n_dec), q_tile),
        scratch_shapes=[pltpu.VMEM((H_A * 2 * n_dec, 1), F32), pltpu.VMEM((H_A * 2 * n_dec, 1), F32),
                        pltpu.VMEM((H_A, 2 * n_dec, DV_A), F32),
                        pltpu.VMEM((2 * tq, DV_A), BF16), pltpu.VMEM((2 * tq, DV_A), F32),
                        pltpu.VMEM((2 * tq, 2 * DV_A), F32)])
    return pl.pallas_call(
        _attn_kernel,
        out_shape=(jax.ShapeDtypeStruct((n_d, n_dec, D_ATT), F32),
                   jax.ShapeDtypeStruct((n_seq * seq_len, D_ATT), BF16)),
        grid_spec=grid_spec,
        compiler_params=pltpu.CompilerParams(
            dimension_semantics=("arbitrary", "arbitrary", "arbitrary", "arbitrary"),
            vmem_limit_bytes=VMEM_LIMIT),
        name="attn",
    )(page_table, q8, k_new, v_new, ga_s, q_p, ktb_p, v_p, ga_p, lq1, lk1, lq2, lk2, subg,
      *([kt_pages] * npg), *([v_pages] * npg))


def _rwkv_pre_kernel(seq_len, rw_ref, prev_ref, first_ref, mu_ref, w0_ref, wup_ref, a0_ref, aup_ref,
                     kk_ref, ka_ref, rk_ref, bd_ref,
                     r_out, w_out, k_out, v_out, nkk_out, b_out, bonus_out):
    tm = rw_ref.shape[0]
    rw = rw_ref[...]
    rolled = pltpu.roll(rw, 1, axis=0)
    rowi = lax.broadcasted_iota(jnp.int32, (tm, 1), 0)
    if seq_len >= tm:
        at_start = (pl.program_id(0) % (seq_len // tm)) == 0
        row0 = jnp.where(at_start, first_ref[...], prev_ref[7:8, :])
        prev = jnp.where(rowi == 0, row0, rolled)
    else:
        prev = jnp.where(rowi % seq_len == 0, first_ref[...], rolled)
    u = rw + (prev - rw) * mu_ref[...]
    r = u[:, 0:D_RWKV]
    k = u[:, D_RWKV:2 * D_RWKV]
    v = u[:, 2 * D_RWKV:3 * D_RWKV]
    wd = u[:, 3 * D_RWKV:3 * D_RWKV + R_LORA]
    ad = u[:, 3 * D_RWKV + R_LORA:]
    w_log = -jax.nn.softplus(-(w0_ref[...] + _dot(jnp.tanh(wd).astype(BF16), wup_ref[...]))) - 0.5
    log_decay = -jnp.exp(w_log)
    a = jax.nn.sigmoid(a0_ref[...] + _dot(ad.astype(BF16), aup_ref[...]))
    bd = bd_ref[...]
    kk = k * kk_ref[...]
    kk = kk / jnp.maximum(jnp.sqrt(_segsum(kk * kk, bd)), L2_EPS)
    k2 = k * (1.0 + (a - 1.0) * ka_ref[...])
    r_out[...] = r
    w_out[...] = log_decay
    k_out[...] = k2
    v_out[...] = v
    nkk_out[...] = -kk
    b_out[...] = kk * a
    bonus_out[...] = _segsum(r * k2 * rk_ref[...], bd) * v


def _rwkv_pre(rw, first, seq_len, mu, w0, wup, a0, aup, k_k, k_a, r_k, bd):
    rows = rw.shape[0]
    tm = ROW_TILE
    const = lambda i: (0, 0)
    row_blk = lambda w: pl.BlockSpec((tm, w), lambda i: (i, 0))
    if seq_len >= tm:
        first_spec = pl.BlockSpec((None, 1, SHIFT_W), lambda i: (i * tm // seq_len, 0, 0))
    else:
        first_spec = row_blk(SHIFT_W)
    in_specs = [
        row_blk(SHIFT_W),
        pl.BlockSpec((8, SHIFT_W), lambda i: (jnp.maximum(i * (tm // 8) - 1, 0), 0)),
        first_spec,
        pl.BlockSpec((1, SHIFT_W), const),
        pl.BlockSpec((1, D_RWKV), const), pl.BlockSpec((R_LORA, D_RWKV), const),
        pl.BlockSpec((1, D_RWKV), const), pl.BlockSpec((R_LORA, D_RWKV), const),
        pl.BlockSpec((1, D_RWKV), const), pl.BlockSpec((1, D_RWKV), const), pl.BlockSpec((1, D_RWKV), const),
        pl.BlockSpec((D_RWKV, D_RWKV), const),
    ]
    out = jax.ShapeDtypeStruct((rows, D_RWKV), F32)
    return pl.pallas_call(
        functools.partial(_rwkv_pre_kernel, seq_len),
        out_shape=(out,) * 7, grid=(rows // tm,), in_specs=in_specs, out_specs=(row_blk(D_RWKV),) * 7,
        compiler_params=pltpu.CompilerParams(dimension_semantics=("parallel",), vmem_limit_bytes=VMEM_LIMIT),
        name="rwkv_pre",
    )(rw, rw, first, mu, w0, wup, a0, aup, k_k, k_a, r_k, bd)


def _rwkv_scan_kernel(r_ref, w_ref, k_ref, v_ref, nkk_ref, b_ref, s0_ref, gng_ref, gnb_ref,
                      y_ref, s_ref):
    nb, tc = r_ref.shape[0], r_ref.shape[1]
    n_grp = H_R // SCAN_HEADS
    gw = SCAN_HEADS * HS_R
    c = pl.program_id(1)

    @pl.when(c == 0)
    def _():
        s_ref[...] = s0_ref[...]

    lane = lax.broadcasted_iota(jnp.int32, (HS_R, gw), 1)
    row = lax.broadcasted_iota(jnp.int32, (HS_R, gw), 0)
    head_of_lane = lane // HS_R
    diag_of_head = [lane == row + e * HS_R for e in range(SCAN_HEADS)]
    head_of_lane1 = lax.broadcasted_iota(jnp.int32, (1, gw), 1) // HS_R
    row8 = lax.broadcasted_iota(jnp.int32, (H_R, gw), 0)
    lane8 = lax.broadcasted_iota(jnp.int32, (H_R, gw), 1)

    unroll = min(8, tc)

    def steps(g, carry):
        t0 = pl.multiple_of(g * unroll, unroll)
        rows = pl.ds(t0, unroll)
        for j in range(unroll):
            for b in range(nb):
                y8 = None
                for p in range(n_grp):
                    ls = slice(p * gw, (p + 1) * gw)
                    row = lambda ref: ref[b, rows, ls][j:j + 1]
                    s = s_ref[b, p]
                    nkk = row(nkk_ref)
                    sa = None
                    for e in range(SCAN_HEADS):
                        in_head = head_of_lane1 == e
                        part = jnp.sum(s * jnp.where(in_head, nkk, 0.0), axis=1, keepdims=True)
                        sa = part if sa is None else jnp.where(head_of_lane == e, part, sa)
                    vr = row(v_ref)
                    vc = None
                    for e in range(SCAN_HEADS):
                        part = jnp.sum(jnp.where(diag_of_head[e], vr, 0.0), axis=1, keepdims=True)
                        vc = part if vc is None else jnp.where(head_of_lane == e, part, vc)
                    s = s * jnp.exp(row(w_ref)) + sa * row(b_ref) + vc * row(k_ref)
                    s_ref[b, p] = s
                    sel = row8 == p * SCAN_HEADS + lane8 // HS_R
                    r2 = jnp.where(sel, jnp.broadcast_to(row(r_ref), (H_R, gw)), 0.0).astype(BF16)
                    yp = lax.dot_general(r2, s.astype(BF16), _NT, preferred_element_type=F32)
                    y8 = yp if y8 is None else y8 + yp
                y_ref[b, t0 + j] = y8
        return carry

    lax.fori_loop(0, tc // unroll, steps, 0)

    y = y_ref[...]
    mean = jnp.mean(y, axis=-1, keepdims=True)
    var = jnp.mean(jnp.square(y - mean), axis=-1, keepdims=True)
    y_ref[...] = (y - mean) * lax.rsqrt(var + GN_EPS) * gng_ref[...] + gnb_ref[...]


def _rwkv_scan(r, w, k, v, nkk, b, s0, gn_g, gn_b, n_seq, seq_len):
    nb = SCAN_SEQS
    tc = min(SCAN_CHUNK, seq_len)
    n_grp = H_R // SCAN_HEADS
    gw = SCAN_HEADS * HS_R
    seq = lambda a: a.reshape(n_seq, seq_len, D_RWKV)
    in_blk = pl.BlockSpec((nb, tc, D_RWKV), lambda g, c: (g, c, 0))
    st_blk = pl.BlockSpec((nb, n_grp, HS_R, gw), lambda g, c: (g, 0, 0, 0))
    gn_blk = pl.BlockSpec((1, 1, H_R, HS_R), lambda g, c: (0, 0, 0, 0))
    return pl.pallas_call(
        _rwkv_scan_kernel,
        out_shape=(jax.ShapeDtypeStruct((n_seq, seq_len, H_R, HS_R), F32),
                   jax.ShapeDtypeStruct((n_seq, n_grp, HS_R, gw), F32)),
        grid=(n_seq // nb, seq_len // tc),
        in_specs=[in_blk] * 6 + [st_blk, gn_blk, gn_blk],
        out_specs=(pl.BlockSpec((nb, tc, H_R, HS_R), lambda g, c: (g, c, 0, 0)), st_blk),
        compiler_params=pltpu.CompilerParams(dimension_semantics=("parallel", "arbitrary"),
                                             vmem_limit_bytes=VMEM_LIMIT),
        name="rwkv_scan",
    )(seq(r), seq(w), seq(k), seq(v), seq(nkk), seq(b), s0,
      gn_g.reshape(1, 1, H_R, HS_R), gn_b.reshape(1, 1, H_R, HS_R))


def _rwkv_chunk_kernel(r_ref, lw_ref, k_ref, v_ref, nkk_ref, b_ref, gng_ref, gnb_ref, y_ref, s_ref):
    nb, C = r_ref.shape[0], r_ref.shape[1]
    n_pair = H_R // 2
    pw = 2 * HS_R
    c = pl.program_id(1)

    @pl.when(c == 0)
    def _():
        s_ref[...] = jnp.zeros(s_ref.shape, F32)

    lane = lax.broadcasted_iota(jnp.int32, (C, pw), 1)
    t_idx = lax.broadcasted_iota(jnp.int32, (C, pw), 0)
    head0 = lane < HS_R
    s_idx = lane % C
    strict = t_idx > s_idx
    incl = t_idx >= s_idx
    eye = jnp.where(t_idx == s_idx, 1.0, 0.0).astype(F32)
    tri = jnp.where(lax.broadcasted_iota(jnp.int32, (C, C), 0) >= lax.broadcasted_iota(jnp.int32, (C, C), 1),
                    1.0, 0.0).astype(BF16)
    same_head = (lax.broadcasted_iota(jnp.int32, (pw, pw), 0) // HS_R
                 == lax.broadcasted_iota(jnp.int32, (pw, pw), 1) // HS_R)

    def stack(x):
        return jnp.concatenate([jnp.where(head0, x, 0.0), jnp.where(head0, 0.0, x)], axis=0).astype(BF16)

    def per_head(coef, x):
        return _dot(coef.astype(BF16), stack(x))

    def head_mean(x):
        return jnp.where(head0, jnp.sum(jnp.where(head0, x, 0.0), axis=1, keepdims=True),
                         jnp.sum(jnp.where(head0, 0.0, x), axis=1, keepdims=True)) * (1.0 / HS_R)

    units = [(b, p) for b in range(nb) for p in range(n_pair)]
    ls = lambda p: slice(p * pw, (p + 1) * pw)
    load = lambda ref: [ref[b, :, ls(p)] for b, p in units]
    lw = load(lw_ref)
    lw_hi = [x.astype(BF16) for x in lw]
    cum = [_dot(tri, h) + _dot(tri, (x - h.astype(F32)).astype(BF16)) for x, h in zip(lw, lw_hi)]
    p_in = [jnp.exp(x) for x in cum]
    p_inv = [jnp.exp(-x) for x in cum]
    nt = [x * jnp.exp(cm - l) for x, cm, l in zip(load(nkk_ref), cum, lw)]
    bt = [x * pi for x, pi in zip(load(b_ref), p_inv)]
    kt = [x * pi for x, pi in zip(load(k_ref), p_inv)]
    rt = [x * pp for x, pp in zip(load(r_ref), p_in)]
    vv = load(v_ref)

    lhs = [jnp.concatenate([n_, r_], axis=0).astype(BF16) for n_, r_ in zip(nt, rt)]
    coef = [lax.dot_general(l_, jnp.concatenate([stack(b_), stack(k_)], axis=0), _NT,
                            preferred_element_type=F32) for l_, b_, k_ in zip(lhs, bt, kt)]
    a_pow = [jnp.where(strict, cf[:C, :pw], 0.0) for cf in coef]
    b_m = [jnp.where(strict, cf[:C, pw:], 0.0) for cf in coef]
    ab_m = [jnp.where(incl, cf[C:, :pw], 0.0) for cf in coef]
    ak_m = [jnp.where(incl, cf[C:, pw:], 0.0) for cf in coef]

    inv = [eye + a for a in a_pow]
    for _ in range(C.bit_length() - 2):
        a_pow = [per_head(a, a) for a in a_pow]
        inv = [per_head(i_, eye + a) for i_, a in zip(inv, a_pow)]

    v_stack = [stack(x) for x in vv]
    bv = [_dot(m.astype(BF16), vs) for m, vs in zip(b_m, v_stack)]
    akv = [_dot(m.astype(BF16), vs) for m, vs in zip(ak_m, v_stack)]

    st = [s_ref[b, p] for b, p in units]
    ns_rs = [lax.dot_general(l_, s_.astype(BF16), _NT, preferred_element_type=F32)
             for l_, s_ in zip(lhs, st)]
    u = [per_head(i_, x[:C] + bv_) for i_, x, bv_ in zip(inv, ns_rs, bv)]
    y = [x[C:] + per_head(m, u_) + akv_ for x, m, u_, akv_ in zip(ns_rs, ab_m, u, akv)]

    for i, (b, p) in enumerate(units):
        uv_t = jnp.concatenate([u[i], vv[i]], axis=0).T.astype(BF16)
        upd = _dot(uv_t, jnp.concatenate([bt[i], kt[i]], axis=0).astype(BF16))
        s_ref[b, p] = jnp.where(same_head, (st[i] + upd) * p_in[i][C - 1:C, :], 0.0)

    for i, (b, p) in enumerate(units):
        d = y[i] - head_mean(y[i])
        var = head_mean(d * d)
        y_ref[b, :, ls(p)] = d * lax.rsqrt(var + GN_EPS) * gng_ref[:, ls(p)] + gnb_ref[:, ls(p)]


def _rwkv_chunk(r, lw, k, v, nkk, b, gn_g, gn_b, n_seq, seq_len):
    nb = SCAN_SEQS
    C = CHUNK
    n_pair = H_R // 2
    seq = lambda a: a.reshape(n_seq, seq_len, D_RWKV)
    in_blk = pl.BlockSpec((nb, C, D_RWKV), lambda g, c: (g, c, 0))
    gn_blk = pl.BlockSpec((1, D_RWKV), lambda g, c: (0, 0))
    return pl.pallas_call(
        _rwkv_chunk_kernel,
        out_shape=(jax.ShapeDtypeStruct((n_seq, seq_len, D_RWKV), F32),
                   jax.ShapeDtypeStruct((n_seq, n_pair, 2 * HS_R, 2 * HS_R), F32)),
        grid=(n_seq // nb, seq_len // C),
        in_specs=[in_blk] * 6 + [gn_blk, gn_blk],
        out_specs=(in_blk, pl.BlockSpec((nb, n_pair, 2 * HS_R, 2 * HS_R), lambda g, c: (g, 0, 0, 0))),
        compiler_params=pltpu.CompilerParams(dimension_semantics=("parallel", "arbitrary"),
                                             vmem_limit_bytes=VMEM_LIMIT),
        name="rwkv_chunk",
    )(seq(r), seq(lw), seq(k), seq(v), seq(nkk), seq(b), gn_g[None], gn_b[None])


def _out_proj_kernel(oa_ref, yg_ref, bonus_ref, gr_ref, x_ref, wo_ref, y_ref):
    mix_r = ((yg_ref[...] + bonus_ref[...]) * gr_ref[...]).astype(BF16)
    y_ref[...] = (x_ref[...] + _dot(oa_ref[...].astype(BF16), wo_ref[0:D_ATT, :])
                  + _dot(mix_r, wo_ref[D_ATT:, :]))


def _out_proj(oa, yg, bonus, gr, x, wo):
    rows = x.shape[0]
    tm = ROW_TILE
    row_blk = lambda w: pl.BlockSpec((tm, w), lambda i: (i, 0))
    return pl.pallas_call(
        _out_proj_kernel, out_shape=jax.ShapeDtypeStruct((rows, D_MODEL), F32), grid=(rows // tm,),
        in_specs=[row_blk(D_ATT), row_blk(D_RWKV), row_blk(D_RWKV), row_blk(D_RWKV), row_blk(D_MODEL),
                  pl.BlockSpec(wo.shape, lambda i: (0, 0))],
        out_specs=row_blk(D_MODEL),
        compiler_params=pltpu.CompilerParams(dimension_semantics=("parallel",), vmem_limit_bytes=VMEM_LIMIT),
        name="out_proj",
    )(oa, yg, bonus, gr, x, wo)


def _rope_tables(pos):
    half = DK_A // 2
    inv = 1.0 / (ROPE_THETA ** (jnp.arange(half, dtype=F32) / half))
    ang = inv[:, None] * pos.astype(F32)[None, :]
    return jnp.cos(ang), jnp.sin(ang)


def _pair_state(s):
    n = s.shape[0]
    g = H_R // SCAN_HEADS
    return (s.reshape(n, g, SCAN_HEADS, HS_R, HS_R).transpose(0, 1, 3, 2, 4)
            .reshape(n, g, HS_R, SCAN_HEADS * HS_R))


def _unpair_state(s):
    n = s.shape[0]
    g = H_R // SCAN_HEADS
    return s.reshape(n, g, HS_R, SCAN_HEADS, HS_R).transpose(0, 1, 3, 2, 4).reshape(n, H_R, HS_R, HS_R)


def kernel(x_prompt, x_sample, cache_k, cache_v, state_wkv, state_shift, page_table, ln_g, w_in, q_norm_g,
           k_norm_g, lambda_q1, lambda_k1, lambda_q2, lambda_k2, subln_g, shift_mu, w0, w_lora_up, a0,
           a_lora_up, k_k, k_a, r_k, gn_g, gn_b, w_out):
    n_b, seq = x_prompt.shape[:2]
    n_d, n_dec = x_sample.shape[:2]
    page = cache_k.shape[2]
    past = page_table.shape[1] * page
    layer = 0

    w_in_b = w_in[layer].astype(BF16)
    wq_t = w_in_b[:, 0:D_ATT].T
    wk_t = w_in_b[:, D_ATT:2 * D_ATT].T
    w_rest = w_in_b[:, 2 * D_ATT:]
    wo = w_out[layer].astype(BF16)
    wup = w_lora_up[layer].astype(BF16)
    aup = a_lora_up[layer].astype(BF16)
    qg = q_norm_g[layer].reshape(DK_A, 1)
    kg = k_norm_g[layer].reshape(DK_A, 1)
    seg = jnp.arange(D_RWKV) // HS_R
    bd = (seg[:, None] == seg[None, :]).astype(BF16)
    lam_vecs = (lambda_q1[layer][None], lambda_k1[layer][None], lambda_q2[layer][None], lambda_k2[layer][None])
    subg = subln_g[layer][None]
    row = lambda a: a[layer][None]

    def scan_short(r, lw, k2, vv, nkk, bb, s0, n_seq, seq_len):
        yg, s_fin = _rwkv_scan(r, lw, k2, vv, nkk, bb, _pair_state(s0), gn_g[layer], gn_b[layer],
                               n_seq, seq_len)
        return yg.reshape(n_seq * seq_len, D_RWKV), _unpair_state(s_fin)

    def scan_chunked(r, lw, k2, vv, nkk, bb, s0, n_seq, seq_len):
        yg, s_fin = _rwkv_chunk(r, lw, k2, vv, nkk, bb, gn_g[layer], gn_b[layer], n_seq, seq_len)
        s6 = s_fin.reshape(n_seq, H_R // 2, 2, HS_R, 2, HS_R)
        s_heads = jnp.stack([s6[:, :, e, :, e, :] for e in range(2)], axis=2)
        return yg.reshape(n_seq * seq_len, D_RWKV), s_heads.reshape(n_seq, H_R, HS_R, HS_R)

    def project(x, pos, rows, proj_seqs):
        cos_t, sin_t = _rope_tables(pos)
        return _proj(x.reshape(rows, D_MODEL), ln_g[layer][None], wq_t, wk_t, w_rest, cos_t, sin_t, qg, kg,
                     proj_seqs, rows // proj_seqs)

    def mix_out(x, out_a, rw, gr, n_seq, seq_len, first, s0, scan):
        rows = n_seq * seq_len
        r, lw, k2, vv, nkk, bb, bonus = _rwkv_pre(rw, first, seq_len, row(shift_mu), row(w0), wup, row(a0),
                                                  aup, row(k_k), row(k_a), row(r_k), bd)
        yg, s_fin = scan(r, lw, k2, vv, nkk, bb, s0, n_seq, seq_len)
        return _out_proj(out_a, yg, bonus, gr, x.reshape(rows, D_MODEL), wo), s_fin

    rows_p = n_b * seq
    rows_s = n_d * n_dec
    q_p, kt_p, ktb_p, v_p, vb_p, ga_p, rw_p, gr_p = project(x_prompt, jnp.tile(jnp.arange(seq), n_b), rows_p, n_b)
    q_s, kt_s, _, v_s, _, ga_s, rw_s, gr_s = project(x_sample, jnp.tile(past + jnp.arange(n_dec), n_d), rows_s, 1)

    kt_pages = cache_k[layer].transpose(0, 2, 3, 4, 1)
    v_pages = cache_v[layer].reshape(-1, page * H_A, DV_A)
    q3 = q_s.reshape(n_d, n_dec, D_ATT)
    k_new = kt_s.reshape(D_ATT, rows_s).T.reshape(n_d, n_dec, D_ATT)
    oa_s, oa_p = _attn(page_table, jnp.concatenate([q3, q3], axis=1), k_new, v_s.reshape(n_d, n_dec, D_ATT),
                       ga_s.reshape(n_d, n_dec, D_ATT), q_p, ktb_p, vb_p, ga_p, *lam_vecs, subg,
                       kt_pages, v_pages, n_b, seq)

    yp, s_p = mix_out(x_prompt, oa_p, rw_p, gr_p, n_b, seq, jnp.zeros((n_b, 1, SHIFT_W), F32), None,
                      scan_chunked)
    first_s = jnp.repeat(state_shift[layer], n_dec, axis=0)
    ys, s_s = mix_out(x_sample, oa_s.reshape(rows_s, D_ATT), rw_s, gr_s, n_d, n_dec, first_s,
                      state_wkv[layer], scan_short)

    k_prompt = kt_p.reshape(n_b, H_A, 2, DK_A, seq).transpose(0, 4, 1, 2, 3)
    k_sample = kt_s.reshape(H_A, 2, DK_A, n_d, n_dec).transpose(3, 4, 0, 1, 2)
    return (yp.reshape(n_b, seq, D_MODEL), ys.reshape(n_d, n_dec, D_MODEL),
            k_prompt[None], v_p.reshape(n_b, seq, H_A, DV_A)[None], s_p[None],
            rw_p.reshape(n_b, seq, SHIFT_W)[:, -1][None],
            k_sample[None], v_s.reshape(n_d, n_dec, H_A, DV_A)[None], s_s[None],
            rw_s.reshape(n_d, n_dec, SHIFT_W)[:, -1][None])
```

```python
import functools
import math

import jax
import jax.numpy as jnp
from jax import lax
from jax.experimental import pallas as pl
from jax.experimental.pallas import tpu as pltpu

F32 = jnp.float32
BF16 = jnp.bfloat16

D_MODEL = 1024
D_ATT = 512
D_RWKV = 512
DK_A = 64
H_A = 4
DV_A = 128
HS_R = 64
H_R = 8
R_LORA = 64
SHIFT_W = 3 * D_RWKV + 2 * R_LORA
ROPE_THETA = 10000.0
RMS_EPS = 1e-6
GN_EPS = 64e-5
L2_EPS = 1e-12
LAM_INIT = 0.8 - 0.6 * math.exp(-0.3 * 0)

ROW_TILE = 256
ATT_TILE = 512
KEY_TILE = 1024
LOG2E = math.log2(math.e)
SCAN_SEQS = 4
SCAN_HEADS = 2
SCAN_CHUNK = 128
CHUNK = HS_R
CHUNK_SEQS = 4
PAGES_PER_STEP = 16
VMEM_LIMIT = 56 * 1024 * 1024

_NT = (((1,), (1,)), ((), ()))


def _dot(a, b):
    return jnp.dot(a, b, preferred_element_type=F32)


def _segsum(x, bd):
    hi = x.astype(BF16)
    lo = (x - hi.astype(F32)).astype(BF16)
    return _dot(hi, bd) + _dot(lo, bd)


def _diff_lambda(lq1, lk1, lq2, lk2):
    return (jnp.exp(jnp.sum(lq1 * lk1, axis=1, keepdims=True))
            - jnp.exp(jnp.sum(lq2 * lk2, axis=1, keepdims=True)) + LAM_INIT)


def _subln_gate(o0, o1, lam, subg, gate):
    y = o0 - lam * o1
    y = y * lax.rsqrt(jnp.mean(y * y, axis=-1, keepdims=True) + RMS_EPS) * subg
    return (y * (1.0 - LAM_INIT)) * gate


def _proj_kernel(x_ref, lng_ref, wq_ref, wk_ref, wr_ref, cos_ref, sin_ref, qg_ref, kg_ref,
                 q_ref, kt_ref, ktb_ref, v_ref, vb_ref, ga_ref, rw_ref, gr_ref):
    tm = x_ref.shape[0]
    x = x_ref[...]
    h = x * lax.rsqrt(jnp.mean(x * x, axis=-1, keepdims=True) + RMS_EPS) * lng_ref[...]
    h = h.astype(BF16)
    cos = cos_ref[...][None]
    sin = sin_ref[...][None]

    def norm_rope(t, g):
        t3 = t.reshape(2 * H_A, DK_A, tm)
        t3 = t3 * lax.rsqrt(jnp.mean(t3 * t3, axis=1, keepdims=True) + RMS_EPS) * g[None]
        x1 = t3[:, :DK_A // 2]
        x2 = t3[:, DK_A // 2:]
        o = jnp.concatenate([x1 * cos - x2 * sin, x2 * cos + x1 * sin], axis=1)
        return o.reshape(D_ATT, tm)

    qt = lax.dot_general(wq_ref[...], h, _NT, preferred_element_type=F32)
    qt = norm_rope(qt, qg_ref[...]) * (DK_A ** -0.5 * LOG2E)
    q_ref[...] = qt.T.astype(BF16)
    kt = lax.dot_general(wk_ref[...], h, _NT, preferred_element_type=F32)
    kt = norm_rope(kt, kg_ref[...])
    kt_ref[...] = kt
    ktb_ref[...] = kt.astype(BF16)

    v = _dot(h, wr_ref[:, 0:D_ATT])
    for hd in range(H_A):
        v_ref[:, hd, :] = v[:, hd * DV_A:(hd + 1) * DV_A]
    vb_ref[...] = v.astype(BF16)
    za = _dot(h, wr_ref[:, D_ATT:2 * D_ATT])
    ga_ref[...] = za * jax.nn.sigmoid(za)
    rw_ref[...] = _dot(h, wr_ref[:, 2 * D_ATT:2 * D_ATT + SHIFT_W])
    zr = _dot(h, wr_ref[:, 2 * D_ATT + SHIFT_W:])
    gr_ref[...] = zr * jax.nn.sigmoid(zr)


def _proj(x, ln_g, wq_t, wk_t, w_rest, cos_t, sin_t, qg, kg, n_seq, seq_len):
    rows = x.shape[0]
    tm = ROW_TILE
    nsi = seq_len // tm
    tk = min(KEY_TILE, seq_len)
    per_tk = tk // tm
    const = lambda i: (0, 0)
    row_blk = lambda w: pl.BlockSpec((tm, w), lambda i: (i, 0))
    out_shape = (
        jax.ShapeDtypeStruct((rows, D_ATT), BF16),
        jax.ShapeDtypeStruct((n_seq, D_ATT, seq_len), F32),
        jax.ShapeDtypeStruct((n_seq, seq_len // tk, D_ATT, tk), BF16),
        jax.ShapeDtypeStruct((rows, H_A, DV_A), F32),
        jax.ShapeDtypeStruct((rows, D_ATT), BF16),
        jax.ShapeDtypeStruct((rows, D_ATT), F32),
        jax.ShapeDtypeStruct((rows, SHIFT_W), F32),
        jax.ShapeDtypeStruct((rows, D_RWKV), F32),
    )
    out_specs = (
        row_blk(D_ATT),
        pl.BlockSpec((None, D_ATT, tm), lambda i: (i // nsi, 0, i % nsi)),
        pl.BlockSpec((None, None, D_ATT, tm), lambda i: (i // nsi, (i % nsi) // per_tk, 0, i % per_tk)),
        pl.BlockSpec((tm, H_A, DV_A), lambda i: (i, 0, 0)),
        row_blk(D_ATT), row_blk(D_ATT), row_blk(SHIFT_W), row_blk(D_RWKV),
    )
    in_specs = [
        row_blk(D_MODEL),
        pl.BlockSpec((1, D_MODEL), const),
        pl.BlockSpec(wq_t.shape, const),
        pl.BlockSpec(wk_t.shape, const),
        pl.BlockSpec(w_rest.shape, const),
        pl.BlockSpec((DK_A // 2, tm), lambda i: (0, i)),
        pl.BlockSpec((DK_A // 2, tm), lambda i: (0, i)),
        pl.BlockSpec((DK_A, 1), const),
        pl.BlockSpec((DK_A, 1), const),
    ]
    return pl.pallas_call(
        _proj_kernel, out_shape=out_shape, grid=(rows // tm,), in_specs=in_specs, out_specs=out_specs,
        compiler_params=pltpu.CompilerParams(dimension_semantics=("parallel",), vmem_limit_bytes=VMEM_LIMIT),
        name="proj",
    )(x, ln_g, wq_t, wk_t, w_rest, cos_t, sin_t, qg, kg)


def _attn_kernel(pt_ref, q8_ref, kn_ref, vn_ref, gas_ref, qp_ref, ktp_ref, vp_ref, gap_ref,
                 lq1_ref, lk1_ref, lq2_ref, lk2_ref, subg_ref, *refs):
    npg = PAGES_PER_STEP
    k_refs = refs[:npg]
    v_refs = refs[npg:2 * npg]
    os_ref, op_ref, m_sc, l_sc, acc_sc, qs_sc, m_p, acc_p = refs[2 * npg:]
    i = pl.program_id(2)
    j = pl.program_id(3)
    g = j
    n_dec = q8_ref.shape[0] // 2
    tq = qp_ref.shape[0]
    tk = ktp_ref.shape[1]

    @pl.when(j == 0)
    def _():
        m_sc[...] = jnp.full(m_sc.shape, -jnp.inf, F32)
        l_sc[...] = jnp.zeros(l_sc.shape, F32)
        acc_sc[...] = jnp.zeros(acc_sc.shape, F32)
        q = qp_ref[...]
        lane_q = lax.broadcasted_iota(jnp.int32, q.shape, 1)
        zero = jnp.zeros_like(q)
        qs_sc[...] = jnp.concatenate([jnp.where(lane_q < DK_A, q, zero), jnp.where(lane_q >= DK_A, q, zero)],
                                     axis=0)
        m_p[...] = jnp.full(m_p.shape, -jnp.inf, F32)
        acc_p[...] = jnp.zeros(acc_p.shape, F32)

    per_key = tk // tq
    jd = i // per_key

    def prompt_tile(diagonal):
        sp = _dot(qs_sc[...], ktp_ref[...])
        if diagonal:
            row_p = lax.broadcasted_iota(jnp.int32, sp.shape, 0)
            col_p = lax.broadcasted_iota(jnp.int32, sp.shape, 1)
            q_off = (i % per_key) * tq
            sp = jnp.where(col_p <= jnp.where(row_p >= tq, row_p - tq, row_p) + q_off, sp, -jnp.inf)
        mp_prev = m_p[...]
        mp_new = jnp.maximum(mp_prev, jnp.max(sp, axis=1, keepdims=True))
        alpha_p = jnp.exp2(mp_prev - mp_new)
        pp = jnp.exp2(sp - jnp.tile(mp_new, (1, tk // DV_A)))
        v_ext = jnp.concatenate([vp_ref[...], jnp.ones((tk, DV_A), BF16)], axis=1)
        acc_p[...] = jnp.tile(alpha_p, (1, 2)) * acc_p[...] + _dot(pp.astype(BF16), v_ext)
        m_p[...] = mp_new

    o_ref, ga_ref = os_ref, gas_ref
    rowi = lax.broadcasted_iota(jnp.int32, (2 * n_dec, DV_A), 0)
    lane = lax.broadcasted_iota(jnp.int32, (2 * n_dec, DV_A), 1)
    sel = ((rowi < n_dec) & (lane < DK_A)) | ((rowi >= n_dec) & (lane >= DK_A))
    rows_h = 2 * n_dec

    def head_q(h):
        qh = q8_ref[:, h * DV_A:(h + 1) * DV_A]
        return jnp.where(sel, qh, jnp.zeros_like(qh))

    def decode_pages():
        page = k_refs[0].shape[-1]
        s = jnp.concatenate(
            [jnp.concatenate([_dot(head_q(h), k_refs[n][h].astype(BF16).reshape(2 * DK_A, -1))
                              for n in range(npg)], axis=1) for h in range(H_A)], axis=0)
        m_prev = m_sc[...]
        m_new = jnp.maximum(m_prev, jnp.max(s, axis=1, keepdims=True))
        alpha = jnp.exp2(m_prev - m_new)
        p = jnp.exp2(s - m_new)
        l_sc[...] = alpha * l_sc[...] + jnp.sum(p, axis=1, keepdims=True)
        m_sc[...] = m_new
        pb = p.astype(BF16)
        for h in range(H_A):
            acc = alpha[h * rows_h:(h + 1) * rows_h] * acc_sc[h]
            for n in range(npg):
                vh = v_refs[n][pl.ds(h, page, stride=H_A), :]
                acc = acc + _dot(pb[h * rows_h:(h + 1) * rows_h, n * page:(n + 1) * page], vh.astype(BF16))
            acc_sc[h] = acc

    @pl.when(j < jd)
    def _():
        prompt_tile(False)
        decode_pages()

    @pl.when(j == jd)
    def _():
        prompt_tile(True)
        decode_pages()
        acc = acc_p[...]
        o = acc[:, :DV_A] / acc[:, DV_A:]
        lam = _diff_lambda(lq1_ref[...], lk1_ref[...], lq2_ref[...], lk2_ref[...])
        op_ref[...] = _subln_gate(o[:tq], o[tq:], lam, subg_ref[...], gap_ref[...]).astype(BF16)

    @pl.when(j > jd)
    def _():
        decode_pages()

    @pl.when(g == pl.num_programs(3) - 1)
    def _():
        lam = _diff_lambda(lq1_ref[...], lk1_ref[...], lq2_ref[...], lk2_ref[...])
        t_idx = jnp.where(rowi[:, :1] >= n_dec, rowi[:, :1] - n_dec, rowi[:, :1])
        for h in range(H_A):
            hs = slice(h * DV_A, (h + 1) * DV_A)
            qf = head_q(h).astype(F32)
            s_new = []
            for j in range(n_dec):
                sj = jnp.sum(qf * kn_ref[j:j + 1, hs], axis=1, keepdims=True)
                s_new.append(jnp.where(j <= t_idx, sj, -jnp.inf))
            m_prev = m_sc[h * rows_h:(h + 1) * rows_h]
            m_new = m_prev
            for sj in s_new:
                m_new = jnp.maximum(m_new, sj)
            alpha = jnp.exp2(m_prev - m_new)
            l = alpha * l_sc[h * rows_h:(h + 1) * rows_h]
            acc = alpha * acc_sc[h]
            for j, sj in enumerate(s_new):
                pj = jnp.exp2(sj - m_new)
                l = l + pj
                acc = acc + pj * vn_ref[j:j + 1, hs]
            o = acc / l
            o_ref[:, hs] = _subln_gate(o[:n_dec], o[n_dec:], lam, subg_ref[...], ga_ref[:, hs])


def _attn(page_table, q8, k_new, v_new, ga_s, q_p, ktb_p, v_p, ga_p, lq1, lk1, lq2, lk2, subg,
          kt_pages, v_pages, n_seq, seq_len):
    n_d, n_pages = page_table.shape
    n_dec = k_new.shape[1]
    npg = PAGES_PER_STEP
    page = kt_pages.shape[-1]
    tq = ATT_TILE
    tk = KEY_TILE
    nq = seq_len // tq
    nk = seq_len // tk
    per_key = tk // tq
    assert n_d == n_seq * H_A * nq and n_pages == nk * npg
    dec = lambda b, h, i: (b * H_A + h) * nq + i
    key_tile = lambda i, j: jnp.minimum(j, i // per_key)
    vec = lambda w: pl.BlockSpec((1, w), lambda b, h, i, j, pt: (0, 0))
    per_d = lambda r: pl.BlockSpec((None, r, D_ATT), lambda b, h, i, j, pt: (dec(b, h, i), 0, 0))
    q_tile = pl.BlockSpec((tq, DV_A), lambda b, h, i, j, pt: (b * nq + i, h))

    def k_spec(n):
        return pl.BlockSpec((None, H_A, 2, DK_A, page),
                            lambda b, h, i, j, pt: (pt[dec(b, h, i), j * npg + n], 0, 0, 0, 0))

    def v_spec(n):
        return pl.BlockSpec((None, page * H_A, DV_A),
                            lambda b, h, i, j, pt: (pt[dec(b, h, i), j * npg + n], 0, 0))

    in_specs = ([per_d(2 * n_dec), per_d(n_dec), per_d(n_dec), per_d(n_dec),
                 q_tile,
                 pl.BlockSpec((None, None, DV_A, tk), lambda b, h, i, j, pt: (b, key_tile(i, j), h, 0)),
                 pl.BlockSpec((tk, DV_A), lambda b, h, i, j, pt: (b * nk + key_tile(i, j), h)),
                 q_tile,
                 vec(DK_A), vec(DK_A), vec(DK_A), vec(DK_A), vec(DV_A)]
                + [k_spec(n) for n in range(npg)] + [v_spec(n) for n in range(npg)])
    grid_spec = pltpu.PrefetchScalarGridSpec(
        num_scalar_prefetch=1, grid=(n_seq, H_A, nq, nk), in_specs=in_specs,
        out_specs=(per_d(n_dec), q_tile),
        scratch_shapes=[pltpu.VMEM((H_A * 2 * n_dec, 1), F32), pltpu.VMEM((H_A * 2 * n_dec, 1), F32),
                        pltpu.VMEM((H_A, 2 * n_dec, DV_A), F32),
                        pltpu.VMEM((2 * tq, DV_A), BF16), pltpu.VMEM((2 * tq, DV_A), F32),
                        pltpu.VMEM((2 * tq, 2 * DV_A), F32)])
    return pl.pallas_call(
        _attn_kernel,
        out_shape=(jax.ShapeDtypeStruct((n_d, n_dec, D_ATT), F32),
                   jax.ShapeDtypeStruct((n_seq * seq_len, D_ATT), BF16)),
        grid_spec=grid_spec,
        compiler_params=pltpu.CompilerParams(
            dimension_semantics=("arbitrary", "arbitrary", "arbitrary", "arbitrary"),
            vmem_limit_bytes=VMEM_LIMIT),
        name="attn",
    )(page_table, q8, k_new, v_new, ga_s, q_p, ktb_p, v_p, ga_p, lq1, lk1, lq2, lk2, subg,
      *([kt_pages] * npg), *([v_pages] * npg))


def _rwkv_pre_math(rw, prev, mu_ref, w0_ref, wup_ref, a0_ref, aup_ref, kk_ref, ka_ref, rk_ref, bd_ref):
    u = rw + (prev - rw) * mu_ref[...]
    r = u[:, 0:D_RWKV]
    k = u[:, D_RWKV:2 * D_RWKV]
    v = u[:, 2 * D_RWKV:3 * D_RWKV]
    wd = u[:, 3 * D_RWKV:3 * D_RWKV + R_LORA]
    ad = u[:, 3 * D_RWKV + R_LORA:]
    w_log = -jax.nn.softplus(-(w0_ref[...] + _dot(jnp.tanh(wd).astype(BF16), wup_ref[...]))) - 0.5
    log_decay = -jnp.exp(w_log)
    a = jax.nn.sigmoid(a0_ref[...] + _dot(ad.astype(BF16), aup_ref[...]))
    bd = bd_ref[...]
    kk = k * kk_ref[...]
    kk = kk / jnp.maximum(jnp.sqrt(_segsum(kk * kk, bd)), L2_EPS)
    k2 = k * (1.0 + (a - 1.0) * ka_ref[...])
    return r, log_decay, k2, v, -kk, kk * a, _segsum(r * k2 * rk_ref[...], bd) * v


def _rwkv_pre_kernel(seq_len, rw_ref, prev_ref, first_ref, mu_ref, w0_ref, wup_ref, a0_ref, aup_ref,
                     kk_ref, ka_ref, rk_ref, bd_ref,
                     r_out, w_out, k_out, v_out, nkk_out, b_out, bonus_out):
    tm = rw_ref.shape[0]
    rw = rw_ref[...]
    rolled = pltpu.roll(rw, 1, axis=0)
    rowi = lax.broadcasted_iota(jnp.int32, (tm, 1), 0)
    if seq_len >= tm:
        at_start = (pl.program_id(0) % (seq_len // tm)) == 0
        row0 = jnp.where(at_start, first_ref[...], prev_ref[7:8, :])
        prev = jnp.where(rowi == 0, row0, rolled)
    else:
        prev = jnp.where(rowi % seq_len == 0, first_ref[...], rolled)
    outs = _rwkv_pre_math(rw, prev, mu_ref, w0_ref, wup_ref, a0_ref, aup_ref, kk_ref, ka_ref, rk_ref, bd_ref)
    for ref, val in zip((r_out, w_out, k_out, v_out, nkk_out, b_out, bonus_out), outs):
        ref[...] = val


def _rwkv_pre(rw, first, seq_len, mu, w0, wup, a0, aup, k_k, k_a, r_k, bd):
    rows = rw.shape[0]
    tm = ROW_TILE
    const = lambda i: (0, 0)
    row_blk = lambda w: pl.BlockSpec((tm, w), lambda i: (i, 0))
    if seq_len >= tm:
        first_spec = pl.BlockSpec((None, 1, SHIFT_W), lambda i: (i * tm // seq_len, 0, 0))
    else:
        first_spec = row_blk(SHIFT_W)
    in_specs = [
        row_blk(SHIFT_W),
        pl.BlockSpec((8, SHIFT_W), lambda i: (jnp.maximum(i * (tm // 8) - 1, 0), 0)),
        first_spec,
        pl.BlockSpec((1, SHIFT_W), const),
        pl.BlockSpec((1, D_RWKV), const), pl.BlockSpec((R_LORA, D_RWKV), const),
        pl.BlockSpec((1, D_RWKV), const), pl.BlockSpec((R_LORA, D_RWKV), const),
        pl.BlockSpec((1, D_RWKV), const), pl.BlockSpec((1, D_RWKV), const), pl.BlockSpec((1, D_RWKV), const),
        pl.BlockSpec((D_RWKV, D_RWKV), const),
    ]
    out = jax.ShapeDtypeStruct((rows, D_RWKV), F32)
    return pl.pallas_call(
        functools.partial(_rwkv_pre_kernel, seq_len),
        out_shape=(out,) * 7, grid=(rows // tm,), in_specs=in_specs, out_specs=(row_blk(D_RWKV),) * 7,
        compiler_params=pltpu.CompilerParams(dimension_semantics=("parallel",), vmem_limit_bytes=VMEM_LIMIT),
        name="rwkv_pre",
    )(rw, rw, first, mu, w0, wup, a0, aup, k_k, k_a, r_k, bd)


def _rwkv_scan_kernel(r_ref, w_ref, k_ref, v_ref, nkk_ref, b_ref, s0_ref, gng_ref, gnb_ref,
                      y_ref, s_ref):
    nb, tc = r_ref.shape[0], r_ref.shape[1]
    n_grp = H_R // SCAN_HEADS
    gw = SCAN_HEADS * HS_R
    c = pl.program_id(1)

    @pl.when(c == 0)
    def _():
        s_ref[...] = s0_ref[...]

    lane = lax.broadcasted_iota(jnp.int32, (HS_R, gw), 1)
    row = lax.broadcasted_iota(jnp.int32, (HS_R, gw), 0)
    head_of_lane = lane // HS_R
    diag_of_head = [lane == row + e * HS_R for e in range(SCAN_HEADS)]
    head_of_lane1 = lax.broadcasted_iota(jnp.int32, (1, gw), 1) // HS_R
    row8 = lax.broadcasted_iota(jnp.int32, (H_R, gw), 0)
    lane8 = lax.broadcasted_iota(jnp.int32, (H_R, gw), 1)

    unroll = min(8, tc)

    def steps(g, carry):
        t0 = pl.multiple_of(g * unroll, unroll)
        rows = pl.ds(t0, unroll)
        for j in range(unroll):
            for b in range(nb):
                y8 = None
                for p in range(n_grp):
                    ls = slice(p * gw, (p + 1) * gw)
                    row = lambda ref: ref[b, rows, ls][j:j + 1]
                    s = s_ref[b, p]
                    nkk = row(nkk_ref)
                    sa = None
                    for e in range(SCAN_HEADS):
                        in_head = head_of_lane1 == e
                        part = jnp.sum(s * jnp.where(in_head, nkk, 0.0), axis=1, keepdims=True)
                        sa = part if sa is None else jnp.where(head_of_lane == e, part, sa)
                    vr = row(v_ref)
                    vc = None
                    for e in range(SCAN_HEADS):
                        part = jnp.sum(jnp.where(diag_of_head[e], vr, 0.0), axis=1, keepdims=True)
                        vc = part if vc is None else jnp.where(head_of_lane == e, part, vc)
                    s = s * jnp.exp(row(w_ref)) + sa * row(b_ref) + vc * row(k_ref)
                    s_ref[b, p] = s
                    sel = row8 == p * SCAN_HEADS + lane8 // HS_R
                    r2 = jnp.where(sel, jnp.broadcast_to(row(r_ref), (H_R, gw)), 0.0).astype(BF16)
                    yp = lax.dot_general(r2, s.astype(BF16), _NT, preferred_element_type=F32)
                    y8 = yp if y8 is None else y8 + yp
                y_ref[b, t0 + j] = y8
        return carry

    lax.fori_loop(0, tc // unroll, steps, 0)

    y = y_ref[...]
    mean = jnp.mean(y, axis=-1, keepdims=True)
    var = jnp.mean(jnp.square(y - mean), axis=-1, keepdims=True)
    y_ref[...] = (y - mean) * lax.rsqrt(var + GN_EPS) * gng_ref[...] + gnb_ref[...]


def _rwkv_scan(r, w, k, v, nkk, b, s0, gn_g, gn_b, n_seq, seq_len):
    nb = SCAN_SEQS
    tc = min(SCAN_CHUNK, seq_len)
    n_grp = H_R // SCAN_HEADS
    gw = SCAN_HEADS * HS_R
    seq = lambda a: a.reshape(n_seq, seq_len, D_RWKV)
    in_blk = pl.BlockSpec((nb, tc, D_RWKV), lambda g, c: (g, c, 0))
    st_blk = pl.BlockSpec((nb, n_grp, HS_R, gw), lambda g, c: (g, 0, 0, 0))
    gn_blk = pl.BlockSpec((1, 1, H_R, HS_R), lambda g, c: (0, 0, 0, 0))
    return pl.pallas_call(
        _rwkv_scan_kernel,
        out_shape=(jax.ShapeDtypeStruct((n_seq, seq_len, H_R, HS_R), F32),
                   jax.ShapeDtypeStruct((n_seq, n_grp, HS_R, gw), F32)),
        grid=(n_seq // nb, seq_len // tc),
        in_specs=[in_blk] * 6 + [st_blk, gn_blk, gn_blk],
        out_specs=(pl.BlockSpec((nb, tc, H_R, HS_R), lambda g, c: (g, c, 0, 0)), st_blk),
        compiler_params=pltpu.CompilerParams(dimension_semantics=("parallel", "arbitrary"),
                                             vmem_limit_bytes=VMEM_LIMIT),
        name="rwkv_scan",
    )(seq(r), seq(w), seq(k), seq(v), seq(nkk), seq(b), s0,
      gn_g.reshape(1, 1, H_R, HS_R), gn_b.reshape(1, 1, H_R, HS_R))


def _rwkv_chunk_kernel(rw_ref, mu_ref, w0_ref, wup_ref, a0_ref, aup_ref, kk_ref, ka_ref, rk_ref, bd_ref,
                       gng_ref, gnb_ref, y_ref, bonus_ref, s_ref, last_sc):
    nb, C = rw_ref.shape[0], rw_ref.shape[1]
    n_pair = H_R // 2
    pw = 2 * HS_R
    c = pl.program_id(1)

    @pl.when(c == 0)
    def _():
        s_ref[...] = jnp.zeros(s_ref.shape, F32)
        last_sc[...] = jnp.zeros(last_sc.shape, F32)

    rw = rw_ref[...].reshape(nb * C, SHIFT_W)
    first = jnp.concatenate([jnp.broadcast_to(last_sc[b:b + 1, :], (C, SHIFT_W)) for b in range(nb)], axis=0)
    row_in_chunk = lax.broadcasted_iota(jnp.int32, (nb * C, 1), 0) % C
    prev = jnp.where(row_in_chunk == 0, first, pltpu.roll(rw, 1, axis=0))
    for b in range(nb):
        last_sc[b:b + 1, :] = rw[(b + 1) * C - 1:(b + 1) * C, :]
    r_all, lw_all, k_all, v_all, nkk_all, b_all, bonus = _rwkv_pre_math(
        rw, prev, mu_ref, w0_ref, wup_ref, a0_ref, aup_ref, kk_ref, ka_ref, rk_ref, bd_ref)
    bonus_ref[...] = bonus.reshape(nb, C, D_RWKV)

    lane = lax.broadcasted_iota(jnp.int32, (C, pw), 1)
    t_idx = lax.broadcasted_iota(jnp.int32, (C, pw), 0)
    head0 = lane < HS_R
    s_idx = lane % C
    strict = t_idx > s_idx
    incl = t_idx >= s_idx
    eye = jnp.where(t_idx == s_idx, 1.0, 0.0).astype(F32)
    tri = jnp.where(lax.broadcasted_iota(jnp.int32, (C, C), 0) >= lax.broadcasted_iota(jnp.int32, (C, C), 1),
                    1.0, 0.0).astype(BF16)
    same_head = (lax.broadcasted_iota(jnp.int32, (pw, pw), 0) // HS_R
                 == lax.broadcasted_iota(jnp.int32, (pw, pw), 1) // HS_R)

    def stack(x):
        return jnp.concatenate([jnp.where(head0, x, 0.0), jnp.where(head0, 0.0, x)], axis=0).astype(BF16)

    def per_head(coef, x):
        return _dot(coef.astype(BF16), stack(x))

    def head_mean(x):
        return jnp.where(head0, jnp.sum(jnp.where(head0, x, 0.0), axis=1, keepdims=True),
                         jnp.sum(jnp.where(head0, 0.0, x), axis=1, keepdims=True)) * (1.0 / HS_R)

    units = [(b, p) for b in range(nb) for p in range(n_pair)]
    ls = lambda p: slice(p * pw, (p + 1) * pw)
    take = lambda arr: [arr[b * C:(b + 1) * C, ls(p)] for b, p in units]
    lw = take(lw_all)
    lw_hi = [x.astype(BF16) for x in lw]
    cum = [_dot(tri, h) + _dot(tri, (x - h.astype(F32)).astype(BF16)) for x, h in zip(lw, lw_hi)]
    p_in = [jnp.exp(x) for x in cum]
    p_inv = [jnp.exp(-x) for x in cum]
    nt = [x * jnp.exp(cm - l) for x, cm, l in zip(take(nkk_all), cum, lw)]
    bt = [x * pi for x, pi in zip(take(b_all), p_inv)]
    kt = [x * pi for x, pi in zip(take(k_all), p_inv)]
    rt = [x * pp for x, pp in zip(take(r_all), p_in)]
    vv = take(v_all)

    lhs = [jnp.concatenate([n_, r_], axis=0).astype(BF16) for n_, r_ in zip(nt, rt)]
    coef = [lax.dot_general(l_, jnp.concatenate([stack(b_), stack(k_)], axis=0), _NT,
                            preferred_element_type=F32) for l_, b_, k_ in zip(lhs, bt, kt)]
    a_pow = [jnp.where(strict, cf[:C, :pw], 0.0) for cf in coef]
    b_m = [jnp.where(strict, cf[:C, pw:], 0.0) for cf in coef]
    ab_m = [jnp.where(incl, cf[C:, :pw], 0.0) for cf in coef]
    ak_m = [jnp.where(incl, cf[C:, pw:], 0.0) for cf in coef]

    inv = [eye + a for a in a_pow]
    for _ in range(C.bit_length() - 2):
        a_pow = [per_head(a, a) for a in a_pow]
        inv = [per_head(i_, eye + a) for i_, a in zip(inv, a_pow)]

    v_stack = [stack(x) for x in vv]
    bv = [_dot(m.astype(BF16), vs) for m, vs in zip(b_m, v_stack)]
    akv = [_dot(m.astype(BF16), vs) for m, vs in zip(ak_m, v_stack)]

    st = [s_ref[b, p] for b, p in units]
    ns_rs = [lax.dot_general(l_, s_.astype(BF16), _NT, preferred_element_type=F32)
             for l_, s_ in zip(lhs, st)]
    u = [per_head(i_, x[:C] + bv_) for i_, x, bv_ in zip(inv, ns_rs, bv)]
    y = [x[C:] + per_head(m, u_) + akv_ for x, m, u_, akv_ in zip(ns_rs, ab_m, u, akv)]

    for i, (b, p) in enumerate(units):
        uv_t = jnp.concatenate([u[i], vv[i]], axis=0).T.astype(BF16)
        upd = _dot(uv_t, jnp.concatenate([bt[i], kt[i]], axis=0).astype(BF16))
        s_ref[b, p] = jnp.where(same_head, (st[i] + upd) * p_in[i][C - 1:C, :], 0.0)

    for i, (b, p) in enumerate(units):
        d = y[i] - head_mean(y[i])
        var = head_mean(d * d)
        y_ref[b, :, ls(p)] = d * lax.rsqrt(var + GN_EPS) * gng_ref[:, ls(p)] + gnb_ref[:, ls(p)]


def _rwkv_chunk(rw, mu, w0, wup, a0, aup, k_k, k_a, r_k, bd, gn_g, gn_b, n_seq, seq_len):
    nb = CHUNK_SEQS
    C = CHUNK
    n_pair = H_R // 2
    const = lambda g, c: (0, 0)
    vec = lambda w: pl.BlockSpec((1, w), const)
    out_blk = pl.BlockSpec((nb, C, D_RWKV), lambda g, c: (g, c, 0))
    in_specs = [pl.BlockSpec((nb, C, SHIFT_W), lambda g, c: (g, c, 0)), vec(SHIFT_W),
                vec(D_RWKV), pl.BlockSpec((R_LORA, D_RWKV), const),
                vec(D_RWKV), pl.BlockSpec((R_LORA, D_RWKV), const),
                vec(D_RWKV), vec(D_RWKV), vec(D_RWKV), pl.BlockSpec((D_RWKV, D_RWKV), const),
                vec(D_RWKV), vec(D_RWKV)]
    seq3 = jax.ShapeDtypeStruct((n_seq, seq_len, D_RWKV), F32)
    return pl.pallas_call(
        _rwkv_chunk_kernel,
        out_shape=(seq3, seq3, jax.ShapeDtypeStruct((n_seq, n_pair, 2 * HS_R, 2 * HS_R), F32)),
        grid=(n_seq // nb, seq_len // C),
        in_specs=in_specs,
        out_specs=(out_blk, out_blk,
                   pl.BlockSpec((nb, n_pair, 2 * HS_R, 2 * HS_R), lambda g, c: (g, 0, 0, 0))),
        scratch_shapes=[pltpu.VMEM((nb, SHIFT_W), F32)],
        compiler_params=pltpu.CompilerParams(dimension_semantics=("parallel", "arbitrary"),
                                             vmem_limit_bytes=VMEM_LIMIT),
        name="rwkv_chunk",
    )(rw.reshape(n_seq, seq_len, SHIFT_W), mu, w0, wup, a0, aup, k_k, k_a, r_k, bd, gn_g[None], gn_b[None])


def _out_proj_kernel(oa_ref, yg_ref, bonus_ref, gr_ref, x_ref, wo_ref, y_ref):
    mix_r = ((yg_ref[...] + bonus_ref[...]) * gr_ref[...]).astype(BF16)
    y_ref[...] = (x_ref[...] + _dot(oa_ref[...].astype(BF16), wo_ref[0:D_ATT, :])
                  + _dot(mix_r, wo_ref[D_ATT:, :]))


def _out_proj(oa, yg, bonus, gr, x, wo):
    rows = x.shape[0]
    tm = ROW_TILE
    row_blk = lambda w: pl.BlockSpec((tm, w), lambda i: (i, 0))
    return pl.pallas_call(
        _out_proj_kernel, out_shape=jax.ShapeDtypeStruct((rows, D_MODEL), F32), grid=(rows // tm,),
        in_specs=[row_blk(D_ATT), row_blk(D_RWKV), row_blk(D_RWKV), row_blk(D_RWKV), row_blk(D_MODEL),
                  pl.BlockSpec(wo.shape, lambda i: (0, 0))],
        out_specs=row_blk(D_MODEL),
        compiler_params=pltpu.CompilerParams(dimension_semantics=("parallel",), vmem_limit_bytes=VMEM_LIMIT),
        name="out_proj",
    )(oa, yg, bonus, gr, x, wo)


def _rope_tables(pos):
    half = DK_A // 2
    inv = 1.0 / (ROPE_THETA ** (jnp.arange(half, dtype=F32) / half))
    ang = inv[:, None] * pos.astype(F32)[None, :]
    return jnp.cos(ang), jnp.sin(ang)


def _pair_state(s):
    n = s.shape[0]
    g = H_R // SCAN_HEADS
    return (s.reshape(n, g, SCAN_HEADS, HS_R, HS_R).transpose(0, 1, 3, 2, 4)
            .reshape(n, g, HS_R, SCAN_HEADS * HS_R))


def _unpair_state(s):
    n = s.shape[0]
    g = H_R // SCAN_HEADS
    return s.reshape(n, g, HS_R, SCAN_HEADS, HS_R).transpose(0, 1, 3, 2, 4).reshape(n, H_R, HS_R, HS_R)


def kernel(x_prompt, x_sample, cache_k, cache_v, state_wkv, state_shift, page_table, ln_g, w_in, q_norm_g,
           k_norm_g, lambda_q1, lambda_k1, lambda_q2, lambda_k2, subln_g, shift_mu, w0, w_lora_up, a0,
           a_lora_up, k_k, k_a, r_k, gn_g, gn_b, w_out):
    n_b, seq = x_prompt.shape[:2]
    n_d, n_dec = x_sample.shape[:2]
    page = cache_k.shape[2]
    past = page_table.shape[1] * page
    layer = 0

    w_in_b = w_in[layer].astype(BF16)
    wq_t = w_in_b[:, 0:D_ATT].T
    wk_t = w_in_b[:, D_ATT:2 * D_ATT].T
    w_rest = w_in_b[:, 2 * D_ATT:]
    wo = w_out[layer].astype(BF16)
    wup = w_lora_up[layer].astype(BF16)
    aup = a_lora_up[layer].astype(BF16)
    qg = q_norm_g[layer].reshape(DK_A, 1)
    kg = k_norm_g[layer].reshape(DK_A, 1)
    seg = jnp.arange(D_RWKV) // HS_R
    bd = (seg[:, None] == seg[None, :]).astype(BF16)
    lam_vecs = (lambda_q1[layer][None], lambda_k1[layer][None], lambda_q2[layer][None], lambda_k2[layer][None])
    subg = subln_g[layer][None]
    row = lambda a: a[layer][None]

    def scan_short(r, lw, k2, vv, nkk, bb, s0, n_seq, seq_len):
        yg, s_fin = _rwkv_scan(r, lw, k2, vv, nkk, bb, _pair_state(s0), gn_g[layer], gn_b[layer],
                               n_seq, seq_len)
        return yg.reshape(n_seq * seq_len, D_RWKV), _unpair_state(s_fin)

    pre_params = (row(shift_mu), row(w0), wup, row(a0), aup, row(k_k), row(k_a), row(r_k), bd)

    def mix_long(rw, n_seq, seq_len):
        yg, bonus, s_fin = _rwkv_chunk(rw, *pre_params, gn_g[layer], gn_b[layer], n_seq, seq_len)
        s6 = s_fin.reshape(n_seq, H_R // 2, 2, HS_R, 2, HS_R)
        s_heads = jnp.stack([s6[:, :, e, :, e, :] for e in range(2)], axis=2)
        rows = n_seq * seq_len
        return yg.reshape(rows, D_RWKV), bonus.reshape(rows, D_RWKV), s_heads.reshape(n_seq, H_R, HS_R, HS_R)

    def mix_short(rw, first, s0, n_seq, seq_len):
        r, lw, k2, vv, nkk, bb, bonus = _rwkv_pre(rw, first, seq_len, *pre_params)
        yg, s_fin = scan_short(r, lw, k2, vv, nkk, bb, s0, n_seq, seq_len)
        return yg, bonus, s_fin

    def project(x, pos, rows, proj_seqs):
        cos_t, sin_t = _rope_tables(pos)
        return _proj(x.reshape(rows, D_MODEL), ln_g[layer][None], wq_t, wk_t, w_rest, cos_t, sin_t, qg, kg,
                     proj_seqs, rows // proj_seqs)

    rows_p = n_b * seq
    rows_s = n_d * n_dec
    q_p, kt_p, ktb_p, v_p, vb_p, ga_p, rw_p, gr_p = project(x_prompt, jnp.tile(jnp.arange(seq), n_b), rows_p, n_b)
    q_s, kt_s, _, v_s, _, ga_s, rw_s, gr_s = project(x_sample, jnp.tile(past + jnp.arange(n_dec), n_d), rows_s, 1)

    kt_pages = cache_k[layer].transpose(0, 2, 3, 4, 1)
    v_pages = cache_v[layer].reshape(-1, page * H_A, DV_A)
    q3 = q_s.reshape(n_d, n_dec, D_ATT)
    k_new = kt_s.reshape(D_ATT, rows_s).T.reshape(n_d, n_dec, D_ATT)
    oa_s, oa_p = _attn(page_table, jnp.concatenate([q3, q3], axis=1), k_new, v_s.reshape(n_d, n_dec, D_ATT),
                       ga_s.reshape(n_d, n_dec, D_ATT), q_p, ktb_p, vb_p, ga_p, *lam_vecs, subg,
                       kt_pages, v_pages, n_b, seq)

    yg_p, bonus_p, s_p = mix_long(rw_p, n_b, seq)
    yp = _out_proj(oa_p, yg_p, bonus_p, gr_p, x_prompt.reshape(rows_p, D_MODEL), wo)
    first_s = jnp.repeat(state_shift[layer], n_dec, axis=0)
    yg_s, bonus_s, s_s = mix_short(rw_s, first_s, state_wkv[layer], n_d, n_dec)
    ys = _out_proj(oa_s.reshape(rows_s, D_ATT), yg_s, bonus_s, gr_s, x_sample.reshape(rows_s, D_MODEL), wo)

    k_prompt = kt_p.reshape(n_b, H_A, 2, DK_A, seq).transpose(0, 4, 1, 2, 3)
    k_sample = kt_s.reshape(H_A, 2, DK_A, n_d, n_dec).transpose(3, 4, 0, 1, 2)
    return (yp.reshape(n_b, seq, D_MODEL), ys.reshape(n_d, n_dec, D_MODEL),
            k_prompt[None], v_p.reshape(n_b, seq, H_A, DV_A)[None], s_p[None],
            rw_p.reshape(n_b, seq, SHIFT_W)[:, -1][None],
            k_sample[None], v_s.reshape(n_d, n_dec, H_A, DV_A)[None], s_s[None],
            rw_s.reshape(n_d, n_dec, SHIFT_W)[:, -1][None])
```

```python
import functools
import math

import jax
import jax.numpy as jnp
from jax import lax
from jax.experimental import pallas as pl
from jax.experimental.pallas import tpu as pltpu

F32 = jnp.float32
BF16 = jnp.bfloat16

D_MODEL = 1024
D_ATT = 512
D_RWKV = 512
DK_A = 64
H_A = 4
DV_A = 128
HS_R = 64
H_R = 8
R_LORA = 64
SHIFT_W = 3 * D_RWKV + 2 * R_LORA
ROPE_THETA = 10000.0
RMS_EPS = 1e-6
GN_EPS = 64e-5
L2_EPS = 1e-12
LAM_INIT = 0.8 - 0.6 * math.exp(-0.3 * 0)

ROW_TILE = 256
ATT_TILE = 512
KEY_TILE = 1024
LOG2E = math.log2(math.e)
CHUNK = HS_R
CHUNK_SEQS = 4
PAGES_PER_STEP = 16
VMEM_LIMIT = 56 * 1024 * 1024

_NT = (((1,), (1,)), ((), ()))


def _dot(a, b):
    return jnp.dot(a, b, preferred_element_type=F32)


def _segsum(x, bd):
    hi = x.astype(BF16)
    lo = (x - hi.astype(F32)).astype(BF16)
    return _dot(hi, bd) + _dot(lo, bd)


def _diff_lambda(lq1, lk1, lq2, lk2):
    return (jnp.exp(jnp.sum(lq1 * lk1, axis=1, keepdims=True))
            - jnp.exp(jnp.sum(lq2 * lk2, axis=1, keepdims=True)) + LAM_INIT)


def _subln_gate(o0, o1, lam, subg, gate):
    y = o0 - lam * o1
    y = y * lax.rsqrt(jnp.mean(y * y, axis=-1, keepdims=True) + RMS_EPS) * subg
    return (y * (1.0 - LAM_INIT)) * gate


def _proj_kernel(x_ref, lng_ref, wq_ref, wk_ref, wr_ref, cos_ref, sin_ref, qg_ref, kg_ref,
                 q_ref, kt_ref, ktb_ref, v_ref, vb_ref, ga_ref, rw_ref, gr_ref):
    tm = x_ref.shape[0]
    x = x_ref[...]
    h = x * lax.rsqrt(jnp.mean(x * x, axis=-1, keepdims=True) + RMS_EPS) * lng_ref[...]
    h = h.astype(BF16)
    cos = cos_ref[...][None]
    sin = sin_ref[...][None]

    def norm_rope(t, g):
        t3 = t.reshape(2 * H_A, DK_A, tm)
        t3 = t3 * lax.rsqrt(jnp.mean(t3 * t3, axis=1, keepdims=True) + RMS_EPS) * g[None]
        x1 = t3[:, :DK_A // 2]
        x2 = t3[:, DK_A // 2:]
        o = jnp.concatenate([x1 * cos - x2 * sin, x2 * cos + x1 * sin], axis=1)
        return o.reshape(D_ATT, tm)

    qt = lax.dot_general(wq_ref[...], h, _NT, preferred_element_type=F32)
    qt = norm_rope(qt, qg_ref[...]) * (DK_A ** -0.5 * LOG2E)
    q_ref[...] = qt.T.astype(BF16)
    kt = lax.dot_general(wk_ref[...], h, _NT, preferred_element_type=F32)
    kt = norm_rope(kt, kg_ref[...])
    kt_ref[...] = kt
    ktb_ref[...] = kt.astype(BF16)

    v = _dot(h, wr_ref[:, 0:D_ATT])
    for hd in range(H_A):
        v_ref[:, hd, :] = v[:, hd * DV_A:(hd + 1) * DV_A]
    vb_ref[...] = v.astype(BF16)
    za = _dot(h, wr_ref[:, D_ATT:2 * D_ATT])
    ga_ref[...] = za * jax.nn.sigmoid(za)
    rw_ref[...] = _dot(h, wr_ref[:, 2 * D_ATT:2 * D_ATT + SHIFT_W])
    zr = _dot(h, wr_ref[:, 2 * D_ATT + SHIFT_W:])
    gr_ref[...] = zr * jax.nn.sigmoid(zr)


def _proj(x, ln_g, wq_t, wk_t, w_rest, cos_t, sin_t, qg, kg, n_seq, seq_len):
    rows = x.shape[0]
    tm = ROW_TILE
    nsi = seq_len // tm
    tk = min(KEY_TILE, seq_len)
    per_tk = tk // tm
    const = lambda i: (0, 0)
    row_blk = lambda w: pl.BlockSpec((tm, w), lambda i: (i, 0))
    out_shape = (
        jax.ShapeDtypeStruct((rows, D_ATT), BF16),
        jax.ShapeDtypeStruct((n_seq, D_ATT, seq_len), F32),
        jax.ShapeDtypeStruct((n_seq, seq_len // tk, D_ATT, tk), BF16),
        jax.ShapeDtypeStruct((rows, H_A, DV_A), F32),
        jax.ShapeDtypeStruct((rows, D_ATT), BF16),
        jax.ShapeDtypeStruct((rows, D_ATT), F32),
        jax.ShapeDtypeStruct((rows, SHIFT_W), F32),
        jax.ShapeDtypeStruct((rows, D_RWKV), F32),
    )
    out_specs = (
        row_blk(D_ATT),
        pl.BlockSpec((None, D_ATT, tm), lambda i: (i // nsi, 0, i % nsi)),
        pl.BlockSpec((None, None, D_ATT, tm), lambda i: (i // nsi, (i % nsi) // per_tk, 0, i % per_tk)),
        pl.BlockSpec((tm, H_A, DV_A), lambda i: (i, 0, 0)),
        row_blk(D_ATT), row_blk(D_ATT), row_blk(SHIFT_W), row_blk(D_RWKV),
    )
    in_specs = [
        row_blk(D_MODEL),
        pl.BlockSpec((1, D_MODEL), const),
        pl.BlockSpec(wq_t.shape, const),
        pl.BlockSpec(wk_t.shape, const),
        pl.BlockSpec(w_rest.shape, const),
        pl.BlockSpec((DK_A // 2, tm), lambda i: (0, i)),
        pl.BlockSpec((DK_A // 2, tm), lambda i: (0, i)),
        pl.BlockSpec((DK_A, 1), const),
        pl.BlockSpec((DK_A, 1), const),
    ]
    return pl.pallas_call(
        _proj_kernel, out_shape=out_shape, grid=(rows // tm,), in_specs=in_specs, out_specs=out_specs,
        compiler_params=pltpu.CompilerParams(dimension_semantics=("parallel",), vmem_limit_bytes=VMEM_LIMIT),
        name="proj",
    )(x, ln_g, wq_t, wk_t, w_rest, cos_t, sin_t, qg, kg)


def _attn_kernel(pt_ref, q8_ref, kn_ref, vn_ref, gas_ref, qp_ref, ktp_ref, vp_ref, gap_ref,
                 lq1_ref, lk1_ref, lq2_ref, lk2_ref, subg_ref, *refs):
    npg = PAGES_PER_STEP
    k_refs = refs[:npg]
    v_refs = refs[npg:2 * npg]
    os_ref, op_ref, m_sc, l_sc, acc_sc, qs_sc, m_p, acc_p = refs[2 * npg:]
    i = pl.program_id(2)
    j = pl.program_id(3)
    g = j
    n_dec = q8_ref.shape[0] // 2
    tq = qp_ref.shape[0]
    tk = ktp_ref.shape[1]

    @pl.when(j == 0)
    def _():
        m_sc[...] = jnp.full(m_sc.shape, -jnp.inf, F32)
        l_sc[...] = jnp.zeros(l_sc.shape, F32)
        acc_sc[...] = jnp.zeros(acc_sc.shape, F32)
        q = qp_ref[...]
        lane_q = lax.broadcasted_iota(jnp.int32, q.shape, 1)
        zero = jnp.zeros_like(q)
        qs_sc[...] = jnp.concatenate([jnp.where(lane_q < DK_A, q, zero), jnp.where(lane_q >= DK_A, q, zero)],
                                     axis=0)
        m_p[...] = jnp.full(m_p.shape, -jnp.inf, F32)
        acc_p[...] = jnp.zeros(acc_p.shape, F32)

    per_key = tk // tq
    jd = i // per_key

    def prompt_tile(diagonal):
        sp = _dot(qs_sc[...], ktp_ref[...])
        if diagonal:
            row_p = lax.broadcasted_iota(jnp.int32, sp.shape, 0)
            col_p = lax.broadcasted_iota(jnp.int32, sp.shape, 1)
            q_off = (i % per_key) * tq
            sp = jnp.where(col_p <= jnp.where(row_p >= tq, row_p - tq, row_p) + q_off, sp, -jnp.inf)
        mp_prev = m_p[...]
        mp_new = jnp.maximum(mp_prev, jnp.max(sp, axis=1, keepdims=True))
        alpha_p = jnp.exp2(mp_prev - mp_new)
        pp = jnp.exp2(sp - jnp.tile(mp_new, (1, tk // DV_A)))
        v_ext = jnp.concatenate([vp_ref[...], jnp.ones((tk, DV_A), BF16)], axis=1)
        acc_p[...] = jnp.tile(alpha_p, (1, 2)) * acc_p[...] + _dot(pp.astype(BF16), v_ext)
        m_p[...] = mp_new

    o_ref, ga_ref = os_ref, gas_ref
    rowi = lax.broadcasted_iota(jnp.int32, (2 * n_dec, DV_A), 0)
    lane = lax.broadcasted_iota(jnp.int32, (2 * n_dec, DV_A), 1)
    sel = ((rowi < n_dec) & (lane < DK_A)) | ((rowi >= n_dec) & (lane >= DK_A))
    rows_h = 2 * n_dec

    def head_q(h):
        qh = q8_ref[:, h * DV_A:(h + 1) * DV_A]
        return jnp.where(sel, qh, jnp.zeros_like(qh))

    def decode_pages():
        page = k_refs[0].shape[-1]
        s = jnp.concatenate(
            [jnp.concatenate([_dot(head_q(h), k_refs[n][h].astype(BF16).reshape(2 * DK_A, -1))
                              for n in range(npg)], axis=1) for h in range(H_A)], axis=0)
        m_prev = m_sc[...]
        m_new = jnp.maximum(m_prev, jnp.max(s, axis=1, keepdims=True))
        alpha = jnp.exp2(m_prev - m_new)
        p = jnp.exp2(s - m_new)
        l_sc[...] = alpha * l_sc[...] + jnp.sum(p, axis=1, keepdims=True)
        m_sc[...] = m_new
        pb = p.astype(BF16)
        for h in range(H_A):
            acc = alpha[h * rows_h:(h + 1) * rows_h] * acc_sc[h]
            for n in range(npg):
                vh = v_refs[n][pl.ds(h, page, stride=H_A), :]
                acc = acc + _dot(pb[h * rows_h:(h + 1) * rows_h, n * page:(n + 1) * page], vh.astype(BF16))
            acc_sc[h] = acc

    @pl.when(j < jd)
    def _():
        prompt_tile(False)
        decode_pages()

    @pl.when(j == jd)
    def _():
        prompt_tile(True)
        decode_pages()
        acc = acc_p[...]
        o = acc[:, :DV_A] / acc[:, DV_A:]
        lam = _diff_lambda(lq1_ref[...], lk1_ref[...], lq2_ref[...], lk2_ref[...])
        op_ref[...] = _subln_gate(o[:tq], o[tq:], lam, subg_ref[...], gap_ref[...]).astype(BF16)

    @pl.when(j > jd)
    def _():
        decode_pages()

    @pl.when(g == pl.num_programs(3) - 1)
    def _():
        lam = _diff_lambda(lq1_ref[...], lk1_ref[...], lq2_ref[...], lk2_ref[...])
        t_idx = jnp.where(rowi[:, :1] >= n_dec, rowi[:, :1] - n_dec, rowi[:, :1])
        for h in range(H_A):
            hs = slice(h * DV_A, (h + 1) * DV_A)
            qf = head_q(h).astype(F32)
            s_new = []
            for j in range(n_dec):
                sj = jnp.sum(qf * kn_ref[j:j + 1, hs], axis=1, keepdims=True)
                s_new.append(jnp.where(j <= t_idx, sj, -jnp.inf))
            m_prev = m_sc[h * rows_h:(h + 1) * rows_h]
            m_new = m_prev
            for sj in s_new:
                m_new = jnp.maximum(m_new, sj)
            alpha = jnp.exp2(m_prev - m_new)
            l = alpha * l_sc[h * rows_h:(h + 1) * rows_h]
            acc = alpha * acc_sc[h]
            for j, sj in enumerate(s_new):
                pj = jnp.exp2(sj - m_new)
                l = l + pj
                acc = acc + pj * vn_ref[j:j + 1, hs]
            o = acc / l
            o_ref[:, hs] = _subln_gate(o[:n_dec], o[n_dec:], lam, subg_ref[...], ga_ref[:, hs])


def _attn(page_table, q8, k_new, v_new, ga_s, q_p, ktb_p, v_p, ga_p, lq1, lk1, lq2, lk2, subg,
          kt_pages, v_pages, n_seq, seq_len):
    n_d, n_pages = page_table.shape
    n_dec = k_new.shape[1]
    npg = PAGES_PER_STEP
    page = kt_pages.shape[-1]
    tq = ATT_TILE
    tk = KEY_TILE
    nq = seq_len // tq
    nk = seq_len // tk
    per_key = tk // tq
    assert n_d == n_seq * H_A * nq and n_pages == nk * npg
    dec = lambda b, h, i: (b * H_A + h) * nq + i
    key_tile = lambda i, j: jnp.minimum(j, i // per_key)
    vec = lambda w: pl.BlockSpec((1, w), lambda b, h, i, j, pt: (0, 0))
    per_d = lambda r: pl.BlockSpec((None, r, D_ATT), lambda b, h, i, j, pt: (dec(b, h, i), 0, 0))
    q_tile = pl.BlockSpec((tq, DV_A), lambda b, h, i, j, pt: (b * nq + i, h))

    def k_spec(n):
        return pl.BlockSpec((None, H_A, 2, DK_A, page),
                            lambda b, h, i, j, pt: (pt[dec(b, h, i), j * npg + n], 0, 0, 0, 0))

    def v_spec(n):
        return pl.BlockSpec((None, page * H_A, DV_A),
                            lambda b, h, i, j, pt: (pt[dec(b, h, i), j * npg + n], 0, 0))

    in_specs = ([per_d(2 * n_dec), per_d(n_dec), per_d(n_dec), per_d(n_dec),
                 q_tile,
                 pl.BlockSpec((None, None, DV_A, tk), lambda b, h, i, j, pt: (b, key_tile(i, j), h, 0)),
                 pl.BlockSpec((tk, DV_A), lambda b, h, i, j, pt: (b * nk + key_tile(i, j), h)),
                 q_tile,
                 vec(DK_A), vec(DK_A), vec(DK_A), vec(DK_A), vec(DV_A)]
                + [k_spec(n) for n in range(npg)] + [v_spec(n) for n in range(npg)])
    grid_spec = pltpu.PrefetchScalarGridSpec(
        num_scalar_prefetch=1, grid=(n_seq, H_A, nq, nk), in_specs=in_specs,
        out_specs=(per_d(n_dec), q_tile),
        scratch_shapes=[pltpu.VMEM((H_A * 2 * n_dec, 1), F32), pltpu.VMEM((H_A * 2 * n_dec, 1), F32),
                        pltpu.VMEM((H_A, 2 * n_dec, DV_A), F32),
                        pltpu.VMEM((2 * tq, DV_A), BF16), pltpu.VMEM((2 * tq, DV_A), F32),
                        pltpu.VMEM((2 * tq, 2 * DV_A), F32)])
    return pl.pallas_call(
        _attn_kernel,
        out_shape=(jax.ShapeDtypeStruct((n_d, n_dec, D_ATT), F32),
                   jax.ShapeDtypeStruct((n_seq * seq_len, D_ATT), BF16)),
        grid_spec=grid_spec,
        compiler_params=pltpu.CompilerParams(
            dimension_semantics=("arbitrary", "arbitrary", "arbitrary", "arbitrary"),
            vmem_limit_bytes=VMEM_LIMIT),
        name="attn",
    )(page_table, q8, k_new, v_new, ga_s, q_p, ktb_p, v_p, ga_p, lq1, lk1, lq2, lk2, subg,
      *([kt_pages] * npg), *([v_pages] * npg))


def _rwkv_pre_math(rw, prev, mu_ref, w0_ref, wup_ref, a0_ref, aup_ref, kk_ref, ka_ref, rk_ref, bd_ref):
    u = rw + (prev - rw) * mu_ref[...]
    r = u[:, 0:D_RWKV]
    k = u[:, D_RWKV:2 * D_RWKV]
    v = u[:, 2 * D_RWKV:3 * D_RWKV]
    wd = u[:, 3 * D_RWKV:3 * D_RWKV + R_LORA]
    ad = u[:, 3 * D_RWKV + R_LORA:]
    w_log = -jax.nn.softplus(-(w0_ref[...] + _dot(jnp.tanh(wd).astype(BF16), wup_ref[...]))) - 0.5
    log_decay = -jnp.exp(w_log)
    a = jax.nn.sigmoid(a0_ref[...] + _dot(ad.astype(BF16), aup_ref[...]))
    bd = bd_ref[...]
    kk = k * kk_ref[...]
    kk = kk / jnp.maximum(jnp.sqrt(_segsum(kk * kk, bd)), L2_EPS)
    k2 = k * (1.0 + (a - 1.0) * ka_ref[...])
    return r, log_decay, k2, v, -kk, kk * a, _segsum(r * k2 * rk_ref[...], bd) * v


def _rwkv_pre_kernel(seq_len, rw_ref, first_ref, mu_ref, w0_ref, wup_ref, a0_ref, aup_ref,
                     kk_ref, ka_ref, rk_ref, bd_ref,
                     r_out, w_out, k_out, v_out, nkk_out, b_out, bonus_out):
    tm = rw_ref.shape[0]
    rw = rw_ref[...]
    rowi = lax.broadcasted_iota(jnp.int32, (tm, 1), 0)
    prev = jnp.where(rowi % seq_len == 0, first_ref[...], pltpu.roll(rw, 1, axis=0))
    outs = _rwkv_pre_math(rw, prev, mu_ref, w0_ref, wup_ref, a0_ref, aup_ref, kk_ref, ka_ref, rk_ref, bd_ref)
    for ref, val in zip((r_out, w_out, k_out, v_out, nkk_out, b_out, bonus_out), outs):
        ref[...] = val


def _rwkv_pre(rw, first, seq_len, mu, w0, wup, a0, aup, k_k, k_a, r_k, bd):
    rows = rw.shape[0]
    tm = ROW_TILE
    assert tm % seq_len == 0
    const = lambda i: (0, 0)
    row_blk = lambda w: pl.BlockSpec((tm, w), lambda i: (i, 0))
    in_specs = [
        row_blk(SHIFT_W),
        row_blk(SHIFT_W),
        pl.BlockSpec((1, SHIFT_W), const),
        pl.BlockSpec((1, D_RWKV), const), pl.BlockSpec((R_LORA, D_RWKV), const),
        pl.BlockSpec((1, D_RWKV), const), pl.BlockSpec((R_LORA, D_RWKV), const),
        pl.BlockSpec((1, D_RWKV), const), pl.BlockSpec((1, D_RWKV), const), pl.BlockSpec((1, D_RWKV), const),
        pl.BlockSpec((D_RWKV, D_RWKV), const),
    ]
    out = jax.ShapeDtypeStruct((rows, D_RWKV), F32)
    return pl.pallas_call(
        functools.partial(_rwkv_pre_kernel, seq_len),
        out_shape=(out,) * 7, grid=(rows // tm,), in_specs=in_specs, out_specs=(row_blk(D_RWKV),) * 7,
        compiler_params=pltpu.CompilerParams(dimension_semantics=("parallel",), vmem_limit_bytes=VMEM_LIMIT),
        name="rwkv_pre",
    )(rw, first, mu, w0, wup, a0, aup, k_k, k_a, r_k, bd)


def _rwkv_steps_kernel(r_ref, lw_ref, k_ref, v_ref, nkk_ref, b_ref, s0_ref, gng_ref, gnb_ref,
                       y_ref, s_ref, xt_sc, y_sc):
    n_b = s0_ref.shape[-1]
    n_t = r_ref.shape[0] // n_b
    for a, ref in enumerate((r_ref, lw_ref, k_ref, v_ref, nkk_ref, b_ref)):
        for t in range(n_t):
            xt_sc[a, t] = ref[pl.ds(t, n_b, stride=n_t), :].T
    s_ref[...] = s0_ref[...]

    for t in range(n_t):
        for e in range(2):
            hk = slice(e * HS_R, (e + 1) * HS_R)
            r_e = xt_sc[0, t, hk]
            w_e = jnp.exp(xt_sc[1, t, hk])
            k_e = xt_sc[2, t, hk]
            nk_e = xt_sc[4, t, hk]
            b_e = xt_sc[5, t, hk]

            def rows(g, carry):
                v0 = pl.multiple_of(g * 8, 8)
                v8 = xt_sc[3, t, pl.ds(e * HS_R + v0, 8), :]
                ys = []
                for i in range(8):
                    s = s_ref[e, v0 + i]
                    sa = jnp.sum(s * nk_e, axis=0, keepdims=True)
                    s = s * w_e + sa * b_e + v8[i:i + 1] * k_e
                    s_ref[e, v0 + i] = s
                    ys.append(jnp.sum(s * r_e, axis=0, keepdims=True))
                y_sc[pl.ds(e * HS_R + v0, 8), :] = jnp.concatenate(ys, axis=0)
                return carry

            lax.fori_loop(0, HS_R // 8, rows, 0)

        y = y_sc[...]
        halves = []
        for e in range(2):
            yh = y[e * HS_R:(e + 1) * HS_R]
            d = yh - jnp.mean(yh, axis=0, keepdims=True)
            halves.append(d * lax.rsqrt(jnp.mean(d * d, axis=0, keepdims=True) + GN_EPS))
        y = jnp.concatenate(halves, axis=0) * gng_ref[...] + gnb_ref[...]
        y_ref[pl.ds(t, n_b, stride=n_t), :] = y.T


def _rwkv_steps(r, lw, k, v, nkk, b, s0, gn_g, gn_b, n_seq, seq_len):
    rows = n_seq * seq_len
    pw = 2 * HS_R
    n_pair = H_R // 2
    in_blk = pl.BlockSpec((rows, pw), lambda p: (0, p))
    st_blk = pl.BlockSpec((2, HS_R, HS_R, n_seq), lambda p: (p, 0, 0, 0))
    gn_blk = pl.BlockSpec((None, pw, 1), lambda p: (p, 0, 0))
    return pl.pallas_call(
        _rwkv_steps_kernel,
        out_shape=(jax.ShapeDtypeStruct((rows, D_RWKV), F32), jax.ShapeDtypeStruct(s0.shape, F32)),
        grid=(n_pair,),
        in_specs=[in_blk] * 6 + [st_blk, gn_blk, gn_blk],
        out_specs=(in_blk, st_blk),
        scratch_shapes=[pltpu.VMEM((6, seq_len, pw, n_seq), F32), pltpu.VMEM((pw, n_seq), F32)],
        compiler_params=pltpu.CompilerParams(dimension_semantics=("parallel",), vmem_limit_bytes=VMEM_LIMIT),
        name="rwkv_steps",
    )(r, lw, k, v, nkk, b, s0, gn_g.reshape(n_pair, pw, 1), gn_b.reshape(n_pair, pw, 1))


def _rwkv_chunk_kernel(rw_ref, mu_ref, w0_ref, wup_ref, a0_ref, aup_ref, kk_ref, ka_ref, rk_ref, bd_ref,
                       gng_ref, gnb_ref, y_ref, bonus_ref, s_ref, last_sc):
    nb, C = rw_ref.shape[0], rw_ref.shape[1]
    n_pair = H_R // 2
    pw = 2 * HS_R
    c = pl.program_id(1)

    @pl.when(c == 0)
    def _():
        s_ref[...] = jnp.zeros(s_ref.shape, F32)
        last_sc[...] = jnp.zeros(last_sc.shape, F32)

    rw = rw_ref[...].reshape(nb * C, SHIFT_W)
    first = jnp.concatenate([jnp.broadcast_to(last_sc[b:b + 1, :], (C, SHIFT_W)) for b in range(nb)], axis=0)
    row_in_chunk = lax.broadcasted_iota(jnp.int32, (nb * C, 1), 0) % C
    prev = jnp.where(row_in_chunk == 0, first, pltpu.roll(rw, 1, axis=0))
    for b in range(nb):
        last_sc[b:b + 1, :] = rw[(b + 1) * C - 1:(b + 1) * C, :]
    r_all, lw_all, k_all, v_all, nkk_all, b_all, bonus = _rwkv_pre_math(
        rw, prev, mu_ref, w0_ref, wup_ref, a0_ref, aup_ref, kk_ref, ka_ref, rk_ref, bd_ref)
    bonus_ref[...] = bonus.reshape(nb, C, D_RWKV)

    lane = lax.broadcasted_iota(jnp.int32, (C, pw), 1)
    t_idx = lax.broadcasted_iota(jnp.int32, (C, pw), 0)
    head0 = lane < HS_R
    s_idx = lane % C
    strict = t_idx > s_idx
    incl = t_idx >= s_idx
    eye = jnp.where(t_idx == s_idx, 1.0, 0.0).astype(F32)
    tri = jnp.where(lax.broadcasted_iota(jnp.int32, (C, C), 0) >= lax.broadcasted_iota(jnp.int32, (C, C), 1),
                    1.0, 0.0).astype(BF16)
    same_head = (lax.broadcasted_iota(jnp.int32, (pw, pw), 0) // HS_R
                 == lax.broadcasted_iota(jnp.int32, (pw, pw), 1) // HS_R)

    def stack(x):
        return jnp.concatenate([jnp.where(head0, x, 0.0), jnp.where(head0, 0.0, x)], axis=0).astype(BF16)

    def per_head(coef, x):
        return _dot(coef.astype(BF16), stack(x))

    def head_mean(x):
        return jnp.where(head0, jnp.sum(jnp.where(head0, x, 0.0), axis=1, keepdims=True),
                         jnp.sum(jnp.where(head0, 0.0, x), axis=1, keepdims=True)) * (1.0 / HS_R)

    units = [(b, p) for b in range(nb) for p in range(n_pair)]
    ls = lambda p: slice(p * pw, (p + 1) * pw)
    take = lambda arr: [arr[b * C:(b + 1) * C, ls(p)] for b, p in units]
    lw = take(lw_all)
    lw_hi = [x.astype(BF16) for x in lw]
    cum = [_dot(tri, h) + _dot(tri, (x - h.astype(F32)).astype(BF16)) for x, h in zip(lw, lw_hi)]
    p_in = [jnp.exp(x) for x in cum]
    p_inv = [jnp.exp(-x) for x in cum]
    nt = [x * jnp.exp(cm - l) for x, cm, l in zip(take(nkk_all), cum, lw)]
    bt = [x * pi for x, pi in zip(take(b_all), p_inv)]
    kt = [x * pi for x, pi in zip(take(k_all), p_inv)]
    rt = [x * pp for x, pp in zip(take(r_all), p_in)]
    vv = take(v_all)

    lhs = [jnp.concatenate([n_, r_], axis=0).astype(BF16) for n_, r_ in zip(nt, rt)]
    coef = [lax.dot_general(l_, jnp.concatenate([stack(b_), stack(k_)], axis=0), _NT,
                            preferred_element_type=F32) for l_, b_, k_ in zip(lhs, bt, kt)]
    a_pow = [jnp.where(strict, cf[:C, :pw], 0.0) for cf in coef]
    b_m = [jnp.where(strict, cf[:C, pw:], 0.0) for cf in coef]
    ab_m = [jnp.where(incl, cf[C:, :pw], 0.0) for cf in coef]
    ak_m = [jnp.where(incl, cf[C:, pw:], 0.0) for cf in coef]

    inv = [eye + a for a in a_pow]
    for _ in range(C.bit_length() - 2):
        a_pow = [per_head(a, a) for a in a_pow]
        inv = [per_head(i_, eye + a) for i_, a in zip(inv, a_pow)]

    v_stack = [stack(x) for x in vv]
    bv = [_dot(m.astype(BF16), vs) for m, vs in zip(b_m, v_stack)]
    akv = [_dot(m.astype(BF16), vs) for m, vs in zip(ak_m, v_stack)]

    st = [s_ref[b, p] for b, p in units]
    ns_rs = [lax.dot_general(l_, s_.astype(BF16), _NT, preferred_element_type=F32)
             for l_, s_ in zip(lhs, st)]
    u = [per_head(i_, x[:C] + bv_) for i_, x, bv_ in zip(inv, ns_rs, bv)]
    y = [x[C:] + per_head(m, u_) + akv_ for x, m, u_, akv_ in zip(ns_rs, ab_m, u, akv)]

    for i, (b, p) in enumerate(units):
        uv_t = jnp.concatenate([u[i], vv[i]], axis=0).T.astype(BF16)
        upd = _dot(uv_t, jnp.concatenate([bt[i], kt[i]], axis=0).astype(BF16))
        s_ref[b, p] = jnp.where(same_head, (st[i] + upd) * p_in[i][C - 1:C, :], 0.0)

    for i, (b, p) in enumerate(units):
        d = y[i] - head_mean(y[i])
        var = head_mean(d * d)
        y_ref[b, :, ls(p)] = d * lax.rsqrt(var + GN_EPS) * gng_ref[:, ls(p)] + gnb_ref[:, ls(p)]


def _rwkv_chunk(rw, mu, w0, wup, a0, aup, k_k, k_a, r_k, bd, gn_g, gn_b, n_seq, seq_len):
    nb = CHUNK_SEQS
    C = CHUNK
    n_pair = H_R // 2
    const = lambda g, c: (0, 0)
    vec = lambda w: pl.BlockSpec((1, w), const)
    out_blk = pl.BlockSpec((nb, C, D_RWKV), lambda g, c: (g, c, 0))
    in_specs = [pl.BlockSpec((nb, C, SHIFT_W), lambda g, c: (g, c, 0)), vec(SHIFT_W),
                vec(D_RWKV), pl.BlockSpec((R_LORA, D_RWKV), const),
                vec(D_RWKV), pl.BlockSpec((R_LORA, D_RWKV), const),
                vec(D_RWKV), vec(D_RWKV), vec(D_RWKV), pl.BlockSpec((D_RWKV, D_RWKV), const),
                vec(D_RWKV), vec(D_RWKV)]
    seq3 = jax.ShapeDtypeStruct((n_seq, seq_len, D_RWKV), F32)
    return pl.pallas_call(
        _rwkv_chunk_kernel,
        out_shape=(seq3, seq3, jax.ShapeDtypeStruct((n_seq, n_pair, 2 * HS_R, 2 * HS_R), F32)),
        grid=(n_seq // nb, seq_len // C),
        in_specs=in_specs,
        out_specs=(out_blk, out_blk,
                   pl.BlockSpec((nb, n_pair, 2 * HS_R, 2 * HS_R), lambda g, c: (g, 0, 0, 0))),
        scratch_shapes=[pltpu.VMEM((nb, SHIFT_W), F32)],
        compiler_params=pltpu.CompilerParams(dimension_semantics=("parallel", "arbitrary"),
                                             vmem_limit_bytes=VMEM_LIMIT),
        name="rwkv_chunk",
    )(rw.reshape(n_seq, seq_len, SHIFT_W), mu, w0, wup, a0, aup, k_k, k_a, r_k, bd, gn_g[None], gn_b[None])


def _out_proj_kernel(oa_ref, yg_ref, bonus_ref, gr_ref, x_ref, wo_ref, y_ref):
    mix_r = ((yg_ref[...] + bonus_ref[...]) * gr_ref[...]).astype(BF16)
    y_ref[...] = (x_ref[...] + _dot(oa_ref[...].astype(BF16), wo_ref[0:D_ATT, :])
                  + _dot(mix_r, wo_ref[D_ATT:, :]))


def _out_proj(oa, yg, bonus, gr, x, wo):
    rows = x.shape[0]
    tm = ROW_TILE
    row_blk = lambda w: pl.BlockSpec((tm, w), lambda i: (i, 0))
    return pl.pallas_call(
        _out_proj_kernel, out_shape=jax.ShapeDtypeStruct((rows, D_MODEL), F32), grid=(rows // tm,),
        in_specs=[row_blk(D_ATT), row_blk(D_RWKV), row_blk(D_RWKV), row_blk(D_RWKV), row_blk(D_MODEL),
                  pl.BlockSpec(wo.shape, lambda i: (0, 0))],
        out_specs=row_blk(D_MODEL),
        compiler_params=pltpu.CompilerParams(dimension_semantics=("parallel",), vmem_limit_bytes=VMEM_LIMIT),
        name="out_proj",
    )(oa, yg, bonus, gr, x, wo)


def _rope_tables(pos):
    half = DK_A // 2
    inv = 1.0 / (ROPE_THETA ** (jnp.arange(half, dtype=F32) / half))
    ang = inv[:, None] * pos.astype(F32)[None, :]
    return jnp.cos(ang), jnp.sin(ang)


def kernel(x_prompt, x_sample, cache_k, cache_v, state_wkv, state_shift, page_table, ln_g, w_in, q_norm_g,
           k_norm_g, lambda_q1, lambda_k1, lambda_q2, lambda_k2, subln_g, shift_mu, w0, w_lora_up, a0,
           a_lora_up, k_k, k_a, r_k, gn_g, gn_b, w_out):
    n_b, seq = x_prompt.shape[:2]
    n_d, n_dec = x_sample.shape[:2]
    page = cache_k.shape[2]
    past = page_table.shape[1] * page
    layer = 0

    w_in_b = w_in[layer].astype(BF16)
    wq_t = w_in_b[:, 0:D_ATT].T
    wk_t = w_in_b[:, D_ATT:2 * D_ATT].T
    w_rest = w_in_b[:, 2 * D_ATT:]
    wo = w_out[layer].astype(BF16)
    wup = w_lora_up[layer].astype(BF16)
    aup = a_lora_up[layer].astype(BF16)
    qg = q_norm_g[layer].reshape(DK_A, 1)
    kg = k_norm_g[layer].reshape(DK_A, 1)
    seg = jnp.arange(D_RWKV) // HS_R
    bd = (seg[:, None] == seg[None, :]).astype(BF16)
    lam_vecs = (lambda_q1[layer][None], lambda_k1[layer][None], lambda_q2[layer][None], lambda_k2[layer][None])
    subg = subln_g[layer][None]
    row = lambda a: a[layer][None]

    pre_params = (row(shift_mu), row(w0), wup, row(a0), aup, row(k_k), row(k_a), row(r_k), bd)

    def mix_long(rw, n_seq, seq_len):
        yg, bonus, s_fin = _rwkv_chunk(rw, *pre_params, gn_g[layer], gn_b[layer], n_seq, seq_len)
        s6 = s_fin.reshape(n_seq, H_R // 2, 2, HS_R, 2, HS_R)
        s_heads = jnp.stack([s6[:, :, e, :, e, :] for e in range(2)], axis=2)
        rows = n_seq * seq_len
        return yg.reshape(rows, D_RWKV), bonus.reshape(rows, D_RWKV), s_heads.reshape(n_seq, H_R, HS_R, HS_R)

    def mix_short(rw, first, s0, n_seq, seq_len):
        r, lw, k2, vv, nkk, bb, bonus = _rwkv_pre(rw, first, seq_len, *pre_params)
        yg, s_fin = _rwkv_steps(r, lw, k2, vv, nkk, bb, s0.transpose(1, 2, 3, 0), gn_g[layer], gn_b[layer],
                                n_seq, seq_len)
        return yg, bonus, s_fin.transpose(3, 0, 1, 2)

    def project(x, pos, rows, proj_seqs):
        cos_t, sin_t = _rope_tables(pos)
        return _proj(x.reshape(rows, D_MODEL), ln_g[layer][None], wq_t, wk_t, w_rest, cos_t, sin_t, qg, kg,
                     proj_seqs, rows // proj_seqs)

    rows_p = n_b * seq
    rows_s = n_d * n_dec
    q_p, kt_p, ktb_p, v_p, vb_p, ga_p, rw_p, gr_p = project(x_prompt, jnp.tile(jnp.arange(seq), n_b), rows_p, n_b)
    q_s, kt_s, _, v_s, _, ga_s, rw_s, gr_s = project(x_sample, jnp.tile(past + jnp.arange(n_dec), n_d), rows_s, 1)

    kt_pages = cache_k[layer].transpose(0, 2, 3, 4, 1)
    v_pages = cache_v[layer].reshape(-1, page * H_A, DV_A)
    q3 = q_s.reshape(n_d, n_dec, D_ATT)
    k_new = kt_s.reshape(D_ATT, rows_s).T.reshape(n_d, n_dec, D_ATT)
    oa_s, oa_p = _attn(page_table, jnp.concatenate([q3, q3], axis=1), k_new, v_s.reshape(n_d, n_dec, D_ATT),
                       ga_s.reshape(n_d, n_dec, D_ATT), q_p, ktb_p, vb_p, ga_p, *lam_vecs, subg,
                       kt_pages, v_pages, n_b, seq)

    yg_p, bonus_p, s_p = mix_long(rw_p, n_b, seq)
    yp = _out_proj(oa_p, yg_p, bonus_p, gr_p, x_prompt.reshape(rows_p, D_MODEL), wo)
    first_s = jnp.repeat(state_shift[layer], n_dec, axis=0)
    yg_s, bonus_s, s_s = mix_short(rw_s, first_s, state_wkv[layer], n_d, n_dec)
    ys = _out_proj(oa_s.reshape(rows_s, D_ATT), yg_s, bonus_s, gr_s, x_sample.reshape(rows_s, D_MODEL), wo)

    k_prompt = kt_p.reshape(n_b, H_A, 2, DK_A, seq).transpose(0, 4, 1, 2, 3)
    k_sample = kt_s.reshape(H_A, 2, DK_A, n_d, n_dec).transpose(3, 4, 0, 1, 2)
    return (yp.reshape(n_b, seq, D_MODEL), ys.reshape(n_d, n_dec, D_MODEL),
            k_prompt[None], v_p.reshape(n_b, seq, H_A, DV_A)[None], s_p[None],
            rw_p.reshape(n_b, seq, SHIFT_W)[:, -1][None],
            k_sample[None], v_s.reshape(n_d, n_dec, H_A, DV_A)[None], s_s[None],
            rw_s.reshape(n_d, n_dec, SHIFT_W)[:, -1][None])
```

```python
import functools
import math

import jax
import jax.numpy as jnp
from jax import lax
from jax.experimental import pallas as pl
from jax.experimental.pallas import tpu as pltpu

F32 = jnp.float32
BF16 = jnp.bfloat16

D_MODEL = 1024
D_ATT = 512
D_RWKV = 512
DK_A = 64
H_A = 4
DV_A = 128
HS_R = 64
H_R = 8
R_LORA = 64
SHIFT_W = 3 * D_RWKV + 2 * R_LORA
ROPE_THETA = 10000.0
RMS_EPS = 1e-6
GN_EPS = 64e-5
L2_EPS = 1e-12
LAM_INIT = 0.8 - 0.6 * math.exp(-0.3 * 0)

ROW_TILE = 256
OUT_TILE = 512
ATT_TILE = 512
KEY_TILE = 1024
LOG2E = math.log2(math.e)
CHUNK = HS_R
CHUNK_SEQS = 4
PAGES_PER_STEP = 16
VMEM_LIMIT = 56 * 1024 * 1024

_NT = (((1,), (1,)), ((), ()))


def _dot(a, b):
    return jnp.dot(a, b, preferred_element_type=F32)


def _segsum(x, bd):
    hi = x.astype(BF16)
    lo = (x - hi.astype(F32)).astype(BF16)
    return _dot(hi, bd) + _dot(lo, bd)


def _diff_lambda(lq1, lk1, lq2, lk2):
    return (jnp.exp(jnp.sum(lq1 * lk1, axis=1, keepdims=True))
            - jnp.exp(jnp.sum(lq2 * lk2, axis=1, keepdims=True)) + LAM_INIT)


def _subln_gate(o0, o1, lam, subg, gate):
    y = o0 - lam * o1
    y = y * lax.rsqrt(jnp.mean(y * y, axis=-1, keepdims=True) + RMS_EPS) * subg
    return (y * (1.0 - LAM_INIT)) * gate


def _proj_kernel(x_ref, lng_ref, wq_ref, wk_ref, wr_ref, cos_ref, sin_ref, qg_ref, kg_ref,
                 q_ref, kt_ref, ktb_ref, v_ref, vb_ref, ga_ref, rw_ref, gr_ref):
    tm = x_ref.shape[0]
    x = x_ref[...]
    h = x * lax.rsqrt(jnp.mean(x * x, axis=-1, keepdims=True) + RMS_EPS) * lng_ref[...]
    h = h.astype(BF16)
    cos = cos_ref[...][None]
    sin = sin_ref[...][None]

    def norm_rope(t, g):
        t3 = t.reshape(2 * H_A, DK_A, tm)
        t3 = t3 * lax.rsqrt(jnp.mean(t3 * t3, axis=1, keepdims=True) + RMS_EPS) * g[None]
        x1 = t3[:, :DK_A // 2]
        x2 = t3[:, DK_A // 2:]
        o = jnp.concatenate([x1 * cos - x2 * sin, x2 * cos + x1 * sin], axis=1)
        return o.reshape(D_ATT, tm)

    qt = lax.dot_general(wq_ref[...], h, _NT, preferred_element_type=F32)
    qt = norm_rope(qt, qg_ref[...]) * (DK_A ** -0.5 * LOG2E)
    q_ref[...] = qt.T.astype(BF16)
    kt = lax.dot_general(wk_ref[...], h, _NT, preferred_element_type=F32)
    kt = norm_rope(kt, kg_ref[...])
    kt_ref[...] = kt
    ktb_ref[...] = kt.astype(BF16)

    v = _dot(h, wr_ref[:, 0:D_ATT])
    for hd in range(H_A):
        v_ref[:, hd, :] = v[:, hd * DV_A:(hd + 1) * DV_A]
    vb_ref[...] = v.astype(BF16)
    za = _dot(h, wr_ref[:, D_ATT:2 * D_ATT])
    ga_ref[...] = za * jax.nn.sigmoid(za)
    rw_ref[...] = _dot(h, wr_ref[:, 2 * D_ATT:2 * D_ATT + SHIFT_W])
    zr = _dot(h, wr_ref[:, 2 * D_ATT + SHIFT_W:])
    gr_ref[...] = zr * jax.nn.sigmoid(zr)


def _proj(x, ln_g, wq_t, wk_t, w_rest, cos_t, sin_t, qg, kg, n_seq, seq_len):
    rows = x.shape[0]
    tm = ROW_TILE
    nsi = seq_len // tm
    tk = min(KEY_TILE, seq_len)
    per_tk = tk // tm
    const = lambda i: (0, 0)
    row_blk = lambda w: pl.BlockSpec((tm, w), lambda i: (i, 0))
    out_shape = (
        jax.ShapeDtypeStruct((rows, D_ATT), BF16),
        jax.ShapeDtypeStruct((n_seq, D_ATT, seq_len), F32),
        jax.ShapeDtypeStruct((n_seq, seq_len // tk, D_ATT, tk), BF16),
        jax.ShapeDtypeStruct((rows, H_A, DV_A), F32),
        jax.ShapeDtypeStruct((rows, D_ATT), BF16),
        jax.ShapeDtypeStruct((rows, D_ATT), F32),
        jax.ShapeDtypeStruct((rows, SHIFT_W), F32),
        jax.ShapeDtypeStruct((rows, D_RWKV), F32),
    )
    out_specs = (
        row_blk(D_ATT),
        pl.BlockSpec((None, D_ATT, tm), lambda i: (i // nsi, 0, i % nsi)),
        pl.BlockSpec((None, None, D_ATT, tm), lambda i: (i // nsi, (i % nsi) // per_tk, 0, i % per_tk)),
        pl.BlockSpec((tm, H_A, DV_A), lambda i: (i, 0, 0)),
        row_blk(D_ATT), row_blk(D_ATT), row_blk(SHIFT_W), row_blk(D_RWKV),
    )
    in_specs = [
        row_blk(D_MODEL),
        pl.BlockSpec((1, D_MODEL), const),
        pl.BlockSpec(wq_t.shape, const),
        pl.BlockSpec(wk_t.shape, const),
        pl.BlockSpec(w_rest.shape, const),
        pl.BlockSpec((DK_A // 2, tm), lambda i: (0, i)),
        pl.BlockSpec((DK_A // 2, tm), lambda i: (0, i)),
        pl.BlockSpec((DK_A, 1), const),
        pl.BlockSpec((DK_A, 1), const),
    ]
    return pl.pallas_call(
        _proj_kernel, out_shape=out_shape, grid=(rows // tm,), in_specs=in_specs, out_specs=out_specs,
        compiler_params=pltpu.CompilerParams(dimension_semantics=("parallel",), vmem_limit_bytes=VMEM_LIMIT),
        name="proj",
    )(x, ln_g, wq_t, wk_t, w_rest, cos_t, sin_t, qg, kg)


def _attn_kernel(pt_ref, q8_ref, kn_ref, vn_ref, gas_ref, qp_ref, ktp_ref, vp_ref, gap_ref,
                 lq1_ref, lk1_ref, lq2_ref, lk2_ref, subg_ref, *refs):
    npg = PAGES_PER_STEP
    k_refs = refs[:npg]
    v_refs = refs[npg:2 * npg]
    os_ref, op_ref, m_sc, l_sc, acc_sc, qs_sc, m_p, acc_p = refs[2 * npg:]
    i = pl.program_id(2)
    j = pl.program_id(3)
    g = j
    n_dec = q8_ref.shape[0] // 2
    tq = qp_ref.shape[0]
    tk = ktp_ref.shape[1]

    @pl.when(j == 0)
    def _():
        m_sc[...] = jnp.full(m_sc.shape, -jnp.inf, F32)
        l_sc[...] = jnp.zeros(l_sc.shape, F32)
        acc_sc[...] = jnp.zeros(acc_sc.shape, F32)
        q = qp_ref[...]
        lane_q = lax.broadcasted_iota(jnp.int32, q.shape, 1)
        zero = jnp.zeros_like(q)
        qs_sc[...] = jnp.concatenate([jnp.where(lane_q < DK_A, q, zero), jnp.where(lane_q >= DK_A, q, zero)],
                                     axis=0)
        m_p[...] = jnp.full(m_p.shape, -jnp.inf, F32)
        acc_p[...] = jnp.zeros(acc_p.shape, F32)

    per_key = tk // tq
    jd = i // per_key

    def prompt_tile(diagonal):
        sp = _dot(qs_sc[...], ktp_ref[...])
        if diagonal:
            row_p = lax.broadcasted_iota(jnp.int32, sp.shape, 0)
            col_p = lax.broadcasted_iota(jnp.int32, sp.shape, 1)
            q_off = (i % per_key) * tq
            sp = jnp.where(col_p <= jnp.where(row_p >= tq, row_p - tq, row_p) + q_off, sp, -jnp.inf)
        mp_prev = m_p[...]
        mp_new = jnp.maximum(mp_prev, jnp.max(sp, axis=1, keepdims=True))
        alpha_p = jnp.exp2(mp_prev - mp_new)
        pp = jnp.exp2(sp - jnp.tile(mp_new, (1, tk // DV_A)))
        v_ext = jnp.concatenate([vp_ref[...], jnp.ones((tk, DV_A), BF16)], axis=1)
        acc_p[...] = jnp.tile(alpha_p, (1, 2)) * acc_p[...] + _dot(pp.astype(BF16), v_ext)
        m_p[...] = mp_new

    o_ref, ga_ref = os_ref, gas_ref
    rowi = lax.broadcasted_iota(jnp.int32, (2 * n_dec, DV_A), 0)
    lane = lax.broadcasted_iota(jnp.int32, (2 * n_dec, DV_A), 1)
    sel = ((rowi < n_dec) & (lane < DK_A)) | ((rowi >= n_dec) & (lane >= DK_A))
    rows_h = 2 * n_dec

    def head_q(h):
        qh = q8_ref[:, h * DV_A:(h + 1) * DV_A]
        return jnp.where(sel, qh, jnp.zeros_like(qh))

    def decode_pages():
        page = k_refs[0].shape[-1]
        s = jnp.concatenate(
            [jnp.concatenate([_dot(head_q(h), k_refs[n][h].astype(BF16).reshape(2 * DK_A, -1))
                              for n in range(npg)], axis=1) for h in range(H_A)], axis=0)
        m_prev = m_sc[...]
        m_new = jnp.maximum(m_prev, jnp.max(s, axis=1, keepdims=True))
        alpha = jnp.exp2(m_prev - m_new)
        p = jnp.exp2(s - m_new)
        l_sc[...] = alpha * l_sc[...] + jnp.sum(p, axis=1, keepdims=True)
        m_sc[...] = m_new
        pb = p.astype(BF16)
        for h in range(H_A):
            acc = alpha[h * rows_h:(h + 1) * rows_h] * acc_sc[h]
            for n in range(npg):
                vh = v_refs[n][pl.ds(h, page, stride=H_A), :]
                acc = acc + _dot(pb[h * rows_h:(h + 1) * rows_h, n * page:(n + 1) * page], vh.astype(BF16))
            acc_sc[h] = acc

    @pl.when(j < jd)
    def _():
        prompt_tile(False)
        decode_pages()

    @pl.when(j == jd)
    def _():
        prompt_tile(True)
        decode_pages()
        acc = acc_p[...]
        o = acc[:, :DV_A] / acc[:, DV_A:]
        lam = _diff_lambda(lq1_ref[...], lk1_ref[...], lq2_ref[...], lk2_ref[...])
        op_ref[...] = _subln_gate(o[:tq], o[tq:], lam, subg_ref[...], gap_ref[...]).astype(BF16)

    @pl.when(j > jd)
    def _():
        decode_pages()

    @pl.when(g == pl.num_programs(3) - 1)
    def _():
        lam = _diff_lambda(lq1_ref[...], lk1_ref[...], lq2_ref[...], lk2_ref[...])
        t_idx = jnp.where(rowi[:, :1] >= n_dec, rowi[:, :1] - n_dec, rowi[:, :1])
        for h in range(H_A):
            hs = slice(h * DV_A, (h + 1) * DV_A)
            qf = head_q(h).astype(F32)
            s_new = []
            for j in range(n_dec):
                sj = jnp.sum(qf * kn_ref[j:j + 1, hs], axis=1, keepdims=True)
                s_new.append(jnp.where(j <= t_idx, sj, -jnp.inf))
            m_prev = m_sc[h * rows_h:(h + 1) * rows_h]
            m_new = m_prev
            for sj in s_new:
                m_new = jnp.maximum(m_new, sj)
            alpha = jnp.exp2(m_prev - m_new)
            l = alpha * l_sc[h * rows_h:(h + 1) * rows_h]
            acc = alpha * acc_sc[h]
            for j, sj in enumerate(s_new):
                pj = jnp.exp2(sj - m_new)
                l = l + pj
                acc = acc + pj * vn_ref[j:j + 1, hs]
            o = acc / l
            o_ref[:, hs] = _subln_gate(o[:n_dec], o[n_dec:], lam, subg_ref[...], ga_ref[:, hs])


def _attn(page_table, q8, k_new, v_new, ga_s, q_p, ktb_p, v_p, ga_p, lq1, lk1, lq2, lk2, subg,
          kt_pages, v_pages, n_seq, seq_len):
    n_d, n_pages = page_table.shape
    n_dec = k_new.shape[1]
    npg = PAGES_PER_STEP
    page = kt_pages.shape[-1]
    tq = ATT_TILE
    tk = KEY_TILE
    nq = seq_len // tq
    nk = seq_len // tk
    per_key = tk // tq
    assert n_d == n_seq * H_A * nq and n_pages == nk * npg
    dec = lambda b, h, i: (b * H_A + h) * nq + i
    key_tile = lambda i, j: jnp.minimum(j, i // per_key)
    vec = lambda w: pl.BlockSpec((1, w), lambda b, h, i, j, pt: (0, 0))
    per_d = lambda r: pl.BlockSpec((None, r, D_ATT), lambda b, h, i, j, pt: (dec(b, h, i), 0, 0))
    q_tile = pl.BlockSpec((tq, DV_A), lambda b, h, i, j, pt: (b * nq + i, h))

    def k_spec(n):
        return pl.BlockSpec((None, H_A, 2, DK_A, page),
                            lambda b, h, i, j, pt: (pt[dec(b, h, i), j * npg + n], 0, 0, 0, 0))

    def v_spec(n):
        return pl.BlockSpec((None, page * H_A, DV_A),
                            lambda b, h, i, j, pt: (pt[dec(b, h, i), j * npg + n], 0, 0))

    in_specs = ([per_d(2 * n_dec), per_d(n_dec), per_d(n_dec), per_d(n_dec),
                 q_tile,
                 pl.BlockSpec((None, None, DV_A, tk), lambda b, h, i, j, pt: (b, key_tile(i, j), h, 0)),
                 pl.BlockSpec((tk, DV_A), lambda b, h, i, j, pt: (b * nk + key_tile(i, j), h)),
                 q_tile,
                 vec(DK_A), vec(DK_A), vec(DK_A), vec(DK_A), vec(DV_A)]
                + [k_spec(n) for n in range(npg)] + [v_spec(n) for n in range(npg)])
    grid_spec = pltpu.PrefetchScalarGridSpec(
        num_scalar_prefetch=1, grid=(n_seq, H_A, nq, nk), in_specs=in_specs,
        out_specs=(per_d(n_dec), q_tile),
        scratch_shapes=[pltpu.VMEM((H_A * 2 * n_dec, 1), F32), pltpu.VMEM((H_A * 2 * n_dec, 1), F32),
                        pltpu.VMEM((H_A, 2 * n_dec, DV_A), F32),
                        pltpu.VMEM((2 * tq, DV_A), BF16), pltpu.VMEM((2 * tq, DV_A), F32),
                        pltpu.VMEM((2 * tq, 2 * DV_A), F32)])
    return pl.pallas_call(
        _attn_kernel,
        out_shape=(jax.ShapeDtypeStruct((n_d, n_dec, D_ATT), F32),
                   jax.ShapeDtypeStruct((n_seq * seq_len, D_ATT), BF16)),
        grid_spec=grid_spec,
        compiler_params=pltpu.CompilerParams(
            dimension_semantics=("arbitrary", "arbitrary", "arbitrary", "arbitrary"),
            vmem_limit_bytes=VMEM_LIMIT),
        name="attn",
    )(page_table, q8, k_new, v_new, ga_s, q_p, ktb_p, v_p, ga_p, lq1, lk1, lq2, lk2, subg,
      *([kt_pages] * npg), *([v_pages] * npg))


def _rwkv_pre_math(rw, prev, mu_ref, w0_ref, wup_ref, a0_ref, aup_ref, kk_ref, ka_ref, rk_ref, bd_ref):
    u = rw + (prev - rw) * mu_ref[...]
    r = u[:, 0:D_RWKV]
    k = u[:, D_RWKV:2 * D_RWKV]
    v = u[:, 2 * D_RWKV:3 * D_RWKV]
    wd = u[:, 3 * D_RWKV:3 * D_RWKV + R_LORA]
    ad = u[:, 3 * D_RWKV + R_LORA:]
    w_log = -jax.nn.softplus(-(w0_ref[...] + _dot(jnp.tanh(wd).astype(BF16), wup_ref[...]))) - 0.5
    log_decay = -jnp.exp(w_log)
    a = jax.nn.sigmoid(a0_ref[...] + _dot(ad.astype(BF16), aup_ref[...]))
    bd = bd_ref[...]
    kk = k * kk_ref[...]
    kk = kk / jnp.maximum(jnp.sqrt(_segsum(kk * kk, bd)), L2_EPS)
    k2 = k * (1.0 + (a - 1.0) * ka_ref[...])
    return r, log_decay, k2, v, -kk, kk * a, _segsum(r * k2 * rk_ref[...], bd) * v


def _rwkv_pre_kernel(seq_len, rw_ref, first_ref, mu_ref, w0_ref, wup_ref, a0_ref, aup_ref,
                     kk_ref, ka_ref, rk_ref, bd_ref,
                     r_out, w_out, k_out, v_out, nkk_out, b_out, bonus_out):
    tm = rw_ref.shape[0]
    rw = rw_ref[...]
    rowi = lax.broadcasted_iota(jnp.int32, (tm, 1), 0)
    prev = jnp.where(rowi % seq_len == 0, first_ref[...], pltpu.roll(rw, 1, axis=0))
    outs = _rwkv_pre_math(rw, prev, mu_ref, w0_ref, wup_ref, a0_ref, aup_ref, kk_ref, ka_ref, rk_ref, bd_ref)
    for ref, val in zip((r_out, w_out, k_out, v_out, nkk_out, b_out, bonus_out), outs):
        ref[...] = val


def _rwkv_pre(rw, first, seq_len, mu, w0, wup, a0, aup, k_k, k_a, r_k, bd):
    rows = rw.shape[0]
    tm = ROW_TILE
    assert tm % seq_len == 0
    const = lambda i: (0, 0)
    row_blk = lambda w: pl.BlockSpec((tm, w), lambda i: (i, 0))
    in_specs = [
        row_blk(SHIFT_W),
        row_blk(SHIFT_W),
        pl.BlockSpec((1, SHIFT_W), const),
        pl.BlockSpec((1, D_RWKV), const), pl.BlockSpec((R_LORA, D_RWKV), const),
        pl.BlockSpec((1, D_RWKV), const), pl.BlockSpec((R_LORA, D_RWKV), const),
        pl.BlockSpec((1, D_RWKV), const), pl.BlockSpec((1, D_RWKV), const), pl.BlockSpec((1, D_RWKV), const),
        pl.BlockSpec((D_RWKV, D_RWKV), const),
    ]
    out = jax.ShapeDtypeStruct((rows, D_RWKV), F32)
    return pl.pallas_call(
        functools.partial(_rwkv_pre_kernel, seq_len),
        out_shape=(out,) * 7, grid=(rows // tm,), in_specs=in_specs, out_specs=(row_blk(D_RWKV),) * 7,
        compiler_params=pltpu.CompilerParams(dimension_semantics=("parallel",), vmem_limit_bytes=VMEM_LIMIT),
        name="rwkv_pre",
    )(rw, first, mu, w0, wup, a0, aup, k_k, k_a, r_k, bd)


def _rwkv_steps_kernel(r_ref, lw_ref, k_ref, v_ref, nkk_ref, b_ref, bonus_ref, s0_ref, gng_ref, gnb_ref,
                       y_ref, s_ref, xt_sc, y_sc):
    n_b = s0_ref.shape[-1]
    n_t = r_ref.shape[0] // n_b
    for a, ref in enumerate((r_ref, lw_ref, k_ref, v_ref, nkk_ref, b_ref)):
        for t in range(n_t):
            xt_sc[a, t] = ref[pl.ds(t, n_b, stride=n_t), :].T
    s_ref[...] = s0_ref[...]

    for t in range(n_t):
        for e in range(2):
            hk = slice(e * HS_R, (e + 1) * HS_R)
            r_e = xt_sc[0, t, hk]
            w_e = jnp.exp(xt_sc[1, t, hk])
            k_e = xt_sc[2, t, hk]
            nk_e = xt_sc[4, t, hk]
            b_e = xt_sc[5, t, hk]

            def rows(g, carry):
                v0 = pl.multiple_of(g * 8, 8)
                v8 = xt_sc[3, t, pl.ds(e * HS_R + v0, 8), :]
                ys = []
                for i in range(8):
                    s = s_ref[e, v0 + i]
                    sa = jnp.sum(s * nk_e, axis=0, keepdims=True)
                    s = s * w_e + sa * b_e + v8[i:i + 1] * k_e
                    s_ref[e, v0 + i] = s
                    ys.append(jnp.sum(s * r_e, axis=0, keepdims=True))
                y_sc[pl.ds(e * HS_R + v0, 8), :] = jnp.concatenate(ys, axis=0)
                return carry

            lax.fori_loop(0, HS_R // 8, rows, 0)

        y = y_sc[...]
        halves = []
        for e in range(2):
            yh = y[e * HS_R:(e + 1) * HS_R]
            d = yh - jnp.mean(yh, axis=0, keepdims=True)
            halves.append(d * lax.rsqrt(jnp.mean(d * d, axis=0, keepdims=True) + GN_EPS))
        y = jnp.concatenate(halves, axis=0) * gng_ref[...] + gnb_ref[...]
        y_ref[pl.ds(t, n_b, stride=n_t), :] = y.T + bonus_ref[pl.ds(t, n_b, stride=n_t), :]


def _rwkv_steps(r, lw, k, v, nkk, b, bonus, s0, gn_g, gn_b, n_seq, seq_len):
    rows = n_seq * seq_len
    pw = 2 * HS_R
    n_pair = H_R // 2
    in_blk = pl.BlockSpec((rows, pw), lambda p: (0, p))
    st_blk = pl.BlockSpec((2, HS_R, HS_R, n_seq), lambda p: (p, 0, 0, 0))
    gn_blk = pl.BlockSpec((None, pw, 1), lambda p: (p, 0, 0))
    return pl.pallas_call(
        _rwkv_steps_kernel,
        out_shape=(jax.ShapeDtypeStruct((rows, D_RWKV), F32), jax.ShapeDtypeStruct(s0.shape, F32)),
        grid=(n_pair,),
        in_specs=[in_blk] * 7 + [st_blk, gn_blk, gn_blk],
        out_specs=(in_blk, st_blk),
        scratch_shapes=[pltpu.VMEM((6, seq_len, pw, n_seq), F32), pltpu.VMEM((pw, n_seq), F32)],
        compiler_params=pltpu.CompilerParams(dimension_semantics=("parallel",), vmem_limit_bytes=VMEM_LIMIT),
        name="rwkv_steps",
    )(r, lw, k, v, nkk, b, bonus, s0, gn_g.reshape(n_pair, pw, 1), gn_b.reshape(n_pair, pw, 1))


def _rwkv_chunk_kernel(rw_ref, mu_ref, w0_ref, wup_ref, a0_ref, aup_ref, kk_ref, ka_ref, rk_ref, bd_ref,
                       gng_ref, gnb_ref, y_ref, s_ref, last_sc):
    nb, C = rw_ref.shape[0], rw_ref.shape[1]
    n_pair = H_R // 2
    pw = 2 * HS_R
    c = pl.program_id(1)

    @pl.when(c == 0)
    def _():
        s_ref[...] = jnp.zeros(s_ref.shape, F32)
        last_sc[...] = jnp.zeros(last_sc.shape, F32)

    rw = rw_ref[...].reshape(nb * C, SHIFT_W)
    first = jnp.concatenate([jnp.broadcast_to(last_sc[b:b + 1, :], (C, SHIFT_W)) for b in range(nb)], axis=0)
    row_in_chunk = lax.broadcasted_iota(jnp.int32, (nb * C, 1), 0) % C
    prev = jnp.where(row_in_chunk == 0, first, pltpu.roll(rw, 1, axis=0))
    for b in range(nb):
        last_sc[b:b + 1, :] = rw[(b + 1) * C - 1:(b + 1) * C, :]
    r_all, lw_all, k_all, v_all, nkk_all, b_all, bonus = _rwkv_pre_math(
        rw, prev, mu_ref, w0_ref, wup_ref, a0_ref, aup_ref, kk_ref, ka_ref, rk_ref, bd_ref)

    lane = lax.broadcasted_iota(jnp.int32, (C, pw), 1)
    t_idx = lax.broadcasted_iota(jnp.int32, (C, pw), 0)
    head0 = lane < HS_R
    s_idx = lane % C
    strict = t_idx > s_idx
    incl = t_idx >= s_idx
    eye = jnp.where(t_idx == s_idx, 1.0, 0.0).astype(F32)
    tri = jnp.where(lax.broadcasted_iota(jnp.int32, (C, C), 0) >= lax.broadcasted_iota(jnp.int32, (C, C), 1),
                    1.0, 0.0).astype(BF16)
    same_head = (lax.broadcasted_iota(jnp.int32, (pw, pw), 0) // HS_R
                 == lax.broadcasted_iota(jnp.int32, (pw, pw), 1) // HS_R)

    def stack(x):
        return jnp.concatenate([jnp.where(head0, x, 0.0), jnp.where(head0, 0.0, x)], axis=0).astype(BF16)

    def per_head(coef, x):
        return _dot(coef.astype(BF16), stack(x))

    def head_mean(x):
        return jnp.where(head0, jnp.sum(jnp.where(head0, x, 0.0), axis=1, keepdims=True),
                         jnp.sum(jnp.where(head0, 0.0, x), axis=1, keepdims=True)) * (1.0 / HS_R)

    units = [(b, p) for b in range(nb) for p in range(n_pair)]
    ls = lambda p: slice(p * pw, (p + 1) * pw)
    take = lambda arr: [arr[b * C:(b + 1) * C, ls(p)] for b, p in units]
    lw = take(lw_all)
    lw_hi = [x.astype(BF16) for x in lw]
    cum = [_dot(tri, h) + _dot(tri, (x - h.astype(F32)).astype(BF16)) for x, h in zip(lw, lw_hi)]
    p_in = [jnp.exp(x) for x in cum]
    p_inv = [jnp.exp(-x) for x in cum]
    nt = [x * jnp.exp(cm - l) for x, cm, l in zip(take(nkk_all), cum, lw)]
    bt = [x * pi for x, pi in zip(take(b_all), p_inv)]
    kt = [x * pi for x, pi in zip(take(k_all), p_inv)]
    rt = [x * pp for x, pp in zip(take(r_all), p_in)]
    vv = take(v_all)

    lhs = [jnp.concatenate([n_, r_], axis=0).astype(BF16) for n_, r_ in zip(nt, rt)]
    coef = [lax.dot_general(l_, jnp.concatenate([stack(b_), stack(k_)], axis=0), _NT,
                            preferred_element_type=F32) for l_, b_, k_ in zip(lhs, bt, kt)]
    a_pow = [jnp.where(strict, cf[:C, :pw], 0.0) for cf in coef]
    b_m = [jnp.where(strict, cf[:C, pw:], 0.0) for cf in coef]
    ab_m = [jnp.where(incl, cf[C:, :pw], 0.0) for cf in coef]
    ak_m = [jnp.where(incl, cf[C:, pw:], 0.0) for cf in coef]

    inv = [eye + a for a in a_pow]
    for _ in range(C.bit_length() - 2):
        a_pow = [per_head(a, a) for a in a_pow]
        inv = [per_head(i_, eye + a) for i_, a in zip(inv, a_pow)]

    v_stack = [stack(x) for x in vv]
    bv = [_dot(m.astype(BF16), vs) for m, vs in zip(b_m, v_stack)]
    akv = [_dot(m.astype(BF16), vs) for m, vs in zip(ak_m, v_stack)]

    st = [s_ref[b, p] for b, p in units]
    ns_rs = [lax.dot_general(l_, s_.astype(BF16), _NT, preferred_element_type=F32)
             for l_, s_ in zip(lhs, st)]
    u = [per_head(i_, x[:C] + bv_) for i_, x, bv_ in zip(inv, ns_rs, bv)]
    y = [x[C:] + per_head(m, u_) + akv_ for x, m, u_, akv_ in zip(ns_rs, ab_m, u, akv)]

    for i, (b, p) in enumerate(units):
        uv_t = jnp.concatenate([u[i], vv[i]], axis=0).T.astype(BF16)
        upd = _dot(uv_t, jnp.concatenate([bt[i], kt[i]], axis=0).astype(BF16))
        s_ref[b, p] = jnp.where(same_head, (st[i] + upd) * p_in[i][C - 1:C, :], 0.0)

    for i, (b, p) in enumerate(units):
        d = y[i] - head_mean(y[i])
        var = head_mean(d * d)
        y_ref[b, :, ls(p)] = (d * lax.rsqrt(var + GN_EPS) * gng_ref[:, ls(p)] + gnb_ref[:, ls(p)]
                              + bonus[b * C:(b + 1) * C, ls(p)])


def _rwkv_chunk(rw, mu, w0, wup, a0, aup, k_k, k_a, r_k, bd, gn_g, gn_b, n_seq, seq_len):
    nb = CHUNK_SEQS
    C = CHUNK
    n_pair = H_R // 2
    const = lambda g, c: (0, 0)
    vec = lambda w: pl.BlockSpec((1, w), const)
    out_blk = pl.BlockSpec((nb, C, D_RWKV), lambda g, c: (g, c, 0))
    in_specs = [pl.BlockSpec((nb, C, SHIFT_W), lambda g, c: (g, c, 0)), vec(SHIFT_W),
                vec(D_RWKV), pl.BlockSpec((R_LORA, D_RWKV), const),
                vec(D_RWKV), pl.BlockSpec((R_LORA, D_RWKV), const),
                vec(D_RWKV), vec(D_RWKV), vec(D_RWKV), pl.BlockSpec((D_RWKV, D_RWKV), const),
                vec(D_RWKV), vec(D_RWKV)]
    seq3 = jax.ShapeDtypeStruct((n_seq, seq_len, D_RWKV), F32)
    return pl.pallas_call(
        _rwkv_chunk_kernel,
        out_shape=(seq3, jax.ShapeDtypeStruct((n_seq, n_pair, 2 * HS_R, 2 * HS_R), F32)),
        grid=(n_seq // nb, seq_len // C),
        in_specs=in_specs,
        out_specs=(out_blk, pl.BlockSpec((nb, n_pair, 2 * HS_R, 2 * HS_R), lambda g, c: (g, 0, 0, 0))),
        scratch_shapes=[pltpu.VMEM((nb, SHIFT_W), F32)],
        compiler_params=pltpu.CompilerParams(dimension_semantics=("parallel", "arbitrary"),
                                             vmem_limit_bytes=VMEM_LIMIT),
        name="rwkv_chunk",
    )(rw.reshape(n_seq, seq_len, SHIFT_W), mu, w0, wup, a0, aup, k_k, k_a, r_k, bd, gn_g[None], gn_b[None])


def _out_proj_kernel(oa_ref, yb_ref, gr_ref, x_ref, wo_ref, y_ref):
    mix_r = (yb_ref[...] * gr_ref[...]).astype(BF16)
    y_ref[...] = (x_ref[...] + _dot(oa_ref[...].astype(BF16), wo_ref[0:D_ATT, :])
                  + _dot(mix_r, wo_ref[D_ATT:, :]))


def _out_proj(oa, yb, gr, x, wo):
    rows = x.shape[0]
    tm = OUT_TILE
    row_blk = lambda w: pl.BlockSpec((tm, w), lambda i: (i, 0))
    return pl.pallas_call(
        _out_proj_kernel, out_shape=jax.ShapeDtypeStruct((rows, D_MODEL), F32), grid=(rows // tm,),
        in_specs=[row_blk(D_ATT), row_blk(D_RWKV), row_blk(D_RWKV), row_blk(D_MODEL),
                  pl.BlockSpec(wo.shape, lambda i: (0, 0))],
        out_specs=row_blk(D_MODEL),
        compiler_params=pltpu.CompilerParams(dimension_semantics=("parallel",), vmem_limit_bytes=VMEM_LIMIT),
        name="out_proj",
    )(oa, yb, gr, x, wo)


def _rope_tables(pos):
    half = DK_A // 2
    inv = 1.0 / (ROPE_THETA ** (jnp.arange(half, dtype=F32) / half))
    ang = inv[:, None] * pos.astype(F32)[None, :]
    return jnp.cos(ang), jnp.sin(ang)


def kernel(x_prompt, x_sample, cache_k, cache_v, state_wkv, state_shift, page_table, ln_g, w_in, q_norm_g,
           k_norm_g, lambda_q1, lambda_k1, lambda_q2, lambda_k2, subln_g, shift_mu, w0, w_lora_up, a0,
           a_lora_up, k_k, k_a, r_k, gn_g, gn_b, w_out):
    n_b, seq = x_prompt.shape[:2]
    n_d, n_dec = x_sample.shape[:2]
    page = cache_k.shape[2]
    past = page_table.shape[1] * page
    layer = 0

    w_in_b = w_in[layer].astype(BF16)
    wq_t = w_in_b[:, 0:D_ATT].T
    wk_t = w_in_b[:, D_ATT:2 * D_ATT].T
    w_rest = w_in_b[:, 2 * D_ATT:]
    wo = w_out[layer].astype(BF16)
    wup = w_lora_up[layer].astype(BF16)
    aup = a_lora_up[layer].astype(BF16)
    qg = q_norm_g[layer].reshape(DK_A, 1)
    kg = k_norm_g[layer].reshape(DK_A, 1)
    seg = jnp.arange(D_RWKV) // HS_R
    bd = (seg[:, None] == seg[None, :]).astype(BF16)
    lam_vecs = (lambda_q1[layer][None], lambda_k1[layer][None], lambda_q2[layer][None], lambda_k2[layer][None])
    subg = subln_g[layer][None]
    row = lambda a: a[layer][None]

    pre_params = (row(shift_mu), row(w0), wup, row(a0), aup, row(k_k), row(k_a), row(r_k), bd)

    def mix_long(rw, n_seq, seq_len):
        yb, s_fin = _rwkv_chunk(rw, *pre_params, gn_g[layer], gn_b[layer], n_seq, seq_len)
        s6 = s_fin.reshape(n_seq, H_R // 2, 2, HS_R, 2, HS_R)
        s_heads = jnp.stack([s6[:, :, e, :, e, :] for e in range(2)], axis=2)
        return yb.reshape(n_seq * seq_len, D_RWKV), s_heads.reshape(n_seq, H_R, HS_R, HS_R)

    def mix_short(rw, first, s0, n_seq, seq_len):
        r, lw, k2, vv, nkk, bb, bonus = _rwkv_pre(rw, first, seq_len, *pre_params)
        yb, s_fin = _rwkv_steps(r, lw, k2, vv, nkk, bb, bonus, s0.transpose(1, 2, 3, 0), gn_g[layer],
                                gn_b[layer], n_seq, seq_len)
        return yb, s_fin.transpose(3, 0, 1, 2)

    def project(x, pos, rows, proj_seqs):
        cos_t, sin_t = _rope_tables(pos)
        return _proj(x.reshape(rows, D_MODEL), ln_g[layer][None], wq_t, wk_t, w_rest, cos_t, sin_t, qg, kg,
                     proj_seqs, rows // proj_seqs)

    rows_p = n_b * seq
    rows_s = n_d * n_dec
    q_p, kt_p, ktb_p, v_p, vb_p, ga_p, rw_p, gr_p = project(x_prompt, jnp.tile(jnp.arange(seq), n_b), rows_p, n_b)
    q_s, kt_s, _, v_s, _, ga_s, rw_s, gr_s = project(x_sample, jnp.tile(past + jnp.arange(n_dec), n_d), rows_s, 1)

    kt_pages = cache_k[layer].transpose(0, 2, 3, 4, 1)
    v_pages = cache_v[layer].reshape(-1, page * H_A, DV_A)
    q3 = q_s.reshape(n_d, n_dec, D_ATT)
    k_new = kt_s.reshape(D_ATT, rows_s).T.reshape(n_d, n_dec, D_ATT)
    oa_s, oa_p = _attn(page_table, jnp.concatenate([q3, q3], axis=1), k_new, v_s.reshape(n_d, n_dec, D_ATT),
                       ga_s.reshape(n_d, n_dec, D_ATT), q_p, ktb_p, vb_p, ga_p, *lam_vecs, subg,
                       kt_pages, v_pages, n_b, seq)

    yb_p, s_p = mix_long(rw_p, n_b, seq)
    yp = _out_proj(oa_p, yb_p, gr_p, x_prompt.reshape(rows_p, D_MODEL), wo)
    first_s = jnp.repeat(state_shift[layer], n_dec, axis=0)
    yb_s, s_s = mix_short(rw_s, first_s, state_wkv[layer], n_d, n_dec)
    ys = _out_proj(oa_s.reshape(rows_s, D_ATT), yb_s, gr_s, x_sample.reshape(rows_s, D_MODEL), wo)

    k_prompt = kt_p.reshape(n_b, H_A, 2, DK_A, seq).transpose(0, 4, 1, 2, 3)
    k_sample = kt_s.reshape(H_A, 2, DK_A, n_d, n_dec).transpose(3, 4, 0, 1, 2)
    return (yp.reshape(n_b, seq, D_MODEL), ys.reshape(n_d, n_dec, D_MODEL),
            k_prompt[None], v_p.reshape(n_b, seq, H_A, DV_A)[None], s_p[None],
            rw_p.reshape(n_b, seq, SHIFT_W)[:, -1][None],
            k_sample[None], v_s.reshape(n_d, n_dec, H_A, DV_A)[None], s_s[None],
            rw_s.reshape(n_d, n_dec, SHIFT_W)[:, -1][None])
```

```python
import functools
import math

import jax
import jax.numpy as jnp
import numpy as np
from jax import lax
from jax.experimental import pallas as pl
from jax.experimental.pallas import tpu as pltpu

F32 = jnp.float32
BF16 = jnp.bfloat16

D_MODEL = 1024
D_ATT = 512
D_RWKV = 512
DK_A = 64
H_A = 4
DV_A = 128
HS_R = 64
H_R = 8
R_LORA = 64
SHIFT_W = 3 * D_RWKV + 2 * R_LORA
ROPE_THETA = 10000.0
RMS_EPS = 1e-6
GN_EPS = 64e-5
L2_EPS = 1e-12
LAM_INIT = 0.8 - 0.6 * math.exp(-0.3 * 0)

ROW_TILE = 512
OUT_TILE = 512
ATT_TILE = 512
KEY_TILE = 1024
LOG2E = math.log2(math.e)
CHUNK = HS_R
CHUNK_SEQS = 4
PAGES_PER_STEP = 16
VMEM_LIMIT = 56 * 1024 * 1024

_NT = (((1,), (1,)), ((), ()))


def _dot(a, b):
    return jnp.dot(a, b, preferred_element_type=F32)


def _segsum(x, bd):
    hi = x.astype(BF16)
    lo = (x - hi.astype(F32)).astype(BF16)
    return _dot(hi, bd) + _dot(lo, bd)


def _diff_lambda(lq1, lk1, lq2, lk2):
    return (jnp.exp(jnp.sum(lq1 * lk1, axis=1, keepdims=True))
            - jnp.exp(jnp.sum(lq2 * lk2, axis=1, keepdims=True)) + LAM_INIT)


def _subln_gate(o0, o1, lam, subg, gate):
    y = o0 - lam * o1
    y = y * lax.rsqrt(jnp.mean(y * y, axis=-1, keepdims=True) + RMS_EPS) * subg
    return (y * (1.0 - LAM_INIT)) * gate


def _proj_kernel(x_ref, lng_ref, wq_ref, wk_ref, wr_ref, cos_ref, sin_ref, qg_ref, kg_ref,
                 q_ref, kt_ref, ktb_ref, v_ref, vb_ref, ga_ref, rw_ref, gr_ref):
    tm = x_ref.shape[0]
    x = x_ref[...]
    h = x * lax.rsqrt(jnp.mean(x * x, axis=-1, keepdims=True) + RMS_EPS) * lng_ref[...]
    h = h.astype(BF16)
    cos = cos_ref[...][None]
    sin = sin_ref[...][None]

    def norm_rope(t, g):
        t3 = t.reshape(2 * H_A, DK_A, tm)
        t3 = t3 * lax.rsqrt(jnp.mean(t3 * t3, axis=1, keepdims=True) + RMS_EPS) * g[None]
        x1 = t3[:, :DK_A // 2]
        x2 = t3[:, DK_A // 2:]
        o = jnp.concatenate([x1 * cos - x2 * sin, x2 * cos + x1 * sin], axis=1)
        return o.reshape(D_ATT, tm)

    qt = lax.dot_general(wq_ref[...], h, _NT, preferred_element_type=F32)
    qt = norm_rope(qt, qg_ref[...]) * (DK_A ** -0.5 * LOG2E)
    q_ref[...] = qt.T.astype(BF16)
    kt = lax.dot_general(wk_ref[...], h, _NT, preferred_element_type=F32)
    kt = norm_rope(kt, kg_ref[...])
    kt_ref[...] = kt
    ktb_ref[...] = kt.astype(BF16)

    v = _dot(h, wr_ref[:, 0:D_ATT])
    for hd in range(H_A):
        v_ref[:, hd, :] = v[:, hd * DV_A:(hd + 1) * DV_A]
    vb_ref[...] = v.astype(BF16)
    za = _dot(h, wr_ref[:, D_ATT:2 * D_ATT])
    ga_ref[...] = za * jax.nn.sigmoid(za)
    rw_ref[...] = _dot(h, wr_ref[:, 2 * D_ATT:2 * D_ATT + SHIFT_W])
    zr = _dot(h, wr_ref[:, 2 * D_ATT + SHIFT_W:])
    gr_ref[...] = zr * jax.nn.sigmoid(zr)


def _proj(x, ln_g, wq_t, wk_t, w_rest, cos_t, sin_t, qg, kg, n_seq, seq_len):
    rows = x.shape[0]
    tm = ROW_TILE
    nsi = seq_len // tm
    tk = min(KEY_TILE, seq_len)
    per_tk = tk // tm
    const = lambda i: (0, 0)
    row_blk = lambda w: pl.BlockSpec((tm, w), lambda i: (i, 0))
    out_shape = (
        jax.ShapeDtypeStruct((rows, D_ATT), BF16),
        jax.ShapeDtypeStruct((n_seq, D_ATT, seq_len), F32),
        jax.ShapeDtypeStruct((n_seq, seq_len // tk, D_ATT, tk), BF16),
        jax.ShapeDtypeStruct((rows, H_A, DV_A), F32),
        jax.ShapeDtypeStruct((rows, D_ATT), BF16),
        jax.ShapeDtypeStruct((rows, D_ATT), F32),
        jax.ShapeDtypeStruct((rows, SHIFT_W), F32),
        jax.ShapeDtypeStruct((rows, D_RWKV), F32),
    )
    out_specs = (
        row_blk(D_ATT),
        pl.BlockSpec((None, D_ATT, tm), lambda i: (i // nsi, 0, i % nsi)),
        pl.BlockSpec((None, None, D_ATT, tm), lambda i: (i // nsi, (i % nsi) // per_tk, 0, i % per_tk)),
        pl.BlockSpec((tm, H_A, DV_A), lambda i: (i, 0, 0)),
        row_blk(D_ATT), row_blk(D_ATT), row_blk(SHIFT_W), row_blk(D_RWKV),
    )
    in_specs = [
        row_blk(D_MODEL),
        pl.BlockSpec((1, D_MODEL), const),
        pl.BlockSpec(wq_t.shape, const),
        pl.BlockSpec(wk_t.shape, const),
        pl.BlockSpec(w_rest.shape, const),
        pl.BlockSpec((DK_A // 2, tm), lambda i: (0, i)),
        pl.BlockSpec((DK_A // 2, tm), lambda i: (0, i)),
        pl.BlockSpec((DK_A, 1), const),
        pl.BlockSpec((DK_A, 1), const),
    ]
    return pl.pallas_call(
        _proj_kernel, out_shape=out_shape, grid=(rows // tm,), in_specs=in_specs, out_specs=out_specs,
        compiler_params=pltpu.CompilerParams(dimension_semantics=("parallel",), vmem_limit_bytes=VMEM_LIMIT),
        name="proj",
    )(x, ln_g, wq_t, wk_t, w_rest, cos_t, sin_t, qg, kg)


def _attn_kernel(pt_ref, q8_ref, kn_ref, vn_ref, gas_ref, qp_ref, ktp_ref, vp_ref, gap_ref,
                 lq1_ref, lk1_ref, lq2_ref, lk2_ref, subg_ref, *refs):
    npg = PAGES_PER_STEP
    k_refs = refs[:npg]
    v_refs = refs[npg:2 * npg]
    os_ref, op_ref, m_sc, l_sc, acc_sc, qs_sc, m_p, acc_p = refs[2 * npg:]
    i = pl.program_id(2)
    j = pl.program_id(3)
    g = j
    n_dec = q8_ref.shape[0] // 2
    tq = qp_ref.shape[0]
    tk = ktp_ref.shape[1]

    @pl.when(j == 0)
    def _():
        m_sc[...] = jnp.full(m_sc.shape, -jnp.inf, F32)
        l_sc[...] = jnp.zeros(l_sc.shape, F32)
        acc_sc[...] = jnp.zeros(acc_sc.shape, F32)
        q = qp_ref[...]
        lane_q = lax.broadcasted_iota(jnp.int32, q.shape, 1)
        zero = jnp.zeros_like(q)
        qs_sc[...] = jnp.concatenate([jnp.where(lane_q < DK_A, q, zero), jnp.where(lane_q >= DK_A, q, zero)],
                                     axis=0)
        m_p[...] = jnp.full(m_p.shape, -jnp.inf, F32)
        acc_p[...] = jnp.zeros(acc_p.shape, F32)

    per_key = tk // tq
    jd = i // per_key

    def prompt_tile(diagonal):
        sp = _dot(qs_sc[...], ktp_ref[...])
        if diagonal:
            row_p = lax.broadcasted_iota(jnp.int32, sp.shape, 0)
            col_p = lax.broadcasted_iota(jnp.int32, sp.shape, 1)
            q_off = (i % per_key) * tq
            sp = jnp.where(col_p <= jnp.where(row_p >= tq, row_p - tq, row_p) + q_off, sp, -jnp.inf)
        mp_prev = m_p[...]
        mp_new = jnp.maximum(mp_prev, jnp.max(sp, axis=1, keepdims=True))
        alpha_p = jnp.exp2(mp_prev - mp_new)
        pp = jnp.exp2(sp - jnp.tile(mp_new, (1, tk // DV_A)))
        v_ext = jnp.concatenate([vp_ref[...], jnp.ones((tk, DV_A), BF16)], axis=1)
        acc_p[...] = jnp.tile(alpha_p, (1, 2)) * acc_p[...] + _dot(pp.astype(BF16), v_ext)
        m_p[...] = mp_new

    o_ref, ga_ref = os_ref, gas_ref
    rowi = lax.broadcasted_iota(jnp.int32, (2 * n_dec, DV_A), 0)
    lane = lax.broadcasted_iota(jnp.int32, (2 * n_dec, DV_A), 1)
    sel = ((rowi < n_dec) & (lane < DK_A)) | ((rowi >= n_dec) & (lane >= DK_A))
    rows_h = 2 * n_dec

    def head_q(h):
        qh = q8_ref[:, h * DV_A:(h + 1) * DV_A]
        return jnp.where(sel, qh, jnp.zeros_like(qh))

    def decode_pages():
        page = k_refs[0].shape[-1]
        s = jnp.concatenate(
            [jnp.concatenate([_dot(head_q(h), k_refs[n][h * 2 * DK_A:(h + 1) * 2 * DK_A, :].astype(BF16))
                              for n in range(npg)], axis=1) for h in range(H_A)], axis=0)
        m_prev = m_sc[...]
        m_new = jnp.maximum(m_prev, jnp.max(s, axis=1, keepdims=True))
        alpha = jnp.exp2(m_prev - m_new)
        p = jnp.exp2(s - m_new)
        l_sc[...] = alpha * l_sc[...] + jnp.sum(p, axis=1, keepdims=True)
        m_sc[...] = m_new
        pb = p.astype(BF16)
        for h in range(H_A):
            acc = alpha[h * rows_h:(h + 1) * rows_h] * acc_sc[h]
            for n in range(npg):
                vh = v_refs[n][pl.ds(h, page, stride=H_A), :]
                acc = acc + _dot(pb[h * rows_h:(h + 1) * rows_h, n * page:(n + 1) * page], vh.astype(BF16))
            acc_sc[h] = acc

    @pl.when(j < jd)
    def _():
        prompt_tile(False)
        decode_pages()

    @pl.when(j == jd)
    def _():
        prompt_tile(True)
        decode_pages()
        acc = acc_p[...]
        o = acc[:, :DV_A] / acc[:, DV_A:]
        lam = _diff_lambda(lq1_ref[...], lk1_ref[...], lq2_ref[...], lk2_ref[...])
        op_ref[...] = _subln_gate(o[:tq], o[tq:], lam, subg_ref[...], gap_ref[...]).astype(BF16)

    @pl.when(j > jd)
    def _():
        decode_pages()

    @pl.when(g == pl.num_programs(3) - 1)
    def _():
        lam = _diff_lambda(lq1_ref[...], lk1_ref[...], lq2_ref[...], lk2_ref[...])
        t_idx = jnp.where(rowi[:, :1] >= n_dec, rowi[:, :1] - n_dec, rowi[:, :1])
        for h in range(H_A):
            hs = slice(h * DV_A, (h + 1) * DV_A)
            qf = head_q(h).astype(F32)
            s_new = []
            for j in range(n_dec):
                sj = jnp.sum(qf * kn_ref[j:j + 1, hs], axis=1, keepdims=True)
                s_new.append(jnp.where(j <= t_idx, sj, -jnp.inf))
            m_prev = m_sc[h * rows_h:(h + 1) * rows_h]
            m_new = m_prev
            for sj in s_new:
                m_new = jnp.maximum(m_new, sj)
            alpha = jnp.exp2(m_prev - m_new)
            l = alpha * l_sc[h * rows_h:(h + 1) * rows_h]
            acc = alpha * acc_sc[h]
            for j, sj in enumerate(s_new):
                pj = jnp.exp2(sj - m_new)
                l = l + pj
                acc = acc + pj * vn_ref[j:j + 1, hs]
            o = acc / l
            o_ref[:, hs] = _subln_gate(o[:n_dec], o[n_dec:], lam, subg_ref[...], ga_ref[:, hs])


def _attn(page_table, q8, k_new, v_new, ga_s, q_p, ktb_p, v_p, ga_p, lq1, lk1, lq2, lk2, subg,
          kt_pages, v_pages, n_seq, seq_len):
    n_d, n_pages = page_table.shape
    n_dec = k_new.shape[1]
    npg = PAGES_PER_STEP
    page = kt_pages.shape[-1]
    tq = ATT_TILE
    tk = KEY_TILE
    nq = seq_len // tq
    nk = seq_len // tk
    per_key = tk // tq
    assert n_d == n_seq * H_A * nq and n_pages == nk * npg
    dec = lambda b, h, i: (b * H_A + h) * nq + i
    key_tile = lambda i, j: jnp.minimum(j, i // per_key)
    vec = lambda w: pl.BlockSpec((1, w), lambda b, h, i, j, pt: (0, 0))
    per_d = lambda r: pl.BlockSpec((None, r, D_ATT), lambda b, h, i, j, pt: (dec(b, h, i), 0, 0))
    q_tile = pl.BlockSpec((tq, DV_A), lambda b, h, i, j, pt: (b * nq + i, h))

    def k_spec(n):
        return pl.BlockSpec((None, H_A * 2 * DK_A, page),
                            lambda b, h, i, j, pt: (pt[dec(b, h, i), j * npg + n], 0, 0))

    def v_spec(n):
        return pl.BlockSpec((None, page * H_A, DV_A),
                            lambda b, h, i, j, pt: (pt[dec(b, h, i), j * npg + n], 0, 0))

    in_specs = ([per_d(2 * n_dec), per_d(n_dec), per_d(n_dec), per_d(n_dec),
                 q_tile,
                 pl.BlockSpec((None, None, DV_A, tk), lambda b, h, i, j, pt: (b, key_tile(i, j), h, 0)),
                 pl.BlockSpec((tk, DV_A), lambda b, h, i, j, pt: (b * nk + key_tile(i, j), h)),
                 q_tile,
                 vec(DK_A), vec(DK_A), vec(DK_A), vec(DK_A), vec(DV_A)]
                + [k_spec(n) for n in range(npg)] + [v_spec(n) for n in range(npg)])
    grid_spec = pltpu.PrefetchScalarGridSpec(
        num_scalar_prefetch=1, grid=(n_seq, H_A, nq, nk), in_specs=in_specs,
        out_specs=(per_d(n_dec), q_tile),
        scratch_shapes=[pltpu.VMEM((H_A * 2 * n_dec, 1), F32), pltpu.VMEM((H_A * 2 * n_dec, 1), F32),
                        pltpu.VMEM((H_A, 2 * n_dec, DV_A), F32),
                        pltpu.VMEM((2 * tq, DV_A), BF16), pltpu.VMEM((2 * tq, DV_A), F32),
                        pltpu.VMEM((2 * tq, 2 * DV_A), F32)])
    return pl.pallas_call(
        _attn_kernel,
        out_shape=(jax.ShapeDtypeStruct((n_d, n_dec, D_ATT), F32),
                   jax.ShapeDtypeStruct((n_seq * seq_len, D_ATT), BF16)),
        grid_spec=grid_spec,
        compiler_params=pltpu.CompilerParams(
            dimension_semantics=("arbitrary", "arbitrary", "arbitrary", "arbitrary"),
            vmem_limit_bytes=VMEM_LIMIT),
        name="attn",
    )(page_table, q8, k_new, v_new, ga_s, q_p, ktb_p, v_p, ga_p, lq1, lk1, lq2, lk2, subg,
      *([kt_pages] * npg), *([v_pages] * npg))


def _rwkv_pre_math(rw, prev, mu_ref, w0_ref, wup_ref, a0_ref, aup_ref, kk_ref, ka_ref, rk_ref, bd_ref):
    u = rw + (prev - rw) * mu_ref[...]
    r = u[:, 0:D_RWKV]
    k = u[:, D_RWKV:2 * D_RWKV]
    v = u[:, 2 * D_RWKV:3 * D_RWKV]
    wd = u[:, 3 * D_RWKV:3 * D_RWKV + R_LORA]
    ad = u[:, 3 * D_RWKV + R_LORA:]
    w_log = -jax.nn.softplus(-(w0_ref[...] + _dot(jnp.tanh(wd).astype(BF16), wup_ref[...]))) - 0.5
    log_decay = -jnp.exp(w_log)
    a = jax.nn.sigmoid(a0_ref[...] + _dot(ad.astype(BF16), aup_ref[...]))
    bd = bd_ref[...]
    kk = k * kk_ref[...]
    kk = kk / jnp.maximum(jnp.sqrt(_segsum(kk * kk, bd)), L2_EPS)
    k2 = k * (1.0 + (a - 1.0) * ka_ref[...])
    return r, log_decay, k2, v, -kk, kk * a, _segsum(r * k2 * rk_ref[...], bd) * v


def _rwkv_pre_kernel(seq_len, rw_ref, first_ref, mu_ref, w0_ref, wup_ref, a0_ref, aup_ref,
                     kk_ref, ka_ref, rk_ref, bd_ref,
                     r_out, w_out, k_out, v_out, nkk_out, b_out, bonus_out):
    tm = rw_ref.shape[0]
    rw = rw_ref[...]
    rowi = lax.broadcasted_iota(jnp.int32, (tm, 1), 0)
    prev = jnp.where(rowi % seq_len == 0, first_ref[...], pltpu.roll(rw, 1, axis=0))
    outs = _rwkv_pre_math(rw, prev, mu_ref, w0_ref, wup_ref, a0_ref, aup_ref, kk_ref, ka_ref, rk_ref, bd_ref)
    for ref, val in zip((r_out, w_out, k_out, v_out, nkk_out, b_out, bonus_out), outs):
        ref[...] = val


def _rwkv_pre(rw, first, seq_len, mu, w0, wup, a0, aup, k_k, k_a, r_k, bd):
    rows = rw.shape[0]
    tm = ROW_TILE
    assert tm % seq_len == 0
    const = lambda i: (0, 0)
    row_blk = lambda w: pl.BlockSpec((tm, w), lambda i: (i, 0))
    in_specs = [
        row_blk(SHIFT_W),
        row_blk(SHIFT_W),
        pl.BlockSpec((1, SHIFT_W), const),
        pl.BlockSpec((1, D_RWKV), const), pl.BlockSpec((R_LORA, D_RWKV), const),
        pl.BlockSpec((1, D_RWKV), const), pl.BlockSpec((R_LORA, D_RWKV), const),
        pl.BlockSpec((1, D_RWKV), const), pl.BlockSpec((1, D_RWKV), const), pl.BlockSpec((1, D_RWKV), const),
        pl.BlockSpec((D_RWKV, D_RWKV), const),
    ]
    out = jax.ShapeDtypeStruct((rows, D_RWKV), F32)
    return pl.pallas_call(
        functools.partial(_rwkv_pre_kernel, seq_len),
        out_shape=(out,) * 7, grid=(rows // tm,), in_specs=in_specs, out_specs=(row_blk(D_RWKV),) * 7,
        compiler_params=pltpu.CompilerParams(dimension_semantics=("parallel",), vmem_limit_bytes=VMEM_LIMIT),
        name="rwkv_pre",
    )(rw, first, mu, w0, wup, a0, aup, k_k, k_a, r_k, bd)


def _rwkv_steps_kernel(r_ref, lw_ref, k_ref, v_ref, nkk_ref, b_ref, bonus_ref, s0_ref, gng_ref, gnb_ref,
                       y_ref, s_ref, xt_sc, y_sc):
    n_b = s0_ref.shape[-1]
    n_t = r_ref.shape[0] // n_b
    for a, ref in enumerate((r_ref, lw_ref, k_ref, v_ref, nkk_ref, b_ref)):
        for t in range(n_t):
            xt_sc[a, t] = ref[pl.ds(t, n_b, stride=n_t), :].T
    s_ref[...] = s0_ref[...]

    for t in range(n_t):
        for e in range(2):
            hk = slice(e * HS_R, (e + 1) * HS_R)
            r_e = xt_sc[0, t, hk]
            w_e = jnp.exp(xt_sc[1, t, hk])
            k_e = xt_sc[2, t, hk]
            nk_e = xt_sc[4, t, hk]
            b_e = xt_sc[5, t, hk]

            def rows(g, carry):
                v0 = pl.multiple_of(g * 8, 8)
                v8 = xt_sc[3, t, pl.ds(e * HS_R + v0, 8), :]
                ys = []
                for i in range(8):
                    s = s_ref[e, v0 + i]
                    sa = jnp.sum(s * nk_e, axis=0, keepdims=True)
                    s = s * w_e + sa * b_e + v8[i:i + 1] * k_e
                    s_ref[e, v0 + i] = s
                    ys.append(jnp.sum(s * r_e, axis=0, keepdims=True))
                y_sc[pl.ds(e * HS_R + v0, 8), :] = jnp.concatenate(ys, axis=0)
                return carry

            lax.fori_loop(0, HS_R // 8, rows, 0)

        y = y_sc[...]
        halves = []
        for e in range(2):
            yh = y[e * HS_R:(e + 1) * HS_R]
            d = yh - jnp.mean(yh, axis=0, keepdims=True)
            halves.append(d * lax.rsqrt(jnp.mean(d * d, axis=0, keepdims=True) + GN_EPS))
        y = jnp.concatenate(halves, axis=0) * gng_ref[...] + gnb_ref[...]
        y_ref[pl.ds(t, n_b, stride=n_t), :] = y.T + bonus_ref[pl.ds(t, n_b, stride=n_t), :]


def _rwkv_steps(r, lw, k, v, nkk, b, bonus, s0, gn_g, gn_b, n_seq, seq_len):
    rows = n_seq * seq_len
    pw = 2 * HS_R
    n_pair = H_R // 2
    in_blk = pl.BlockSpec((rows, pw), lambda p: (0, p))
    st_blk = pl.BlockSpec((2, HS_R, HS_R, n_seq), lambda p: (p, 0, 0, 0))
    gn_blk = pl.BlockSpec((None, pw, 1), lambda p: (p, 0, 0))
    return pl.pallas_call(
        _rwkv_steps_kernel,
        out_shape=(jax.ShapeDtypeStruct((rows, D_RWKV), F32), jax.ShapeDtypeStruct(s0.shape, F32)),
        grid=(n_pair,),
        in_specs=[in_blk] * 7 + [st_blk, gn_blk, gn_blk],
        out_specs=(in_blk, st_blk),
        scratch_shapes=[pltpu.VMEM((6, seq_len, pw, n_seq), F32), pltpu.VMEM((pw, n_seq), F32)],
        compiler_params=pltpu.CompilerParams(dimension_semantics=("parallel",), vmem_limit_bytes=VMEM_LIMIT),
        name="rwkv_steps",
    )(r, lw, k, v, nkk, b, bonus, s0, gn_g.reshape(n_pair, pw, 1), gn_b.reshape(n_pair, pw, 1))


def _rwkv_chunk_kernel(rw_ref, mu_ref, w0_ref, wup_ref, a0_ref, aup_ref, kk_ref, ka_ref, rk_ref, bd_ref,
                       gng_ref, gnb_ref, y_ref, s_ref, last_sc):
    nb, C = rw_ref.shape[0], rw_ref.shape[1]
    n_pair = H_R // 2
    pw = 2 * HS_R
    c = pl.program_id(1)

    @pl.when(c == 0)
    def _():
        s_ref[...] = jnp.zeros(s_ref.shape, F32)
        last_sc[...] = jnp.zeros(last_sc.shape, F32)

    rw = rw_ref[...].reshape(nb * C, SHIFT_W)
    first = jnp.concatenate([jnp.broadcast_to(last_sc[b:b + 1, :], (C, SHIFT_W)) for b in range(nb)], axis=0)
    row_in_chunk = lax.broadcasted_iota(jnp.int32, (nb * C, 1), 0) % C
    prev = jnp.where(row_in_chunk == 0, first, pltpu.roll(rw, 1, axis=0))
    for b in range(nb):
        last_sc[b:b + 1, :] = rw[(b + 1) * C - 1:(b + 1) * C, :]
    r_all, lw_all, k_all, v_all, nkk_all, b_all, bonus = _rwkv_pre_math(
        rw, prev, mu_ref, w0_ref, wup_ref, a0_ref, aup_ref, kk_ref, ka_ref, rk_ref, bd_ref)

    lane = lax.broadcasted_iota(jnp.int32, (C, pw), 1)
    t_idx = lax.broadcasted_iota(jnp.int32, (C, pw), 0)
    head0 = lane < HS_R
    s_idx = lane % C
    strict = t_idx > s_idx
    incl = t_idx >= s_idx
    eye = jnp.where(t_idx == s_idx, 1.0, 0.0).astype(F32)
    tri = jnp.where(lax.broadcasted_iota(jnp.int32, (C, C), 0) >= lax.broadcasted_iota(jnp.int32, (C, C), 1),
                    1.0, 0.0).astype(BF16)
    same_head = (lax.broadcasted_iota(jnp.int32, (pw, pw), 0) // HS_R
                 == lax.broadcasted_iota(jnp.int32, (pw, pw), 1) // HS_R)

    def stack(x):
        return jnp.concatenate([jnp.where(head0, x, 0.0), jnp.where(head0, 0.0, x)], axis=0).astype(BF16)

    def per_head(coef, x):
        return _dot(coef.astype(BF16), stack(x))

    def head_mean(x):
        return jnp.where(head0, jnp.sum(jnp.where(head0, x, 0.0), axis=1, keepdims=True),
                         jnp.sum(jnp.where(head0, 0.0, x), axis=1, keepdims=True)) * (1.0 / HS_R)

    units = [(b, p) for b in range(nb) for p in range(n_pair)]
    ls = lambda p: slice(p * pw, (p + 1) * pw)
    take = lambda arr: [arr[b * C:(b + 1) * C, ls(p)] for b, p in units]
    lw = take(lw_all)
    lw_hi = [x.astype(BF16) for x in lw]
    cum = [_dot(tri, h) + _dot(tri, (x - h.astype(F32)).astype(BF16)) for x, h in zip(lw, lw_hi)]
    p_in = [jnp.exp(x) for x in cum]
    p_inv = [jnp.exp(-x) for x in cum]
    nt = [x * jnp.exp(cm - l) for x, cm, l in zip(take(nkk_all), cum, lw)]
    bt = [x * pi for x, pi in zip(take(b_all), p_inv)]
    kt = [x * pi for x, pi in zip(take(k_all), p_inv)]
    rt = [x * pp for x, pp in zip(take(r_all), p_in)]
    vv = take(v_all)

    lhs = [jnp.concatenate([n_, r_], axis=0).astype(BF16) for n_, r_ in zip(nt, rt)]
    coef = [lax.dot_general(l_, jnp.concatenate([stack(b_), stack(k_)], axis=0), _NT,
                            preferred_element_type=F32) for l_, b_, k_ in zip(lhs, bt, kt)]
    a_pow = [jnp.where(strict, cf[:C, :pw], 0.0) for cf in coef]
    b_m = [jnp.where(strict, cf[:C, pw:], 0.0) for cf in coef]
    ab_m = [jnp.where(incl, cf[C:, :pw], 0.0) for cf in coef]
    ak_m = [jnp.where(incl, cf[C:, pw:], 0.0) for cf in coef]

    inv = [eye + a for a in a_pow]
    for _ in range(C.bit_length() - 2):
        a_pow = [per_head(a, a) for a in a_pow]
        inv = [per_head(i_, eye + a) for i_, a in zip(inv, a_pow)]

    v_stack = [stack(x) for x in vv]
    bv = [_dot(m.astype(BF16), vs) for m, vs in zip(b_m, v_stack)]
    akv = [_dot(m.astype(BF16), vs) for m, vs in zip(ak_m, v_stack)]

    st = [s_ref[b, p] for b, p in units]
    ns_rs = [lax.dot_general(l_, s_.astype(BF16), _NT, preferred_element_type=F32)
             for l_, s_ in zip(lhs, st)]
    u = [per_head(i_, x[:C] + bv_) for i_, x, bv_ in zip(inv, ns_rs, bv)]
    y = [x[C:] + per_head(m, u_) + akv_ for x, m, u_, akv_ in zip(ns_rs, ab_m, u, akv)]

    for i, (b, p) in enumerate(units):
        uv_t = jnp.concatenate([u[i], vv[i]], axis=0).T.astype(BF16)
        upd = _dot(uv_t, jnp.concatenate([bt[i], kt[i]], axis=0).astype(BF16))
        s_ref[b, p] = jnp.where(same_head, (st[i] + upd) * p_in[i][C - 1:C, :], 0.0)

    for i, (b, p) in enumerate(units):
        d = y[i] - head_mean(y[i])
        var = head_mean(d * d)
        y_ref[b, :, ls(p)] = (d * lax.rsqrt(var + GN_EPS) * gng_ref[:, ls(p)] + gnb_ref[:, ls(p)]
                              + bonus[b * C:(b + 1) * C, ls(p)])


def _rwkv_chunk(rw, mu, w0, wup, a0, aup, k_k, k_a, r_k, bd, gn_g, gn_b, n_seq, seq_len):
    nb = CHUNK_SEQS
    C = CHUNK
    n_pair = H_R // 2
    const = lambda g, c: (0, 0)
    vec = lambda w: pl.BlockSpec((1, w), const)
    out_blk = pl.BlockSpec((nb, C, D_RWKV), lambda g, c: (g, c, 0))
    in_specs = [pl.BlockSpec((nb, C, SHIFT_W), lambda g, c: (g, c, 0)), vec(SHIFT_W),
                vec(D_RWKV), pl.BlockSpec((R_LORA, D_RWKV), const),
                vec(D_RWKV), pl.BlockSpec((R_LORA, D_RWKV), const),
                vec(D_RWKV), vec(D_RWKV), vec(D_RWKV), pl.BlockSpec((D_RWKV, D_RWKV), const),
                vec(D_RWKV), vec(D_RWKV)]
    seq3 = jax.ShapeDtypeStruct((n_seq, seq_len, D_RWKV), F32)
    return pl.pallas_call(
        _rwkv_chunk_kernel,
        out_shape=(seq3, jax.ShapeDtypeStruct((n_seq, n_pair, 2 * HS_R, 2 * HS_R), F32)),
        grid=(n_seq // nb, seq_len // C),
        in_specs=in_specs,
        out_specs=(out_blk, pl.BlockSpec((nb, n_pair, 2 * HS_R, 2 * HS_R), lambda g, c: (g, 0, 0, 0))),
        scratch_shapes=[pltpu.VMEM((nb, SHIFT_W), F32)],
        compiler_params=pltpu.CompilerParams(dimension_semantics=("parallel", "arbitrary"),
                                             vmem_limit_bytes=VMEM_LIMIT),
        name="rwkv_chunk",
    )(rw.reshape(n_seq, seq_len, SHIFT_W), mu, w0, wup, a0, aup, k_k, k_a, r_k, bd, gn_g[None], gn_b[None])


def _out_proj_kernel(oa_ref, yb_ref, gr_ref, x_ref, wo_ref, y_ref):
    mix_r = (yb_ref[...] * gr_ref[...]).astype(BF16)
    y_ref[...] = (x_ref[...] + _dot(oa_ref[...].astype(BF16), wo_ref[0:D_ATT, :])
                  + _dot(mix_r, wo_ref[D_ATT:, :]))


def _out_proj(oa, yb, gr, x, wo):
    rows = x.shape[0]
    tm = OUT_TILE
    row_blk = lambda w: pl.BlockSpec((tm, w), lambda i: (i, 0))
    return pl.pallas_call(
        _out_proj_kernel, out_shape=jax.ShapeDtypeStruct((rows, D_MODEL), F32), grid=(rows // tm,),
        in_specs=[row_blk(D_ATT), row_blk(D_RWKV), row_blk(D_RWKV), row_blk(D_MODEL),
                  pl.BlockSpec(wo.shape, lambda i: (0, 0))],
        out_specs=row_blk(D_MODEL),
        compiler_params=pltpu.CompilerParams(dimension_semantics=("parallel",), vmem_limit_bytes=VMEM_LIMIT),
        name="out_proj",
    )(oa, yb, gr, x, wo)


def _rope_tables(pos):
    half = DK_A // 2
    inv = 1.0 / ROPE_THETA ** (np.arange(half, dtype=np.float64) / half)
    ang = inv[:, None] * pos.astype(np.float64)[None, :]
    return jnp.asarray(np.cos(ang), F32), jnp.asarray(np.sin(ang), F32)


def kernel(x_prompt, x_sample, cache_k, cache_v, state_wkv, state_shift, page_table, ln_g, w_in, q_norm_g,
           k_norm_g, lambda_q1, lambda_k1, lambda_q2, lambda_k2, subln_g, shift_mu, w0, w_lora_up, a0,
           a_lora_up, k_k, k_a, r_k, gn_g, gn_b, w_out):
    n_b, seq = x_prompt.shape[:2]
    n_d, n_dec = x_sample.shape[:2]
    page = cache_k.shape[2]
    past = page_table.shape[1] * page
    layer = 0

    w_in_b = w_in[layer].astype(BF16)
    wq_t = w_in_b[:, 0:D_ATT].T
    wk_t = w_in_b[:, D_ATT:2 * D_ATT].T
    w_rest = w_in_b[:, 2 * D_ATT:]
    wo = w_out[layer].astype(BF16)
    wup = w_lora_up[layer].astype(BF16)
    aup = a_lora_up[layer].astype(BF16)
    qg = q_norm_g[layer].reshape(DK_A, 1)
    kg = k_norm_g[layer].reshape(DK_A, 1)
    seg = jnp.arange(D_RWKV) // HS_R
    bd = (seg[:, None] == seg[None, :]).astype(BF16)
    lam_vecs = (lambda_q1[layer][None], lambda_k1[layer][None], lambda_q2[layer][None], lambda_k2[layer][None])
    subg = subln_g[layer][None]
    row = lambda a: a[layer][None]

    pre_params = (row(shift_mu), row(w0), wup, row(a0), aup, row(k_k), row(k_a), row(r_k), bd)

    def mix_long(rw, n_seq, seq_len):
        yb, s_fin = _rwkv_chunk(rw, *pre_params, gn_g[layer], gn_b[layer], n_seq, seq_len)
        s6 = s_fin.reshape(n_seq, H_R // 2, 2, HS_R, 2, HS_R)
        s_heads = jnp.stack([s6[:, :, e, :, e, :] for e in range(2)], axis=2)
        return yb.reshape(n_seq * seq_len, D_RWKV), s_heads.reshape(n_seq, H_R, HS_R, HS_R)

    def mix_short(rw, first, s0, n_seq, seq_len):
        r, lw, k2, vv, nkk, bb, bonus = _rwkv_pre(rw, first, seq_len, *pre_params)
        yb, s_fin = _rwkv_steps(r, lw, k2, vv, nkk, bb, bonus, s0.transpose(1, 2, 3, 0), gn_g[layer],
                                gn_b[layer], n_seq, seq_len)
        return yb, s_fin.transpose(3, 0, 1, 2)

    def project(x, pos, rows, proj_seqs):
        cos_t, sin_t = _rope_tables(pos)
        return _proj(x.reshape(rows, D_MODEL), ln_g[layer][None], wq_t, wk_t, w_rest, cos_t, sin_t, qg, kg,
                     proj_seqs, rows // proj_seqs)

    rows_p = n_b * seq
    rows_s = n_d * n_dec
    q_p, kt_p, ktb_p, v_p, vb_p, ga_p, rw_p, gr_p = project(x_prompt, np.tile(np.arange(seq), n_b), rows_p, n_b)
    q_s, kt_s, _, v_s, _, ga_s, rw_s, gr_s = project(x_sample, np.tile(past + np.arange(n_dec), n_d), rows_s, 1)

    kt_pages = cache_k[layer].transpose(0, 2, 3, 4, 1).reshape(-1, H_A * 2 * DK_A, page)
    v_pages = cache_v[layer].reshape(-1, page * H_A, DV_A)
    q3 = q_s.reshape(n_d, n_dec, D_ATT)
    k_new = kt_s.reshape(D_ATT, rows_s).T.reshape(n_d, n_dec, D_ATT)
    oa_s, oa_p = _attn(page_table, jnp.concatenate([q3, q3], axis=1), k_new, v_s.reshape(n_d, n_dec, D_ATT),
                       ga_s.reshape(n_d, n_dec, D_ATT), q_p, ktb_p, vb_p, ga_p, *lam_vecs, subg,
                       kt_pages, v_pages, n_b, seq)

    yb_p, s_p = mix_long(rw_p, n_b, seq)
    yp = _out_proj(oa_p, yb_p, gr_p, x_prompt.reshape(rows_p, D_MODEL), wo)
    first_s = jnp.repeat(state_shift[layer], n_dec, axis=0)
    yb_s, s_s = mix_short(rw_s, first_s, state_wkv[layer], n_d, n_dec)
    ys = _out_proj(oa_s.reshape(rows_s, D_ATT), yb_s, gr_s, x_sample.reshape(rows_s, D_MODEL), wo)

    k_prompt = kt_p.reshape(n_b, H_A, 2, DK_A, seq).transpose(0, 4, 1, 2, 3)
    k_sample = kt_s.reshape(H_A, 2, DK_A, n_d, n_dec).transpose(3, 4, 0, 1, 2)
    return (yp.reshape(n_b, seq, D_MODEL), ys.reshape(n_d, n_dec, D_MODEL),
            k_prompt[None], v_p.reshape(n_b, seq, H_A, DV_A)[None], s_p[None],
            rw_p.reshape(n_b, seq, SHIFT_W)[:, -1][None],
            k_sample[None], v_s.reshape(n_d, n_dec, H_A, DV_A)[None], s_s[None],
            rw_s.reshape(n_d, n_dec, SHIFT_W)[:, -1][None])
```

```python
import functools
import math

import jax
import jax.numpy as jnp
import numpy as np
from jax import lax
from jax.experimental import pallas as pl
from jax.experimental.pallas import tpu as pltpu

F32 = jnp.float32
BF16 = jnp.bfloat16

D_MODEL = 1024
D_ATT = 512
D_RWKV = 512
DK_A = 64
H_A = 4
DV_A = 128
HS_R = 64
H_R = 8
R_LORA = 64
SHIFT_W = 3 * D_RWKV + 2 * R_LORA
ROPE_THETA = 10000.0
RMS_EPS = 1e-6
GN_EPS = 64e-5
L2_EPS = 1e-12
LAM_INIT = 0.8 - 0.6 * math.exp(-0.3 * 0)

ROW_TILE = 512
OUT_TILE = 512
ATT_TILE = 512
KEY_TILE = 1024
LOG2E = math.log2(math.e)
CHUNK = HS_R
CHUNK_SEQS = 4
PAGES_PER_STEP = 16
VMEM_LIMIT = 56 * 1024 * 1024

_NT = (((1,), (1,)), ((), ()))


def _dot(a, b):
    return jnp.dot(a, b, preferred_element_type=F32)


def _segsum(x, bd):
    hi = x.astype(BF16)
    lo = (x - hi.astype(F32)).astype(BF16)
    return _dot(hi, bd) + _dot(lo, bd)


def _diff_lambda(lq1, lk1, lq2, lk2):
    return (jnp.exp(jnp.sum(lq1 * lk1, axis=1, keepdims=True))
            - jnp.exp(jnp.sum(lq2 * lk2, axis=1, keepdims=True)) + LAM_INIT)


def _subln_gate(o0, o1, lam, subg, gate):
    y = o0 - lam * o1
    y = y * lax.rsqrt(jnp.mean(y * y, axis=-1, keepdims=True) + RMS_EPS) * subg
    return (y * (1.0 - LAM_INIT)) * gate


def _proj_kernel(x_ref, lng_ref, wq_ref, wk_ref, wr_ref, cos_ref, sin_ref, qg_ref, kg_ref,
                 q_ref, kt_ref, ktb_ref, v_ref, vb_ref, ga_ref, rw_ref, gr_ref):
    tm = x_ref.shape[0]
    x = x_ref[...]
    h = x * lax.rsqrt(jnp.mean(x * x, axis=-1, keepdims=True) + RMS_EPS) * lng_ref[...]
    h = h.astype(BF16)
    cos = cos_ref[...][None]
    sin = sin_ref[...][None]

    def norm_rope(t, g):
        t3 = t.reshape(2 * H_A, DK_A, tm)
        t3 = t3 * lax.rsqrt(jnp.mean(t3 * t3, axis=1, keepdims=True) + RMS_EPS) * g[None]
        x1 = t3[:, :DK_A // 2]
        x2 = t3[:, DK_A // 2:]
        o = jnp.concatenate([x1 * cos - x2 * sin, x2 * cos + x1 * sin], axis=1)
        return o.reshape(D_ATT, tm)

    qt = lax.dot_general(wq_ref[...], h, _NT, preferred_element_type=F32)
    qt = norm_rope(qt, qg_ref[...]) * (DK_A ** -0.5 * LOG2E)
    q_ref[...] = qt.T.astype(BF16)
    kt = lax.dot_general(wk_ref[...], h, _NT, preferred_element_type=F32)
    kt = norm_rope(kt, kg_ref[...])
    kt_ref[...] = kt
    ktb_ref[...] = kt.astype(BF16)

    v = _dot(h, wr_ref[:, 0:D_ATT])
    for hd in range(H_A):
        v_ref[:, hd, :] = v[:, hd * DV_A:(hd + 1) * DV_A]
    vb_ref[...] = v.astype(BF16)
    za = _dot(h, wr_ref[:, D_ATT:2 * D_ATT])
    ga_ref[...] = za * jax.nn.sigmoid(za)
    rw_ref[...] = _dot(h, wr_ref[:, 2 * D_ATT:2 * D_ATT + SHIFT_W])
    zr = _dot(h, wr_ref[:, 2 * D_ATT + SHIFT_W:])
    gr_ref[...] = zr * jax.nn.sigmoid(zr)


def _proj(x, ln_g, wq_t, wk_t, w_rest, cos_t, sin_t, qg, kg, n_seq, seq_len):
    rows = x.shape[0]
    tm = ROW_TILE
    nsi = seq_len // tm
    tk = min(KEY_TILE, seq_len)
    per_tk = tk // tm
    const = lambda i: (0, 0)
    row_blk = lambda w: pl.BlockSpec((tm, w), lambda i: (i, 0))
    out_shape = (
        jax.ShapeDtypeStruct((rows, D_ATT), BF16),
        jax.ShapeDtypeStruct((n_seq, D_ATT, seq_len), F32),
        jax.ShapeDtypeStruct((n_seq, seq_len // tk, D_ATT, tk), BF16),
        jax.ShapeDtypeStruct((rows, H_A, DV_A), F32),
        jax.ShapeDtypeStruct((rows, D_ATT), BF16),
        jax.ShapeDtypeStruct((rows, D_ATT), F32),
        jax.ShapeDtypeStruct((rows, SHIFT_W), F32),
        jax.ShapeDtypeStruct((rows, D_RWKV), F32),
    )
    out_specs = (
        row_blk(D_ATT),
        pl.BlockSpec((None, D_ATT, tm), lambda i: (i // nsi, 0, i % nsi)),
        pl.BlockSpec((None, None, D_ATT, tm), lambda i: (i // nsi, (i % nsi) // per_tk, 0, i % per_tk)),
        pl.BlockSpec((tm, H_A, DV_A), lambda i: (i, 0, 0)),
        row_blk(D_ATT), row_blk(D_ATT), row_blk(SHIFT_W), row_blk(D_RWKV),
    )
    in_specs = [
        row_blk(D_MODEL),
        pl.BlockSpec((1, D_MODEL), const),
        pl.BlockSpec(wq_t.shape, const),
        pl.BlockSpec(wk_t.shape, const),
        pl.BlockSpec(w_rest.shape, const),
        pl.BlockSpec((DK_A // 2, tm), lambda i: (0, i)),
        pl.BlockSpec((DK_A // 2, tm), lambda i: (0, i)),
        pl.BlockSpec((DK_A, 1), const),
        pl.BlockSpec((DK_A, 1), const),
    ]
    return pl.pallas_call(
        _proj_kernel, out_shape=out_shape, grid=(rows // tm,), in_specs=in_specs, out_specs=out_specs,
        compiler_params=pltpu.CompilerParams(dimension_semantics=("parallel",), vmem_limit_bytes=VMEM_LIMIT),
        name="proj",
    )(x, ln_g, wq_t, wk_t, w_rest, cos_t, sin_t, qg, kg)


def _attn_kernel(n_dec, pt_ref, dec_ref, qp_ref, ktp_ref, vp_ref, gap_ref, par_ref, *refs):
    npg = PAGES_PER_STEP
    k_refs = refs[:npg]
    v_refs = refs[npg:2 * npg]
    os_ref, op_ref, m_sc, l_sc, acc_sc, qs_sc, m_p, acc_p = refs[2 * npg:]
    i = pl.program_id(2)
    j = pl.program_id(3)
    g = j
    tq = qp_ref.shape[0]
    kn0, vn0, ga0 = 2 * n_dec, 3 * n_dec, 4 * n_dec
    lam_vecs = [par_ref[:, n * DK_A:(n + 1) * DK_A] for n in range(4)]
    subg = par_ref[:, 4 * DK_A:4 * DK_A + DV_A]
    tk = ktp_ref.shape[1]

    @pl.when(j == 0)
    def _():
        m_sc[...] = jnp.full(m_sc.shape, -jnp.inf, F32)
        l_sc[...] = jnp.zeros(l_sc.shape, F32)
        acc_sc[...] = jnp.zeros(acc_sc.shape, F32)
        q = qp_ref[...]
        lane_q = lax.broadcasted_iota(jnp.int32, q.shape, 1)
        zero = jnp.zeros_like(q)
        qs_sc[...] = jnp.concatenate([jnp.where(lane_q < DK_A, q, zero), jnp.where(lane_q >= DK_A, q, zero)],
                                     axis=0)
        m_p[...] = jnp.full(m_p.shape, -jnp.inf, F32)
        acc_p[...] = jnp.zeros(acc_p.shape, F32)

    per_key = tk // tq
    jd = i // per_key

    def prompt_tile(diagonal):
        sp = _dot(qs_sc[...], ktp_ref[...])
        if diagonal:
            row_p = lax.broadcasted_iota(jnp.int32, sp.shape, 0)
            col_p = lax.broadcasted_iota(jnp.int32, sp.shape, 1)
            q_off = (i % per_key) * tq
            sp = jnp.where(col_p <= jnp.where(row_p >= tq, row_p - tq, row_p) + q_off, sp, -jnp.inf)
        mp_prev = m_p[...]
        mp_new = jnp.maximum(mp_prev, jnp.max(sp, axis=1, keepdims=True))
        alpha_p = jnp.exp2(mp_prev - mp_new)
        pp = jnp.exp2(sp - jnp.tile(mp_new, (1, tk // DV_A)))
        v_ext = jnp.concatenate([vp_ref[...], jnp.ones((tk, DV_A), BF16)], axis=1)
        acc_p[...] = jnp.tile(alpha_p, (1, 2)) * acc_p[...] + _dot(pp.astype(BF16), v_ext)
        m_p[...] = mp_new

    o_ref = os_ref
    rowi = lax.broadcasted_iota(jnp.int32, (2 * n_dec, DV_A), 0)
    lane = lax.broadcasted_iota(jnp.int32, (2 * n_dec, DV_A), 1)
    sel = ((rowi < n_dec) & (lane < DK_A)) | ((rowi >= n_dec) & (lane >= DK_A))
    rows_h = 2 * n_dec

    def head_q(h):
        qh = dec_ref[0:2 * n_dec, h * DV_A:(h + 1) * DV_A].astype(BF16)
        return jnp.where(sel, qh, jnp.zeros_like(qh))

    def decode_pages():
        page = k_refs[0].shape[-1]
        s = jnp.concatenate(
            [jnp.concatenate([_dot(head_q(h), k_refs[n][h * 2 * DK_A:(h + 1) * 2 * DK_A, :].astype(BF16))
                              for n in range(npg)], axis=1) for h in range(H_A)], axis=0)
        m_prev = m_sc[...]
        m_new = jnp.maximum(m_prev, jnp.max(s, axis=1, keepdims=True))
        alpha = jnp.exp2(m_prev - m_new)
        p = jnp.exp2(s - m_new)
        l_sc[...] = alpha * l_sc[...] + jnp.sum(p, axis=1, keepdims=True)
        m_sc[...] = m_new
        pb = p.astype(BF16)
        for h in range(H_A):
            acc = alpha[h * rows_h:(h + 1) * rows_h] * acc_sc[h]
            for n in range(npg):
                vh = v_refs[n][pl.ds(h, page, stride=H_A), :]
                acc = acc + _dot(pb[h * rows_h:(h + 1) * rows_h, n * page:(n + 1) * page], vh.astype(BF16))
            acc_sc[h] = acc

    @pl.when(j < jd)
    def _():
        prompt_tile(False)
        decode_pages()

    @pl.when(j == jd)
    def _():
        prompt_tile(True)
        decode_pages()
        acc = acc_p[...]
        o = acc[:, :DV_A] / acc[:, DV_A:]
        lam = _diff_lambda(*lam_vecs)
        op_ref[...] = _subln_gate(o[:tq], o[tq:], lam, subg, gap_ref[...]).astype(BF16)

    @pl.when(j > jd)
    def _():
        decode_pages()

    @pl.when(g == pl.num_programs(3) - 1)
    def _():
        lam = _diff_lambda(*lam_vecs)
        t_idx = jnp.where(rowi[:, :1] >= n_dec, rowi[:, :1] - n_dec, rowi[:, :1])
        for h in range(H_A):
            hs = slice(h * DV_A, (h + 1) * DV_A)
            qf = head_q(h).astype(F32)
            s_new = []
            for j in range(n_dec):
                sj = jnp.sum(qf * dec_ref[kn0 + j:kn0 + j + 1, hs], axis=1, keepdims=True)
                s_new.append(jnp.where(j <= t_idx, sj, -jnp.inf))
            m_prev = m_sc[h * rows_h:(h + 1) * rows_h]
            m_new = m_prev
            for sj in s_new:
                m_new = jnp.maximum(m_new, sj)
            alpha = jnp.exp2(m_prev - m_new)
            l = alpha * l_sc[h * rows_h:(h + 1) * rows_h]
            acc = alpha * acc_sc[h]
            for j, sj in enumerate(s_new):
                pj = jnp.exp2(sj - m_new)
                l = l + pj
                acc = acc + pj * dec_ref[vn0 + j:vn0 + j + 1, hs]
            o = acc / l
            o_ref[:, hs] = _subln_gate(o[:n_dec], o[n_dec:], lam, subg, dec_ref[ga0:ga0 + n_dec, hs])


def _attn(page_table, dec_ops, n_dec, q_p, ktb_p, v_p, ga_p, params, kt_pages, v_pages, n_seq, seq_len):
    n_d, n_pages = page_table.shape
    npg = PAGES_PER_STEP
    page = kt_pages.shape[-1]
    tq = ATT_TILE
    tk = KEY_TILE
    nq = seq_len // tq
    nk = seq_len // tk
    per_key = tk // tq
    assert n_d == n_seq * H_A * nq and n_pages == nk * npg
    dec = lambda b, h, i: (b * H_A + h) * nq + i
    key_tile = lambda i, j: jnp.minimum(j, i // per_key)
    per_d = lambda r: pl.BlockSpec((None, r, D_ATT), lambda b, h, i, j, pt: (dec(b, h, i), 0, 0))
    q_tile = pl.BlockSpec((tq, DV_A), lambda b, h, i, j, pt: (b * nq + i, h))

    page_id = lambda b, h, i, j, pt, n: pt[(dec(b, h, i) * nk + j) * npg + n]

    def k_spec(n):
        return pl.BlockSpec((None, H_A * 2 * DK_A, page),
                            lambda b, h, i, j, pt: (page_id(b, h, i, j, pt, n), 0, 0))

    def v_spec(n):
        return pl.BlockSpec((None, page * H_A, DV_A),
                            lambda b, h, i, j, pt: (page_id(b, h, i, j, pt, n), 0, 0))

    in_specs = ([per_d(dec_ops.shape[1]),
                 q_tile,
                 pl.BlockSpec((None, None, DV_A, tk), lambda b, h, i, j, pt: (b, key_tile(i, j), h, 0)),
                 pl.BlockSpec((tk, DV_A), lambda b, h, i, j, pt: (b * nk + key_tile(i, j), h)),
                 q_tile,
                 pl.BlockSpec(params.shape, lambda b, h, i, j, pt: (0, 0))]
                + [k_spec(n) for n in range(npg)] + [v_spec(n) for n in range(npg)])
    grid_spec = pltpu.PrefetchScalarGridSpec(
        num_scalar_prefetch=1, grid=(n_seq, H_A, nq, nk), in_specs=in_specs,
        out_specs=(per_d(n_dec), q_tile),
        scratch_shapes=[pltpu.VMEM((H_A * 2 * n_dec, 1), F32), pltpu.VMEM((H_A * 2 * n_dec, 1), F32),
                        pltpu.VMEM((H_A, 2 * n_dec, DV_A), F32),
                        pltpu.VMEM((2 * tq, DV_A), BF16), pltpu.VMEM((2 * tq, DV_A), F32),
                        pltpu.VMEM((2 * tq, 2 * DV_A), F32)])
    return pl.pallas_call(
        functools.partial(_attn_kernel, n_dec),
        out_shape=(jax.ShapeDtypeStruct((n_d, n_dec, D_ATT), F32),
                   jax.ShapeDtypeStruct((n_seq * seq_len, D_ATT), BF16)),
        grid_spec=grid_spec,
        compiler_params=pltpu.CompilerParams(
            dimension_semantics=("arbitrary", "arbitrary", "arbitrary", "arbitrary"),
            vmem_limit_bytes=VMEM_LIMIT),
        name="attn",
    )(page_table.reshape(-1), dec_ops, q_p, ktb_p, v_p, ga_p, params,
      *([kt_pages] * npg), *([v_pages] * npg))


def _rwkv_pre_math(rw, prev, mu_ref, w0_ref, wup_ref, a0_ref, aup_ref, kk_ref, ka_ref, rk_ref, bd_ref):
    u = rw + (prev - rw) * mu_ref[...]
    r = u[:, 0:D_RWKV]
    k = u[:, D_RWKV:2 * D_RWKV]
    v = u[:, 2 * D_RWKV:3 * D_RWKV]
    wd = u[:, 3 * D_RWKV:3 * D_RWKV + R_LORA]
    ad = u[:, 3 * D_RWKV + R_LORA:]
    w_log = -jax.nn.softplus(-(w0_ref[...] + _dot(jnp.tanh(wd).astype(BF16), wup_ref[...]))) - 0.5
    log_decay = -jnp.exp(w_log)
    a = jax.nn.sigmoid(a0_ref[...] + _dot(ad.astype(BF16), aup_ref[...]))
    bd = bd_ref[...]
    kk = k * kk_ref[...]
    kk = kk / jnp.maximum(jnp.sqrt(_segsum(kk * kk, bd)), L2_EPS)
    k2 = k * (1.0 + (a - 1.0) * ka_ref[...])
    return r, log_decay, k2, v, -kk, kk * a, _segsum(r * k2 * rk_ref[...], bd) * v


def _rwkv_pre_kernel(seq_len, rw_ref, first_ref, mu_ref, w0_ref, wup_ref, a0_ref, aup_ref,
                     kk_ref, ka_ref, rk_ref, bd_ref,
                     r_out, w_out, k_out, v_out, nkk_out, b_out, bonus_out):
    tm = rw_ref.shape[0]
    rw = rw_ref[...]
    rowi = lax.broadcasted_iota(jnp.int32, (tm, 1), 0)
    prev = jnp.where(rowi % seq_len == 0, first_ref[...], pltpu.roll(rw, 1, axis=0))
    outs = _rwkv_pre_math(rw, prev, mu_ref, w0_ref, wup_ref, a0_ref, aup_ref, kk_ref, ka_ref, rk_ref, bd_ref)
    for ref, val in zip((r_out, w_out, k_out, v_out, nkk_out, b_out, bonus_out), outs):
        ref[...] = val


def _rwkv_pre(rw, first, seq_len, mu, w0, wup, a0, aup, k_k, k_a, r_k, bd):
    rows = rw.shape[0]
    tm = ROW_TILE
    assert tm % seq_len == 0
    const = lambda i: (0, 0)
    row_blk = lambda w: pl.BlockSpec((tm, w), lambda i: (i, 0))
    in_specs = [
        row_blk(SHIFT_W),
        row_blk(SHIFT_W),
        pl.BlockSpec((1, SHIFT_W), const),
        pl.BlockSpec((1, D_RWKV), const), pl.BlockSpec((R_LORA, D_RWKV), const),
        pl.BlockSpec((1, D_RWKV), const), pl.BlockSpec((R_LORA, D_RWKV), const),
        pl.BlockSpec((1, D_RWKV), const), pl.BlockSpec((1, D_RWKV), const), pl.BlockSpec((1, D_RWKV), const),
        pl.BlockSpec((D_RWKV, D_RWKV), const),
    ]
    out = jax.ShapeDtypeStruct((rows, D_RWKV), F32)
    return pl.pallas_call(
        functools.partial(_rwkv_pre_kernel, seq_len),
        out_shape=(out,) * 7, grid=(rows // tm,), in_specs=in_specs, out_specs=(row_blk(D_RWKV),) * 7,
        compiler_params=pltpu.CompilerParams(dimension_semantics=("parallel",), vmem_limit_bytes=VMEM_LIMIT),
        name="rwkv_pre",
    )(rw, first, mu, w0, wup, a0, aup, k_k, k_a, r_k, bd)


def _rwkv_steps_kernel(r_ref, lw_ref, k_ref, v_ref, nkk_ref, b_ref, bonus_ref, s0_ref, gng_ref, gnb_ref,
                       y_ref, s_ref, xt_sc, y_sc):
    n_b = s0_ref.shape[-1]
    n_t = r_ref.shape[0] // n_b
    for a, ref in enumerate((r_ref, lw_ref, k_ref, v_ref, nkk_ref, b_ref)):
        for t in range(n_t):
            xt_sc[a, t] = ref[pl.ds(t, n_b, stride=n_t), :].T
    s_ref[...] = s0_ref[...]

    for t in range(n_t):
        for e in range(2):
            hk = slice(e * HS_R, (e + 1) * HS_R)
            r_e = xt_sc[0, t, hk]
            w_e = jnp.exp(xt_sc[1, t, hk])
            k_e = xt_sc[2, t, hk]
            nk_e = xt_sc[4, t, hk]
            b_e = xt_sc[5, t, hk]

            def rows(g, carry):
                v0 = pl.multiple_of(g * 8, 8)
                v8 = xt_sc[3, t, pl.ds(e * HS_R + v0, 8), :]
                ys = []
                for i in range(8):
                    s = s_ref[e, v0 + i]
                    sa = jnp.sum(s * nk_e, axis=0, keepdims=True)
                    s = s * w_e + sa * b_e + v8[i:i + 1] * k_e
                    s_ref[e, v0 + i] = s
                    ys.append(jnp.sum(s * r_e, axis=0, keepdims=True))
                y_sc[pl.ds(e * HS_R + v0, 8), :] = jnp.concatenate(ys, axis=0)
                return carry

            lax.fori_loop(0, HS_R // 8, rows, 0)

        y = y_sc[...]
        halves = []
        for e in range(2):
            yh = y[e * HS_R:(e + 1) * HS_R]
            d = yh - jnp.mean(yh, axis=0, keepdims=True)
            halves.append(d * lax.rsqrt(jnp.mean(d * d, axis=0, keepdims=True) + GN_EPS))
        y = jnp.concatenate(halves, axis=0) * gng_ref[...] + gnb_ref[...]
        y_ref[pl.ds(t, n_b, stride=n_t), :] = y.T + bonus_ref[pl.ds(t, n_b, stride=n_t), :]


def _rwkv_steps(r, lw, k, v, nkk, b, bonus, s0, gn_g, gn_b, n_seq, seq_len):
    rows = n_seq * seq_len
    pw = 2 * HS_R
    n_pair = H_R // 2
    in_blk = pl.BlockSpec((rows, pw), lambda p: (0, p))
    st_blk = pl.BlockSpec((2, HS_R, HS_R, n_seq), lambda p: (p, 0, 0, 0))
    gn_blk = pl.BlockSpec((None, pw, 1), lambda p: (p, 0, 0))
    return pl.pallas_call(
        _rwkv_steps_kernel,
        out_shape=(jax.ShapeDtypeStruct((rows, D_RWKV), F32), jax.ShapeDtypeStruct(s0.shape, F32)),
        grid=(n_pair,),
        in_specs=[in_blk] * 7 + [st_blk, gn_blk, gn_blk],
        out_specs=(in_blk, st_blk),
        scratch_shapes=[pltpu.VMEM((6, seq_len, pw, n_seq), F32), pltpu.VMEM((pw, n_seq), F32)],
        compiler_params=pltpu.CompilerParams(dimension_semantics=("parallel",), vmem_limit_bytes=VMEM_LIMIT),
        name="rwkv_steps",
    )(r, lw, k, v, nkk, b, bonus, s0, gn_g.reshape(n_pair, pw, 1), gn_b.reshape(n_pair, pw, 1))


def _rwkv_chunk_kernel(rw_ref, mu_ref, w0_ref, wup_ref, a0_ref, aup_ref, kk_ref, ka_ref, rk_ref, bd_ref,
                       gng_ref, gnb_ref, y_ref, s_ref, last_sc):
    nb, C = rw_ref.shape[0], rw_ref.shape[1]
    n_pair = H_R // 2
    pw = 2 * HS_R
    c = pl.program_id(1)

    @pl.when(c == 0)
    def _():
        s_ref[...] = jnp.zeros(s_ref.shape, F32)
        last_sc[...] = jnp.zeros(last_sc.shape, F32)

    rw = rw_ref[...].reshape(nb * C, SHIFT_W)
    first = jnp.concatenate([jnp.broadcast_to(last_sc[b:b + 1, :], (C, SHIFT_W)) for b in range(nb)], axis=0)
    row_in_chunk = lax.broadcasted_iota(jnp.int32, (nb * C, 1), 0) % C
    prev = jnp.where(row_in_chunk == 0, first, pltpu.roll(rw, 1, axis=0))
    for b in range(nb):
        last_sc[b:b + 1, :] = rw[(b + 1) * C - 1:(b + 1) * C, :]
    r_all, lw_all, k_all, v_all, nkk_all, b_all, bonus = _rwkv_pre_math(
        rw, prev, mu_ref, w0_ref, wup_ref, a0_ref, aup_ref, kk_ref, ka_ref, rk_ref, bd_ref)

    lane = lax.broadcasted_iota(jnp.int32, (C, pw), 1)
    t_idx = lax.broadcasted_iota(jnp.int32, (C, pw), 0)
    head0 = lane < HS_R
    s_idx = lane % C
    strict = t_idx > s_idx
    incl = t_idx >= s_idx
    eye = jnp.where(t_idx == s_idx, 1.0, 0.0).astype(F32)
    tri = jnp.where(lax.broadcasted_iota(jnp.int32, (C, C), 0) >= lax.broadcasted_iota(jnp.int32, (C, C), 1),
                    1.0, 0.0).astype(BF16)
    same_head = (lax.broadcasted_iota(jnp.int32, (pw, pw), 0) // HS_R
                 == lax.broadcasted_iota(jnp.int32, (pw, pw), 1) // HS_R)

    def stack(x):
        return jnp.concatenate([jnp.where(head0, x, 0.0), jnp.where(head0, 0.0, x)], axis=0).astype(BF16)

    def per_head(coef, x):
        return _dot(coef.astype(BF16), stack(x))

    def head_mean(x):
        return jnp.where(head0, jnp.sum(jnp.where(head0, x, 0.0), axis=1, keepdims=True),
                         jnp.sum(jnp.where(head0, 0.0, x), axis=1, keepdims=True)) * (1.0 / HS_R)

    units = [(b, p) for b in range(nb) for p in range(n_pair)]
    ls = lambda p: slice(p * pw, (p + 1) * pw)
    take = lambda arr: [arr[b * C:(b + 1) * C, ls(p)] for b, p in units]
    lw = take(lw_all)
    lw_hi = [x.astype(BF16) for x in lw]
    cum = [_dot(tri, h) + _dot(tri, (x - h.astype(F32)).astype(BF16)) for x, h in zip(lw, lw_hi)]
    p_in = [jnp.exp(x) for x in cum]
    p_inv = [jnp.exp(-x) for x in cum]
    nt = [x * jnp.exp(cm - l) for x, cm, l in zip(take(nkk_all), cum, lw)]
    bt = [x * pi for x, pi in zip(take(b_all), p_inv)]
    kt = [x * pi for x, pi in zip(take(k_all), p_inv)]
    rt = [x * pp for x, pp in zip(take(r_all), p_in)]
    vv = take(v_all)

    lhs = [jnp.concatenate([n_, r_], axis=0).astype(BF16) for n_, r_ in zip(nt, rt)]
    coef = [lax.dot_general(l_, jnp.concatenate([stack(b_), stack(k_)], axis=0), _NT,
                            preferred_element_type=F32) for l_, b_, k_ in zip(lhs, bt, kt)]
    a_pow = [jnp.where(strict, cf[:C, :pw], 0.0) for cf in coef]
    b_m = [jnp.where(strict, cf[:C, pw:], 0.0) for cf in coef]
    ab_m = [jnp.where(incl, cf[C:, :pw], 0.0) for cf in coef]
    ak_m = [jnp.where(incl, cf[C:, pw:], 0.0) for cf in coef]

    inv = [eye + a for a in a_pow]
    for _ in range(C.bit_length() - 2):
        a_pow = [per_head(a, a) for a in a_pow]
        inv = [per_head(i_, eye + a) for i_, a in zip(inv, a_pow)]

    v_stack = [stack(x) for x in vv]
    bv = [_dot(m.astype(BF16), vs) for m, vs in zip(b_m, v_stack)]
    akv = [_dot(m.astype(BF16), vs) for m, vs in zip(ak_m, v_stack)]

    st = [s_ref[b, p] for b, p in units]
    ns_rs = [lax.dot_general(l_, s_.astype(BF16), _NT, preferred_element_type=F32)
             for l_, s_ in zip(lhs, st)]
    u = [per_head(i_, x[:C] + bv_) for i_, x, bv_ in zip(inv, ns_rs, bv)]
    y = [x[C:] + per_head(m, u_) + akv_ for x, m, u_, akv_ in zip(ns_rs, ab_m, u, akv)]

    for i, (b, p) in enumerate(units):
        uv_t = jnp.concatenate([u[i], vv[i]], axis=0).T.astype(BF16)
        upd = _dot(uv_t, jnp.concatenate([bt[i], kt[i]], axis=0).astype(BF16))
        s_ref[b, p] = jnp.where(same_head, (st[i] + upd) * p_in[i][C - 1:C, :], 0.0)

    for i, (b, p) in enumerate(units):
        d = y[i] - head_mean(y[i])
        var = head_mean(d * d)
        y_ref[b, :, ls(p)] = (d * lax.rsqrt(var + GN_EPS) * gng_ref[:, ls(p)] + gnb_ref[:, ls(p)]
                              + bonus[b * C:(b + 1) * C, ls(p)])


def _rwkv_chunk(rw, mu, w0, wup, a0, aup, k_k, k_a, r_k, bd, gn_g, gn_b, n_seq, seq_len):
    nb = CHUNK_SEQS
    C = CHUNK
    n_pair = H_R // 2
    const = lambda g, c: (0, 0)
    vec = lambda w: pl.BlockSpec((1, w), const)
    out_blk = pl.BlockSpec((nb, C, D_RWKV), lambda g, c: (g, c, 0))
    in_specs = [pl.BlockSpec((nb, C, SHIFT_W), lambda g, c: (g, c, 0)), vec(SHIFT_W),
                vec(D_RWKV), pl.BlockSpec((R_LORA, D_RWKV), const),
                vec(D_RWKV), pl.BlockSpec((R_LORA, D_RWKV), const),
                vec(D_RWKV), vec(D_RWKV), vec(D_RWKV), pl.BlockSpec((D_RWKV, D_RWKV), const),
                vec(D_RWKV), vec(D_RWKV)]
    seq3 = jax.ShapeDtypeStruct((n_seq, seq_len, D_RWKV), F32)
    return pl.pallas_call(
        _rwkv_chunk_kernel,
        out_shape=(seq3, jax.ShapeDtypeStruct((n_seq, n_pair, 2 * HS_R, 2 * HS_R), F32)),
        grid=(n_seq // nb, seq_len // C),
        in_specs=in_specs,
        out_specs=(out_blk, pl.BlockSpec((nb, n_pair, 2 * HS_R, 2 * HS_R), lambda g, c: (g, 0, 0, 0))),
        scratch_shapes=[pltpu.VMEM((nb, SHIFT_W), F32)],
        compiler_params=pltpu.CompilerParams(dimension_semantics=("parallel", "arbitrary"),
                                             vmem_limit_bytes=VMEM_LIMIT),
        name="rwkv_chunk",
    )(rw.reshape(n_seq, seq_len, SHIFT_W), mu, w0, wup, a0, aup, k_k, k_a, r_k, bd, gn_g[None], gn_b[None])


def _out_proj_kernel(oa_ref, yb_ref, gr_ref, x_ref, wo_ref, y_ref):
    mix_r = (yb_ref[...] * gr_ref[...]).astype(BF16)
    y_ref[...] = (x_ref[...] + _dot(oa_ref[...].astype(BF16), wo_ref[0:D_ATT, :])
                  + _dot(mix_r, wo_ref[D_ATT:, :]))


def _out_proj(oa, yb, gr, x, wo):
    rows = x.shape[0]
    tm = OUT_TILE
    row_blk = lambda w: pl.BlockSpec((tm, w), lambda i: (i, 0))
    return pl.pallas_call(
        _out_proj_kernel, out_shape=jax.ShapeDtypeStruct((rows, D_MODEL), F32), grid=(rows // tm,),
        in_specs=[row_blk(D_ATT), row_blk(D_RWKV), row_blk(D_RWKV), row_blk(D_MODEL),
                  pl.BlockSpec(wo.shape, lambda i: (0, 0))],
        out_specs=row_blk(D_MODEL),
        compiler_params=pltpu.CompilerParams(dimension_semantics=("parallel",), vmem_limit_bytes=VMEM_LIMIT),
        name="out_proj",
    )(oa, yb, gr, x, wo)


def _rope_tables(pos):
    half = DK_A // 2
    inv = 1.0 / ROPE_THETA ** (np.arange(half, dtype=np.float64) / half)
    ang = inv[:, None] * pos.astype(np.float64)[None, :]
    return jnp.asarray(np.cos(ang), F32), jnp.asarray(np.sin(ang), F32)


def kernel(x_prompt, x_sample, cache_k, cache_v, state_wkv, state_shift, page_table, ln_g, w_in, q_norm_g,
           k_norm_g, lambda_q1, lambda_k1, lambda_q2, lambda_k2, subln_g, shift_mu, w0, w_lora_up, a0,
           a_lora_up, k_k, k_a, r_k, gn_g, gn_b, w_out):
    n_b, seq = x_prompt.shape[:2]
    n_d, n_dec = x_sample.shape[:2]
    page = cache_k.shape[2]
    past = page_table.shape[1] * page
    layer = 0

    w_in_b = w_in[layer].astype(BF16)
    wq_t = w_in_b[:, 0:D_ATT].T
    wk_t = w_in_b[:, D_ATT:2 * D_ATT].T
    w_rest = w_in_b[:, 2 * D_ATT:]
    wo = w_out[layer].astype(BF16)
    wup = w_lora_up[layer].astype(BF16)
    aup = a_lora_up[layer].astype(BF16)
    qg = q_norm_g[layer].reshape(DK_A, 1)
    kg = k_norm_g[layer].reshape(DK_A, 1)
    seg = jnp.arange(D_RWKV) // HS_R
    bd = (seg[:, None] == seg[None, :]).astype(BF16)
    lam_vecs = (lambda_q1[layer][None], lambda_k1[layer][None], lambda_q2[layer][None], lambda_k2[layer][None])
    subg = subln_g[layer][None]
    row = lambda a: a[layer][None]

    pre_params = (row(shift_mu), row(w0), wup, row(a0), aup, row(k_k), row(k_a), row(r_k), bd)

    def mix_long(rw, n_seq, seq_len):
        yb, s_fin = _rwkv_chunk(rw, *pre_params, gn_g[layer], gn_b[layer], n_seq, seq_len)
        s6 = s_fin.reshape(n_seq, H_R // 2, 2, HS_R, 2, HS_R)
        s_heads = jnp.stack([s6[:, :, e, :, e, :] for e in range(2)], axis=2)
        return yb.reshape(n_seq * seq_len, D_RWKV), s_heads.reshape(n_seq, H_R, HS_R, HS_R)

    def mix_short(rw, first, s0, n_seq, seq_len):
        r, lw, k2, vv, nkk, bb, bonus = _rwkv_pre(rw, first, seq_len, *pre_params)
        yb, s_fin = _rwkv_steps(r, lw, k2, vv, nkk, bb, bonus, s0.transpose(1, 2, 3, 0), gn_g[layer],
                                gn_b[layer], n_seq, seq_len)
        return yb, s_fin.transpose(3, 0, 1, 2)

    def project(x, pos, rows, proj_seqs):
        cos_t, sin_t = _rope_tables(pos)
        return _proj(x.reshape(rows, D_MODEL), ln_g[layer][None], wq_t, wk_t, w_rest, cos_t, sin_t, qg, kg,
                     proj_seqs, rows // proj_seqs)

    rows_p = n_b * seq
    rows_s = n_d * n_dec
    q_p, kt_p, ktb_p, v_p, vb_p, ga_p, rw_p, gr_p = project(x_prompt, np.tile(np.arange(seq), n_b), rows_p, n_b)
    q_s, kt_s, _, v_s, _, ga_s, rw_s, gr_s = project(x_sample, np.tile(past + np.arange(n_dec), n_d), rows_s, 1)

    kt_pages = cache_k[layer].transpose(0, 2, 3, 4, 1).reshape(-1, H_A * 2 * DK_A, page)
    v_pages = cache_v[layer].reshape(-1, page * H_A, DV_A)
    q3 = q_s.reshape(n_d, n_dec, D_ATT).astype(F32)
    k_new = kt_s.reshape(D_ATT, rows_s).T.reshape(n_d, n_dec, D_ATT)
    dec_ops = jnp.concatenate([q3, q3, k_new, v_s.reshape(n_d, n_dec, D_ATT), ga_s.reshape(n_d, n_dec, D_ATT)],
                              axis=1)
    oa_s, oa_p = _attn(page_table, dec_ops, n_dec, q_p, ktb_p, vb_p, ga_p, jnp.concatenate(lam_vecs + (subg,), 1),
                       kt_pages, v_pages, n_b, seq)

    yb_p, s_p = mix_long(rw_p, n_b, seq)
    yp = _out_proj(oa_p, yb_p, gr_p, x_prompt.reshape(rows_p, D_MODEL), wo)
    first_s = jnp.repeat(state_shift[layer], n_dec, axis=0)
    yb_s, s_s = mix_short(rw_s, first_s, state_wkv[layer], n_d, n_dec)
    ys = _out_proj(oa_s.reshape(rows_s, D_ATT), yb_s, gr_s, x_sample.reshape(rows_s, D_MODEL), wo)

    k_prompt = kt_p.reshape(n_b, H_A, 2, DK_A, seq).transpose(0, 4, 1, 2, 3)
    k_sample = kt_s.reshape(H_A, 2, DK_A, n_d, n_dec).transpose(3, 4, 0, 1, 2)
    return (yp.reshape(n_b, seq, D_MODEL), ys.reshape(n_d, n_dec, D_MODEL),
            k_prompt[None], v_p.reshape(n_b, seq, H_A, DV_A)[None], s_p[None],
            rw_p.reshape(n_b, seq, SHIFT_W)[:, -1][None],
            k_sample[None], v_s.reshape(n_d, n_dec, H_A, DV_A)[None], s_s[None],
            rw_s.reshape(n_d, n_dec, SHIFT_W)[:, -1][None])
```

```python
import functools
import math

import jax
import jax.numpy as jnp
import numpy as np
from jax import lax
from jax.experimental import pallas as pl
from jax.experimental.pallas import tpu as pltpu

F32 = jnp.float32
BF16 = jnp.bfloat16

D_MODEL = 1024
D_ATT = 512
D_RWKV = 512
DK_A = 64
H_A = 4
DV_A = 128
HS_R = 64
H_R = 8
R_LORA = 64
SHIFT_W = 3 * D_RWKV + 2 * R_LORA
ROPE_THETA = 10000.0
RMS_EPS = 1e-6
GN_EPS = 64e-5
L2_EPS = 1e-12
LAM_INIT = 0.8 - 0.6 * math.exp(-0.3 * 0)

ROW_TILE = 512
OUT_TILE = 512
ATT_TILE = 512
KEY_TILE = 1024
LOG2E = math.log2(math.e)
CHUNK = HS_R
CHUNK_SEQS = 4
PAGES_PER_STEP = 16
VMEM_LIMIT = 56 * 1024 * 1024

_NT = (((1,), (1,)), ((), ()))


def _dot(a, b):
    return jnp.dot(a, b, preferred_element_type=F32)


def _segsum(x, bd):
    hi = x.astype(BF16)
    lo = (x - hi.astype(F32)).astype(BF16)
    return _dot(hi, bd) + _dot(lo, bd)


def _diff_lambda(lq1, lk1, lq2, lk2):
    return (jnp.exp(jnp.sum(lq1 * lk1, axis=1, keepdims=True))
            - jnp.exp(jnp.sum(lq2 * lk2, axis=1, keepdims=True)) + LAM_INIT)


def _subln_gate(o0, o1, lam, subg, gate):
    y = o0 - lam * o1
    y = y * lax.rsqrt(jnp.mean(y * y, axis=-1, keepdims=True) + RMS_EPS) * subg
    return (y * (1.0 - LAM_INIT)) * gate


def _proj_kernel(x_ref, lng_ref, wq_ref, wk_ref, wr_ref, cos_ref, sin_ref, qg_ref, kg_ref,
                 q_ref, kt_ref, ktb_ref, v_ref, vb_ref, ga_ref, rw_ref, gr_ref):
    tm = x_ref.shape[0]
    x = x_ref[...]
    h = x * lax.rsqrt(jnp.mean(x * x, axis=-1, keepdims=True) + RMS_EPS) * lng_ref[...]
    h = h.astype(BF16)
    cos = cos_ref[...][None]
    sin = sin_ref[...][None]

    def norm_rope(t, g):
        t3 = t.reshape(2 * H_A, DK_A, tm)
        t3 = t3 * lax.rsqrt(jnp.mean(t3 * t3, axis=1, keepdims=True) + RMS_EPS) * g[None]
        x1 = t3[:, :DK_A // 2]
        x2 = t3[:, DK_A // 2:]
        o = jnp.concatenate([x1 * cos - x2 * sin, x2 * cos + x1 * sin], axis=1)
        return o.reshape(D_ATT, tm)

    qt = lax.dot_general(wq_ref[...], h, _NT, preferred_element_type=F32)
    qt = norm_rope(qt, qg_ref[...]) * (DK_A ** -0.5 * LOG2E)
    q_ref[...] = qt.T.astype(BF16)
    kt = lax.dot_general(wk_ref[...], h, _NT, preferred_element_type=F32)
    kt = norm_rope(kt, kg_ref[...])
    kt_ref[...] = kt
    ktb_ref[...] = kt.astype(BF16)

    v = _dot(h, wr_ref[:, 0:D_ATT])
    for hd in range(H_A):
        v_ref[:, hd, :] = v[:, hd * DV_A:(hd + 1) * DV_A]
    vb_ref[...] = v.astype(BF16)
    za = _dot(h, wr_ref[:, D_ATT:2 * D_ATT])
    ga_ref[...] = za * jax.nn.sigmoid(za)
    rw_ref[...] = _dot(h, wr_ref[:, 2 * D_ATT:2 * D_ATT + SHIFT_W])
    zr = _dot(h, wr_ref[:, 2 * D_ATT + SHIFT_W:])
    gr_ref[...] = zr * jax.nn.sigmoid(zr)


def _proj(x, ln_g, wq_t, wk_t, w_rest, cos_t, sin_t, qg, kg, n_seq, seq_len):
    rows = x.shape[0]
    tm = ROW_TILE
    nsi = seq_len // tm
    tk = min(KEY_TILE, seq_len)
    per_tk = tk // tm
    const = lambda i: (0, 0)
    row_blk = lambda w: pl.BlockSpec((tm, w), lambda i: (i, 0))
    out_shape = (
        jax.ShapeDtypeStruct((rows, D_ATT), BF16),
        jax.ShapeDtypeStruct((n_seq, D_ATT, seq_len), F32),
        jax.ShapeDtypeStruct((n_seq, seq_len // tk, D_ATT, tk), BF16),
        jax.ShapeDtypeStruct((rows, H_A, DV_A), F32),
        jax.ShapeDtypeStruct((rows, D_ATT), BF16),
        jax.ShapeDtypeStruct((rows, D_ATT), F32),
        jax.ShapeDtypeStruct((rows, SHIFT_W), F32),
        jax.ShapeDtypeStruct((rows, D_RWKV), F32),
    )
    out_specs = (
        row_blk(D_ATT),
        pl.BlockSpec((None, D_ATT, tm), lambda i: (i // nsi, 0, i % nsi)),
        pl.BlockSpec((None, None, D_ATT, tm), lambda i: (i // nsi, (i % nsi) // per_tk, 0, i % per_tk)),
        pl.BlockSpec((tm, H_A, DV_A), lambda i: (i, 0, 0)),
        row_blk(D_ATT), row_blk(D_ATT), row_blk(SHIFT_W), row_blk(D_RWKV),
    )
    in_specs = [
        row_blk(D_MODEL),
        pl.BlockSpec((1, D_MODEL), const),
        pl.BlockSpec(wq_t.shape, const),
        pl.BlockSpec(wk_t.shape, const),
        pl.BlockSpec(w_rest.shape, const),
        pl.BlockSpec((DK_A // 2, tm), lambda i: (0, i)),
        pl.BlockSpec((DK_A // 2, tm), lambda i: (0, i)),
        pl.BlockSpec((DK_A, 1), const),
        pl.BlockSpec((DK_A, 1), const),
    ]
    return pl.pallas_call(
        _proj_kernel, out_shape=out_shape, grid=(rows // tm,), in_specs=in_specs, out_specs=out_specs,
        compiler_params=pltpu.CompilerParams(dimension_semantics=("parallel",), vmem_limit_bytes=VMEM_LIMIT),
        name="proj",
    )(x, ln_g, wq_t, wk_t, w_rest, cos_t, sin_t, qg, kg)


def _attn_kernel(n_dec, pt_ref, dec_ref, qp_ref, ktp_ref, vp_ref, gap_ref, par_ref, *refs):
    npg = PAGES_PER_STEP
    k_refs = refs[:npg]
    v_refs = refs[npg:2 * npg]
    os_ref, op_ref, m_sc, l_sc, acc_sc, qs_sc, m_p, acc_p = refs[2 * npg:]
    i = pl.program_id(2)
    j = pl.program_id(3)
    g = j
    tq = qp_ref.shape[0]
    kn0, vn0, ga0 = 2 * n_dec, 3 * n_dec, 4 * n_dec
    lam_vecs = [par_ref[:, n * DK_A:(n + 1) * DK_A] for n in range(4)]
    subg = par_ref[:, 4 * DK_A:4 * DK_A + DV_A]
    tk = ktp_ref.shape[1]

    @pl.when(j == 0)
    def _():
        m_sc[...] = jnp.full(m_sc.shape, -jnp.inf, F32)
        l_sc[...] = jnp.zeros(l_sc.shape, F32)
        acc_sc[...] = jnp.zeros(acc_sc.shape, F32)
        q = qp_ref[...]
        lane_q = lax.broadcasted_iota(jnp.int32, q.shape, 1)
        zero = jnp.zeros_like(q)
        qs_sc[...] = jnp.concatenate([jnp.where(lane_q < DK_A, q, zero), jnp.where(lane_q >= DK_A, q, zero)],
                                     axis=0)
        m_p[...] = jnp.full(m_p.shape, -jnp.inf, F32)
        acc_p[...] = jnp.zeros(acc_p.shape, F32)

    per_key = tk // tq
    jd = i // per_key

    def prompt_tile(diagonal, width=None):
        width = tk if width is None else width
        sp = _dot(qs_sc[...], ktp_ref[:, :width])
        if diagonal:
            row_p = lax.broadcasted_iota(jnp.int32, sp.shape, 0)
            col_p = lax.broadcasted_iota(jnp.int32, sp.shape, 1)
            q_off = (i % per_key) * tq
            sp = jnp.where(col_p <= jnp.where(row_p >= tq, row_p - tq, row_p) + q_off, sp, -jnp.inf)
        mp_prev = m_p[...]
        mp_new = jnp.maximum(mp_prev, jnp.max(sp, axis=1, keepdims=True))
        alpha_p = jnp.exp2(mp_prev - mp_new)
        pp = jnp.exp2(sp - jnp.tile(mp_new, (1, width // DV_A)))
        v_ext = jnp.concatenate([vp_ref[:width, :], jnp.ones((width, DV_A), BF16)], axis=1)
        acc_p[...] = jnp.tile(alpha_p, (1, 2)) * acc_p[...] + _dot(pp.astype(BF16), v_ext)
        m_p[...] = mp_new

    def prompt_finish():
        acc = acc_p[...]
        o = acc[:, :DV_A] / acc[:, DV_A:]
        lam = _diff_lambda(*lam_vecs)
        op_ref[...] = _subln_gate(o[:tq], o[tq:], lam, subg, gap_ref[...]).astype(BF16)

    o_ref = os_ref
    rowi = lax.broadcasted_iota(jnp.int32, (2 * n_dec, DV_A), 0)
    lane = lax.broadcasted_iota(jnp.int32, (2 * n_dec, DV_A), 1)
    sel = ((rowi < n_dec) & (lane < DK_A)) | ((rowi >= n_dec) & (lane >= DK_A))
    rows_h = 2 * n_dec

    def head_q(h):
        qh = dec_ref[0:2 * n_dec, h * DV_A:(h + 1) * DV_A].astype(BF16)
        return jnp.where(sel, qh, jnp.zeros_like(qh))

    def decode_pages():
        page = k_refs[0].shape[-1]
        s = jnp.concatenate(
            [jnp.concatenate([_dot(head_q(h), k_refs[n][h * 2 * DK_A:(h + 1) * 2 * DK_A, :].astype(BF16))
                              for n in range(npg)], axis=1) for h in range(H_A)], axis=0)
        m_prev = m_sc[...]
        m_new = jnp.maximum(m_prev, jnp.max(s, axis=1, keepdims=True))
        alpha = jnp.exp2(m_prev - m_new)
        p = jnp.exp2(s - m_new)
        l_sc[...] = alpha * l_sc[...] + jnp.sum(p, axis=1, keepdims=True)
        m_sc[...] = m_new
        pb = p.astype(BF16)
        for h in range(H_A):
            acc = alpha[h * rows_h:(h + 1) * rows_h] * acc_sc[h]
            for n in range(npg):
                vh = v_refs[n][pl.ds(h, page, stride=H_A), :]
                acc = acc + _dot(pb[h * rows_h:(h + 1) * rows_h, n * page:(n + 1) * page], vh.astype(BF16))
            acc_sc[h] = acc

    @pl.when(j < jd)
    def _():
        prompt_tile(False)
        decode_pages()

    q_first = (i % per_key) == 0

    @pl.when((j == jd) & q_first)
    def _():
        prompt_tile(True, tq)
        decode_pages()
        prompt_finish()

    @pl.when((j == jd) & jnp.logical_not(q_first))
    def _():
        prompt_tile(True)
        decode_pages()
        prompt_finish()

    @pl.when(j > jd)
    def _():
        decode_pages()

    @pl.when(g == pl.num_programs(3) - 1)
    def _():
        lam = _diff_lambda(*lam_vecs)
        t_idx = jnp.where(rowi[:, :1] >= n_dec, rowi[:, :1] - n_dec, rowi[:, :1])
        for h in range(H_A):
            hs = slice(h * DV_A, (h + 1) * DV_A)
            qf = head_q(h).astype(F32)
            s_new = []
            for j in range(n_dec):
                sj = jnp.sum(qf * dec_ref[kn0 + j:kn0 + j + 1, hs], axis=1, keepdims=True)
                s_new.append(jnp.where(j <= t_idx, sj, -jnp.inf))
            m_prev = m_sc[h * rows_h:(h + 1) * rows_h]
            m_new = m_prev
            for sj in s_new:
                m_new = jnp.maximum(m_new, sj)
            alpha = jnp.exp2(m_prev - m_new)
            l = alpha * l_sc[h * rows_h:(h + 1) * rows_h]
            acc = alpha * acc_sc[h]
            for j, sj in enumerate(s_new):
                pj = jnp.exp2(sj - m_new)
                l = l + pj
                acc = acc + pj * dec_ref[vn0 + j:vn0 + j + 1, hs]
            o = acc / l
            o_ref[:, hs] = _subln_gate(o[:n_dec], o[n_dec:], lam, subg, dec_ref[ga0:ga0 + n_dec, hs])


def _attn(page_table, dec_ops, n_dec, q_p, ktb_p, v_p, ga_p, params, kt_pages, v_pages, n_seq, seq_len):
    n_d, n_pages = page_table.shape
    npg = PAGES_PER_STEP
    page = kt_pages.shape[-1]
    tq = ATT_TILE
    tk = KEY_TILE
    nq = seq_len // tq
    nk = seq_len // tk
    per_key = tk // tq
    assert n_d == n_seq * H_A * nq and n_pages == nk * npg
    dec = lambda b, h, i: (b * H_A + h) * nq + i
    key_tile = lambda i, j: jnp.minimum(j, i // per_key)
    per_d = lambda r: pl.BlockSpec((None, r, D_ATT), lambda b, h, i, j, pt: (dec(b, h, i), 0, 0))
    q_tile = pl.BlockSpec((tq, DV_A), lambda b, h, i, j, pt: (b * nq + i, h))

    page_id = lambda b, h, i, j, pt, n: pt[(dec(b, h, i) * nk + j) * npg + n]

    def k_spec(n):
        return pl.BlockSpec((None, H_A * 2 * DK_A, page),
                            lambda b, h, i, j, pt: (page_id(b, h, i, j, pt, n), 0, 0))

    def v_spec(n):
        return pl.BlockSpec((None, page * H_A, DV_A),
                            lambda b, h, i, j, pt: (page_id(b, h, i, j, pt, n), 0, 0))

    in_specs = ([per_d(dec_ops.shape[1]),
                 q_tile,
                 pl.BlockSpec((None, None, DV_A, tk), lambda b, h, i, j, pt: (b, key_tile(i, j), h, 0)),
                 pl.BlockSpec((tk, DV_A), lambda b, h, i, j, pt: (b * nk + key_tile(i, j), h)),
                 q_tile,
                 pl.BlockSpec(params.shape, lambda b, h, i, j, pt: (0, 0))]
                + [k_spec(n) for n in range(npg)] + [v_spec(n) for n in range(npg)])
    grid_spec = pltpu.PrefetchScalarGridSpec(
        num_scalar_prefetch=1, grid=(n_seq, H_A, nq, nk), in_specs=in_specs,
        out_specs=(per_d(n_dec), q_tile),
        scratch_shapes=[pltpu.VMEM((H_A * 2 * n_dec, 1), F32), pltpu.VMEM((H_A * 2 * n_dec, 1), F32),
                        pltpu.VMEM((H_A, 2 * n_dec, DV_A), F32),
                        pltpu.VMEM((2 * tq, DV_A), BF16), pltpu.VMEM((2 * tq, DV_A), F32),
                        pltpu.VMEM((2 * tq, 2 * DV_A), F32)])
    return pl.pallas_call(
        functools.partial(_attn_kernel, n_dec),
        out_shape=(jax.ShapeDtypeStruct((n_d, n_dec, D_ATT), F32),
                   jax.ShapeDtypeStruct((n_seq * seq_len, D_ATT), BF16)),
        grid_spec=grid_spec,
        compiler_params=pltpu.CompilerParams(
            dimension_semantics=("arbitrary", "arbitrary", "arbitrary", "arbitrary"),
            vmem_limit_bytes=VMEM_LIMIT),
        name="attn",
    )(page_table.reshape(-1), dec_ops, q_p, ktb_p, v_p, ga_p, params,
      *([kt_pages] * npg), *([v_pages] * npg))


def _rwkv_pre_math(rw, prev, mu_ref, w0_ref, wup_ref, a0_ref, aup_ref, kk_ref, ka_ref, rk_ref, bd_ref):
    u = rw + (prev - rw) * mu_ref[...]
    r = u[:, 0:D_RWKV]
    k = u[:, D_RWKV:2 * D_RWKV]
    v = u[:, 2 * D_RWKV:3 * D_RWKV]
    wd = u[:, 3 * D_RWKV:3 * D_RWKV + R_LORA]
    ad = u[:, 3 * D_RWKV + R_LORA:]
    w_log = -jax.nn.softplus(-(w0_ref[...] + _dot(jnp.tanh(wd).astype(BF16), wup_ref[...]))) - 0.5
    log_decay = -jnp.exp(w_log)
    a = jax.nn.sigmoid(a0_ref[...] + _dot(ad.astype(BF16), aup_ref[...]))
    bd = bd_ref[...]
    kk = k * kk_ref[...]
    kk = kk / jnp.maximum(jnp.sqrt(_segsum(kk * kk, bd)), L2_EPS)
    k2 = k * (1.0 + (a - 1.0) * ka_ref[...])
    return r, log_decay, k2, v, -kk, kk * a, _segsum(r * k2 * rk_ref[...], bd) * v


def _rwkv_pre_kernel(seq_len, rw_ref, first_ref, mu_ref, w0_ref, wup_ref, a0_ref, aup_ref,
                     kk_ref, ka_ref, rk_ref, bd_ref,
                     r_out, w_out, k_out, v_out, nkk_out, b_out, bonus_out):
    tm = rw_ref.shape[0]
    rw = rw_ref[...]
    rowi = lax.broadcasted_iota(jnp.int32, (tm, 1), 0)
    prev = jnp.where(rowi % seq_len == 0, first_ref[...], pltpu.roll(rw, 1, axis=0))
    outs = _rwkv_pre_math(rw, prev, mu_ref, w0_ref, wup_ref, a0_ref, aup_ref, kk_ref, ka_ref, rk_ref, bd_ref)
    for ref, val in zip((r_out, w_out, k_out, v_out, nkk_out, b_out, bonus_out), outs):
        ref[...] = val


def _rwkv_pre(rw, first, seq_len, mu, w0, wup, a0, aup, k_k, k_a, r_k, bd):
    rows = rw.shape[0]
    tm = ROW_TILE
    assert tm % seq_len == 0
    const = lambda i: (0, 0)
    row_blk = lambda w: pl.BlockSpec((tm, w), lambda i: (i, 0))
    in_specs = [
        row_blk(SHIFT_W),
        row_blk(SHIFT_W),
        pl.BlockSpec((1, SHIFT_W), const),
        pl.BlockSpec((1, D_RWKV), const), pl.BlockSpec((R_LORA, D_RWKV), const),
        pl.BlockSpec((1, D_RWKV), const), pl.BlockSpec((R_LORA, D_RWKV), const),
        pl.BlockSpec((1, D_RWKV), const), pl.BlockSpec((1, D_RWKV), const), pl.BlockSpec((1, D_RWKV), const),
        pl.BlockSpec((D_RWKV, D_RWKV), const),
    ]
    out = jax.ShapeDtypeStruct((rows, D_RWKV), F32)
    return pl.pallas_call(
        functools.partial(_rwkv_pre_kernel, seq_len),
        out_shape=(out,) * 7, grid=(rows // tm,), in_specs=in_specs, out_specs=(row_blk(D_RWKV),) * 7,
        compiler_params=pltpu.CompilerParams(dimension_semantics=("parallel",), vmem_limit_bytes=VMEM_LIMIT),
        name="rwkv_pre",
    )(rw, first, mu, w0, wup, a0, aup, k_k, k_a, r_k, bd)


def _rwkv_steps_kernel(r_ref, lw_ref, k_ref, v_ref, nkk_ref, b_ref, bonus_ref, s0_ref, gng_ref, gnb_ref,
                       y_ref, s_ref, xt_sc, y_sc):
    n_b = s0_ref.shape[-1]
    n_t = r_ref.shape[0] // n_b
    for a, ref in enumerate((r_ref, lw_ref, k_ref, v_ref, nkk_ref, b_ref)):
        for t in range(n_t):
            xt_sc[a, t] = ref[pl.ds(t, n_b, stride=n_t), :].T
    s_ref[...] = s0_ref[...]

    for t in range(n_t):
        for e in range(2):
            hk = slice(e * HS_R, (e + 1) * HS_R)
            r_e = xt_sc[0, t, hk]
            w_e = jnp.exp(xt_sc[1, t, hk])
            k_e = xt_sc[2, t, hk]
            nk_e = xt_sc[4, t, hk]
            b_e = xt_sc[5, t, hk]

            def rows(g, carry):
                v0 = pl.multiple_of(g * 8, 8)
                v8 = xt_sc[3, t, pl.ds(e * HS_R + v0, 8), :]
                ys = []
                for i in range(8):
                    s = s_ref[e, v0 + i]
                    sa = jnp.sum(s * nk_e, axis=0, keepdims=True)
                    s = s * w_e + sa * b_e + v8[i:i + 1] * k_e
                    s_ref[e, v0 + i] = s
                    ys.append(jnp.sum(s * r_e, axis=0, keepdims=True))
                y_sc[pl.ds(e * HS_R + v0, 8), :] = jnp.concatenate(ys, axis=0)
                return carry

            lax.fori_loop(0, HS_R // 8, rows, 0)

        y = y_sc[...]
        halves = []
        for e in range(2):
            yh = y[e * HS_R:(e + 1) * HS_R]
            d = yh - jnp.mean(yh, axis=0, keepdims=True)
            halves.append(d * lax.rsqrt(jnp.mean(d * d, axis=0, keepdims=True) + GN_EPS))
        y = jnp.concatenate(halves, axis=0) * gng_ref[...] + gnb_ref[...]
        y_ref[pl.ds(t, n_b, stride=n_t), :] = y.T + bonus_ref[pl.ds(t, n_b, stride=n_t), :]


def _rwkv_steps(r, lw, k, v, nkk, b, bonus, s0, gn_g, gn_b, n_seq, seq_len):
    rows = n_seq * seq_len
    pw = 2 * HS_R
    n_pair = H_R // 2
    in_blk = pl.BlockSpec((rows, pw), lambda p: (0, p))
    st_blk = pl.BlockSpec((2, HS_R, HS_R, n_seq), lambda p: (p, 0, 0, 0))
    gn_blk = pl.BlockSpec((None, pw, 1), lambda p: (p, 0, 0))
    return pl.pallas_call(
        _rwkv_steps_kernel,
        out_shape=(jax.ShapeDtypeStruct((rows, D_RWKV), F32), jax.ShapeDtypeStruct(s0.shape, F32)),
        grid=(n_pair,),
        in_specs=[in_blk] * 7 + [st_blk, gn_blk, gn_blk],
        out_specs=(in_blk, st_blk),
        scratch_shapes=[pltpu.VMEM((6, seq_len, pw, n_seq), F32), pltpu.VMEM((pw, n_seq), F32)],
        compiler_params=pltpu.CompilerParams(dimension_semantics=("parallel",), vmem_limit_bytes=VMEM_LIMIT),
        name="rwkv_steps",
    )(r, lw, k, v, nkk, b, bonus, s0, gn_g.reshape(n_pair, pw, 1), gn_b.reshape(n_pair, pw, 1))


def _rwkv_chunk_kernel(rw_ref, mu_ref, w0_ref, wup_ref, a0_ref, aup_ref, kk_ref, ka_ref, rk_ref, bd_ref,
                       gng_ref, gnb_ref, y_ref, s_ref, last_sc):
    nb, C = rw_ref.shape[0], rw_ref.shape[1]
    n_pair = H_R // 2
    pw = 2 * HS_R
    c = pl.program_id(1)

    @pl.when(c == 0)
    def _():
        s_ref[...] = jnp.zeros(s_ref.shape, F32)
        last_sc[...] = jnp.zeros(last_sc.shape, F32)

    rw = rw_ref[...].reshape(nb * C, SHIFT_W)
    first = jnp.concatenate([jnp.broadcast_to(last_sc[b:b + 1, :], (C, SHIFT_W)) for b in range(nb)], axis=0)
    row_in_chunk = lax.broadcasted_iota(jnp.int32, (nb * C, 1), 0) % C
    prev = jnp.where(row_in_chunk == 0, first, pltpu.roll(rw, 1, axis=0))
    for b in range(nb):
        last_sc[b:b + 1, :] = rw[(b + 1) * C - 1:(b + 1) * C, :]
    r_all, lw_all, k_all, v_all, nkk_all, b_all, bonus = _rwkv_pre_math(
        rw, prev, mu_ref, w0_ref, wup_ref, a0_ref, aup_ref, kk_ref, ka_ref, rk_ref, bd_ref)

    lane = lax.broadcasted_iota(jnp.int32, (C, pw), 1)
    t_idx = lax.broadcasted_iota(jnp.int32, (C, pw), 0)
    head0 = lane < HS_R
    s_idx = lane % C
    strict = t_idx > s_idx
    incl = t_idx >= s_idx
    eye = jnp.where(t_idx == s_idx, 1.0, 0.0).astype(F32)
    tri = jnp.where(lax.broadcasted_iota(jnp.int32, (C, C), 0) >= lax.broadcasted_iota(jnp.int32, (C, C), 1),
                    1.0, 0.0).astype(BF16)
    same_head = (lax.broadcasted_iota(jnp.int32, (pw, pw), 0) // HS_R
                 == lax.broadcasted_iota(jnp.int32, (pw, pw), 1) // HS_R)

    def stack(x):
        return jnp.concatenate([jnp.where(head0, x, 0.0), jnp.where(head0, 0.0, x)], axis=0).astype(BF16)

    def per_head(coef, x):
        return _dot(coef.astype(BF16), stack(x))

    def head_mean(x):
        return jnp.where(head0, jnp.sum(jnp.where(head0, x, 0.0), axis=1, keepdims=True),
                         jnp.sum(jnp.where(head0, 0.0, x), axis=1, keepdims=True)) * (1.0 / HS_R)

    units = [(b, p) for b in range(nb) for p in range(n_pair)]
    ls = lambda p: slice(p * pw, (p + 1) * pw)
    take = lambda arr: [arr[b * C:(b + 1) * C, ls(p)] for b, p in units]
    lw = take(lw_all)
    lw_hi = [x.astype(BF16) for x in lw]
    cum = [_dot(tri, h) + _dot(tri, (x - h.astype(F32)).astype(BF16)) for x, h in zip(lw, lw_hi)]
    p_in = [jnp.exp(x) for x in cum]
    p_inv = [jnp.exp(-x) for x in cum]
    nt = [x * jnp.exp(cm - l) for x, cm, l in zip(take(nkk_all), cum, lw)]
    bt = [x * pi for x, pi in zip(take(b_all), p_inv)]
    kt = [x * pi for x, pi in zip(take(k_all), p_inv)]
    rt = [x * pp for x, pp in zip(take(r_all), p_in)]
    vv = take(v_all)

    lhs = [jnp.concatenate([n_, r_], axis=0).astype(BF16) for n_, r_ in zip(nt, rt)]
    coef = [lax.dot_general(l_, jnp.concatenate([stack(b_), stack(k_)], axis=0), _NT,
                            preferred_element_type=F32) for l_, b_, k_ in zip(lhs, bt, kt)]
    a_pow = [jnp.where(strict, cf[:C, :pw], 0.0) for cf in coef]
    b_m = [jnp.where(strict, cf[:C, pw:], 0.0) for cf in coef]
    ab_m = [jnp.where(incl, cf[C:, :pw], 0.0) for cf in coef]
    ak_m = [jnp.where(incl, cf[C:, pw:], 0.0) for cf in coef]

    inv = [eye + a for a in a_pow]
    for _ in range(C.bit_length() - 2):
        a_pow = [per_head(a, a) for a in a_pow]
        inv = [per_head(i_, eye + a) for i_, a in zip(inv, a_pow)]

    v_stack = [stack(x) for x in vv]
    bv = [_dot(m.astype(BF16), vs) for m, vs in zip(b_m, v_stack)]
    akv = [_dot(m.astype(BF16), vs) for m, vs in zip(ak_m, v_stack)]

    st = [s_ref[b, p] for b, p in units]
    ns_rs = [lax.dot_general(l_, s_.astype(BF16), _NT, preferred_element_type=F32)
             for l_, s_ in zip(lhs, st)]
    u = [per_head(i_, x[:C] + bv_) for i_, x, bv_ in zip(inv, ns_rs, bv)]
    y = [x[C:] + per_head(m, u_) + akv_ for x, m, u_, akv_ in zip(ns_rs, ab_m, u, akv)]

    for i, (b, p) in enumerate(units):
        uv_t = jnp.concatenate([u[i], vv[i]], axis=0).T.astype(BF16)
        upd = _dot(uv_t, jnp.concatenate([bt[i], kt[i]], axis=0).astype(BF16))
        s_ref[b, p] = jnp.where(same_head, (st[i] + upd) * p_in[i][C - 1:C, :], 0.0)

    for i, (b, p) in enumerate(units):
        d = y[i] - head_mean(y[i])
        var = head_mean(d * d)
        y_ref[b, :, ls(p)] = (d * lax.rsqrt(var + GN_EPS) * gng_ref[:, ls(p)] + gnb_ref[:, ls(p)]
                              + bonus[b * C:(b + 1) * C, ls(p)])


def _rwkv_chunk(rw, mu, w0, wup, a0, aup, k_k, k_a, r_k, bd, gn_g, gn_b, n_seq, seq_len):
    nb = CHUNK_SEQS
    C = CHUNK
    n_pair = H_R // 2
    const = lambda g, c: (0, 0)
    vec = lambda w: pl.BlockSpec((1, w), const)
    out_blk = pl.BlockSpec((nb, C, D_RWKV), lambda g, c: (g, c, 0))
    in_specs = [pl.BlockSpec((nb, C, SHIFT_W), lambda g, c: (g, c, 0)), vec(SHIFT_W),
                vec(D_RWKV), pl.BlockSpec((R_LORA, D_RWKV), const),
                vec(D_RWKV), pl.BlockSpec((R_LORA, D_RWKV), const),
                vec(D_RWKV), vec(D_RWKV), vec(D_RWKV), pl.BlockSpec((D_RWKV, D_RWKV), const),
                vec(D_RWKV), vec(D_RWKV)]
    seq3 = jax.ShapeDtypeStruct((n_seq, seq_len, D_RWKV), F32)
    return pl.pallas_call(
        _rwkv_chunk_kernel,
        out_shape=(seq3, jax.ShapeDtypeStruct((n_seq, n_pair, 2 * HS_R, 2 * HS_R), F32)),
        grid=(n_seq // nb, seq_len // C),
        in_specs=in_specs,
        out_specs=(out_blk, pl.BlockSpec((nb, n_pair, 2 * HS_R, 2 * HS_R), lambda g, c: (g, 0, 0, 0))),
        scratch_shapes=[pltpu.VMEM((nb, SHIFT_W), F32)],
        compiler_params=pltpu.CompilerParams(dimension_semantics=("parallel", "arbitrary"),
                                             vmem_limit_bytes=VMEM_LIMIT),
        name="rwkv_chunk",
    )(rw.reshape(n_seq, seq_len, SHIFT_W), mu, w0, wup, a0, aup, k_k, k_a, r_k, bd, gn_g[None], gn_b[None])


def _out_proj_kernel(oa_ref, yb_ref, gr_ref, x_ref, wo_ref, y_ref):
    mix_r = (yb_ref[...] * gr_ref[...]).astype(BF16)
    y_ref[...] = (x_ref[...] + _dot(oa_ref[...].astype(BF16), wo_ref[0:D_ATT, :])
                  + _dot(mix_r, wo_ref[D_ATT:, :]))


def _out_proj(oa, yb, gr, x, wo):
    rows = x.shape[0]
    tm = OUT_TILE
    row_blk = lambda w: pl.BlockSpec((tm, w), lambda i: (i, 0))
    return pl.pallas_call(
        _out_proj_kernel, out_shape=jax.ShapeDtypeStruct((rows, D_MODEL), F32), grid=(rows // tm,),
        in_specs=[row_blk(D_ATT), row_blk(D_RWKV), row_blk(D_RWKV), row_blk(D_MODEL),
                  pl.BlockSpec(wo.shape, lambda i: (0, 0))],
        out_specs=row_blk(D_MODEL),
        compiler_params=pltpu.CompilerParams(dimension_semantics=("parallel",), vmem_limit_bytes=VMEM_LIMIT),
        name="out_proj",
    )(oa, yb, gr, x, wo)


def _rope_tables(pos):
    half = DK_A // 2
    inv = 1.0 / ROPE_THETA ** (np.arange(half, dtype=np.float64) / half)
    ang = inv[:, None] * pos.astype(np.float64)[None, :]
    return jnp.asarray(np.cos(ang), F32), jnp.asarray(np.sin(ang), F32)


def kernel(x_prompt, x_sample, cache_k, cache_v, state_wkv, state_shift, page_table, ln_g, w_in, q_norm_g,
           k_norm_g, lambda_q1, lambda_k1, lambda_q2, lambda_k2, subln_g, shift_mu, w0, w_lora_up, a0,
           a_lora_up, k_k, k_a, r_k, gn_g, gn_b, w_out):
    n_b, seq = x_prompt.shape[:2]
    n_d, n_dec = x_sample.shape[:2]
    page = cache_k.shape[2]
    past = page_table.shape[1] * page
    layer = 0

    w_in_b = w_in[layer].astype(BF16)
    wq_t = w_in_b[:, 0:D_ATT].T
    wk_t = w_in_b[:, D_ATT:2 * D_ATT].T
    w_rest = w_in_b[:, 2 * D_ATT:]
    wo = w_out[layer].astype(BF16)
    wup = w_lora_up[layer].astype(BF16)
    aup = a_lora_up[layer].astype(BF16)
    qg = q_norm_g[layer].reshape(DK_A, 1)
    kg = k_norm_g[layer].reshape(DK_A, 1)
    seg = jnp.arange(D_RWKV) // HS_R
    bd = (seg[:, None] == seg[None, :]).astype(BF16)
    lam_vecs = (lambda_q1[layer][None], lambda_k1[layer][None], lambda_q2[layer][None], lambda_k2[layer][None])
    subg = subln_g[layer][None]
    row = lambda a: a[layer][None]

    pre_params = (row(shift_mu), row(w0), wup, row(a0), aup, row(k_k), row(k_a), row(r_k), bd)

    def mix_long(rw, n_seq, seq_len):
        yb, s_fin = _rwkv_chunk(rw, *pre_params, gn_g[layer], gn_b[layer], n_seq, seq_len)
        s6 = s_fin.reshape(n_seq, H_R // 2, 2, HS_R, 2, HS_R)
        s_heads = jnp.stack([s6[:, :, e, :, e, :] for e in range(2)], axis=2)
        return yb.reshape(n_seq * seq_len, D_RWKV), s_heads.reshape(n_seq, H_R, HS_R, HS_R)

    def mix_short(rw, first, s0, n_seq, seq_len):
        r, lw, k2, vv, nkk, bb, bonus = _rwkv_pre(rw, first, seq_len, *pre_params)
        yb, s_fin = _rwkv_steps(r, lw, k2, vv, nkk, bb, bonus, s0.transpose(1, 2, 3, 0), gn_g[layer],
                                gn_b[layer], n_seq, seq_len)
        return yb, s_fin.transpose(3, 0, 1, 2)

    def project(x, pos, rows, proj_seqs):
        cos_t, sin_t = _rope_tables(pos)
        return _proj(x.reshape(rows, D_MODEL), ln_g[layer][None], wq_t, wk_t, w_rest, cos_t, sin_t, qg, kg,
                     proj_seqs, rows // proj_seqs)

    rows_p = n_b * seq
    rows_s = n_d * n_dec
    q_p, kt_p, ktb_p, v_p, vb_p, ga_p, rw_p, gr_p = project(x_prompt, np.tile(np.arange(seq), n_b), rows_p, n_b)
    q_s, kt_s, _, v_s, _, ga_s, rw_s, gr_s = project(x_sample, np.tile(past + np.arange(n_dec), n_d), rows_s, 1)

    kt_pages = cache_k[layer].transpose(0, 2, 3, 4, 1).reshape(-1, H_A * 2 * DK_A, page)
    v_pages = cache_v[layer].reshape(-1, page * H_A, DV_A)
    q3 = q_s.reshape(n_d, n_dec, D_ATT).astype(F32)
    k_new = kt_s.reshape(D_ATT, rows_s).T.reshape(n_d, n_dec, D_ATT)
    dec_ops = jnp.concatenate([q3, q3, k_new, v_s.reshape(n_d, n_dec, D_ATT), ga_s.reshape(n_d, n_dec, D_ATT)],
                              axis=1)
    oa_s, oa_p = _attn(page_table, dec_ops, n_dec, q_p, ktb_p, vb_p, ga_p, jnp.concatenate(lam_vecs + (subg,), 1),
                       kt_pages, v_pages, n_b, seq)

    yb_p, s_p = mix_long(rw_p, n_b, seq)
    yp = _out_proj(oa_p, yb_p, gr_p, x_prompt.reshape(rows_p, D_MODEL), wo)
    first_s = jnp.repeat(state_shift[layer], n_dec, axis=0)
    yb_s, s_s = mix_short(rw_s, first_s, state_wkv[layer], n_d, n_dec)
    ys = _out_proj(oa_s.reshape(rows_s, D_ATT), yb_s, gr_s, x_sample.reshape(rows_s, D_MODEL), wo)

    k_prompt = kt_p.reshape(n_b, H_A, 2, DK_A, seq).transpose(0, 4, 1, 2, 3)
    k_sample = kt_s.reshape(H_A, 2, DK_A, n_d, n_dec).transpose(3, 4, 0, 1, 2)
    return (yp.reshape(n_b, seq, D_MODEL), ys.reshape(n_d, n_dec, D_MODEL),
            k_prompt[None], v_p.reshape(n_b, seq, H_A, DV_A)[None], s_p[None],
            rw_p.reshape(n_b, seq, SHIFT_W)[:, -1][None],
            k_sample[None], v_s.reshape(n_d, n_dec, H_A, DV_A)[None], s_s[None],
            rw_s.reshape(n_d, n_dec, SHIFT_W)[:, -1][None])
```
